```python
import math
import jax
import jax.numpy as jnp
from jax import lax
import numpy as np

D_MODEL = 1024
BATCH = 16
SEQ = 256
DEPTH = 4
DEC_BATCH = 2
DEC_SEQ = 4096
PAST_LEN = 512

GRID_W = 64
N_EVEN = (DEPTH + 1) // 2
N_ODD = DEPTH // 2
EPS = 1e-6
F32 = jnp.float32

S5_WIDTH = D_MODEL // 2
S5_GROUP = 16
S5_GROUPS = S5_WIDTH // S5_GROUP
S5_STATE = 64
ATT_HEAD_DIM = 64
ATT_HEADS = (D_MODEL // 2) // ATT_HEAD_DIM
ATT_KV_HEADS = ATT_HEADS // 4
ATT_GROUP = ATT_HEADS // ATT_KV_HEADS
ATT_Q_WIDTH = ATT_HEADS * ATT_HEAD_DIM
ATT_KV_WIDTH = ATT_KV_HEADS * ATT_HEAD_DIM
WINDOW = 128
BLOCK = 128
ROPE_BASE = 10000.0
AB_IN = S5_WIDTH + ATT_Q_WIDTH + 2 * ATT_KV_WIDTH
AB_OUT = S5_WIDTH + ATT_Q_WIDTH
HGRN_WIDTH = D_MODEL // 2
HGRN_HEAD_DIM = 128
HGRN_HEADS = HGRN_WIDTH // HGRN_HEAD_DIM
RET_WIDTH = D_MODEL // 2
RET_HEAD_DIM = 128
RET_HEADS = RET_WIDTH // RET_HEAD_DIM
CHUNK = 64
CD_IN = 5 * HGRN_WIDTH + 4 * RET_WIDTH
CD_OUT = HGRN_WIDTH + RET_WIDTH
N_EXPERTS = 16
EXPERT_FF = 1536
CAPACITY_FACTOR = 2

kernel_name = 'hybrid_dit_s5_swa_hgrn2_retnet_ec'


def rmsnorm(x, w):
    xf = x.astype(F32)
    y = xf * lax.rsqrt(jnp.mean(xf * xf, axis=-1, keepdims=True) + EPS)
    return (y * w.astype(F32)).astype(x.dtype)


def adaln(cond, w, b):
    mod = jax.nn.silu(cond) @ w + b
    return jnp.split(mod[:, None, :], 6, axis=-1)


def grid_rope(n_tokens, dim):
    n_rows = n_tokens // GRID_W
    rows = jnp.repeat(jnp.arange(n_rows, dtype=F32), GRID_W)
    cols = (jnp.arange(n_tokens) % GRID_W).astype(F32)
    nf = dim // 4
    inv = ROPE_BASE ** (-jnp.arange(nf, dtype=F32) / nf)
    ang = jnp.concatenate([rows[:, None] * inv, cols[:, None] * inv], axis=-1)
    return jnp.cos(ang), jnp.sin(ang)


def apply_rope(x, cos, sin):
    xf = x.astype(F32)
    half = x.shape[-1] // 2
    x1, x2 = xf[..., :half], xf[..., half:]
    c = cos[None, :, None, :]
    s = sin[None, :, None, :]
    return jnp.concatenate([x1 * c - x2 * s, x2 * c + x1 * s], axis=-1).astype(x.dtype)


def sink_attention(q, keys, values, masks, sink):
    scale = q.shape[-1] ** -0.5
    logits = []
    for kk, m in zip(keys, masks):
        s = jnp.einsum('bqkgd,bskd->bkgqs', q, kk).astype(F32) * scale
        if m is not None:
            s = jnp.where(m, s, -1e30)
        logits.append(s)
    b, nq, nkv, g, _ = q.shape
    sink_col = jnp.broadcast_to(sink.astype(F32)[None, :, :, None, None], (b, nkv, g, nq, 1))
    probs = jax.nn.softmax(jnp.concatenate(logits + [sink_col], axis=-1), axis=-1)
    out = None
    start = 0
    for vv, s in zip(values, logits):
        p = probs[..., start:start + s.shape[-1]].astype(vv.dtype)
        o = jnp.einsum('bkgqs,bskd->bqkgd', p, vv)
        out = o if out is None else out + o
        start += s.shape[-1]
    return out


def context_attention(q, k, v, sink):
    b, n = q.shape[:2]
    qb = q.reshape(b, n // BLOCK, BLOCK, ATT_KV_HEADS, ATT_GROUP, ATT_HEAD_DIM).transpose(1, 0, 2, 3, 4, 5)
    out = lax.map(lambda qq: sink_attention(qq, [k], [v], [None], sink), qb)
    return out.transpose(1, 0, 2, 3, 4, 5).reshape(b, n, ATT_Q_WIDTH)


def window_attention_latent(q, k, v, k_ctx, v_ctx, sink):
    b, n = q.shape[:2]
    nb = n // BLOCK
    qb = q.reshape(b, nb, BLOCK, ATT_KV_HEADS, ATT_GROUP, ATT_HEAD_DIM).transpose(1, 0, 2, 3, 4, 5)

    def bands(t):
        tp = jnp.pad(t, ((0, 0), (BLOCK, BLOCK), (0, 0), (0, 0))).reshape(b, nb + 2, BLOCK, ATT_KV_HEADS, ATT_HEAD_DIM)
        return jnp.concatenate([tp[:, :-2], tp[:, 1:-1], tp[:, 2:]], axis=2).transpose(1, 0, 2, 3, 4)

    q_pos = jnp.arange(nb)[:, None] * BLOCK + jnp.arange(BLOCK)[None, :]
    k_pos = (jnp.arange(nb)[:, None] - 1) * BLOCK + jnp.arange(3 * BLOCK)[None, :]
    rel = k_pos[:, None, :] - q_pos[:, :, None]
    mask = (jnp.abs(rel) <= WINDOW) & (k_pos[:, None, :] >= 0) & (k_pos[:, None, :] < n)

    def block_fn(args):
        qq, kk, vv, mm = args
        return sink_attention(qq, [kk, k_ctx], [vv, v_ctx], [mm, None], sink)

    out = lax.map(block_fn, (qb, bands(k), bands(v), mask))
    return out.transpose(1, 0, 2, 3, 4, 5).reshape(b, n, ATT_Q_WIDTH)


def s5_op(e1, e2):
    a1r, a1i, b1r, b1i = e1
    a2r, a2i, b2r, b2i = e2
    return (a2r * a1r - a2i * a1i, a2r * a1i + a2i * a1r,
            a2r * b1r - a2i * b1i + b2r, a2r * b1i + a2i * b1r + b2i)


def s5_mixer(u, lam_re, lam_im, log_dt, b_re, b_im, c_re, c_im, d_skip, glu_w, h0):
    b, n, _ = u.shape
    uf = u.astype(F32)
    ug = uf.reshape(b, n, S5_GROUPS, S5_GROUP)
    y = uf * d_skip.astype(F32)
    finals = []
    for dr in range(2):
        lr = lam_re[dr].astype(F32)
        li = lam_im[dr].astype(F32)
        dt = jnp.exp(log_dt[dr].astype(F32))[:, None]
        mag = jnp.exp(lr * dt)
        ar, ai = mag * jnp.cos(li * dt), mag * jnp.sin(li * dt)
        den = lr * lr + li * li
        cr = ((ar - 1.0) * lr + ai * li) / den
        ci = (ai * lr - (ar - 1.0) * li) / den
        br, bi = b_re[dr].astype(F32), b_im[dr].astype(F32)
        bbr = cr[..., None] * br - ci[..., None] * bi
        bbi = cr[..., None] * bi + ci[..., None] * br
        ud = ug if dr == 0 else jnp.flip(ug, axis=1)
        bu_r = jnp.einsum('gpc,bngc->bngp', bbr, ud)
        bu_i = jnp.einsum('gpc,bngc->bngp', bbi, ud)
        shp = bu_r.shape
        acr, aci, xr, xi = lax.associative_scan(
            s5_op, (jnp.broadcast_to(ar, shp), jnp.broadcast_to(ai, shp), bu_r, bu_i), axis=1)
        h0r = h0[:, dr, 0].astype(F32)[:, None]
        h0i = h0[:, dr, 1].astype(F32)[:, None]
        xr, xi = xr + acr * h0r - aci * h0i, xi + acr * h0i + aci * h0r
        finals.append(jnp.stack([xr[:, -1], xi[:, -1]], axis=1))
        yd = (jnp.einsum('gcp,bngp->bngc', c_re[dr].astype(F32), xr)
              - jnp.einsum('gcp,bngp->bngc', c_im[dr].astype(F32), xi))
        if dr == 1:
            yd = jnp.flip(yd, axis=1)
        y = y + yd.reshape(b, n, S5_WIDTH)
    y = jax.nn.gelu(y)
    out = y * jax.nn.sigmoid(y @ glu_w.astype(F32))
    return out.astype(u.dtype), jnp.stack(finals, axis=1)


def chunked_recurrence(q, k, v, logf, s0):
    b, n, h, _ = q.shape
    dv = v.shape[-1]
    nc = n // CHUNK

    def chunks(t):
        return t.astype(F32).reshape(b, nc, CHUNK, h, t.shape[-1]).transpose(1, 0, 2, 3, 4)

    tri = jnp.tril(jnp.ones((CHUNK, CHUNK), dtype=bool))[None, :, :, None, None]

    def step(state, inp):
        qc, kc, vc, fc = inp
        cum = jnp.cumsum(fc, axis=1)
        o_inter = jnp.einsum('blhk,bhkv->blhv', qc * jnp.exp(cum), state)
        decay = jnp.exp(jnp.where(tri, cum[:, :, None] - cum[:, None, :], -jnp.inf))
        scores = jnp.einsum('bthk,bshk,btshk->bhts', qc, kc, decay)
        o_intra = jnp.einsum('bhts,bshv->bthv', scores, vc)
        last = cum[:, -1]
        k_dec = kc * jnp.exp(last[:, None] - cum)
        new_state = jnp.exp(last)[..., None] * state + jnp.einsum('bshk,bshv->bhkv', k_dec, vc)
        return new_state, o_inter + o_intra

    state, o = lax.scan(step, s0.astype(F32), (chunks(q), chunks(k), chunks(v), chunks(logf)))
    return o.transpose(1, 0, 2, 3, 4).reshape(b, n, h, dv), state


def bidirectional_recurrence(q, ks, v, logfs, s0):
    o_f, s_f = chunked_recurrence(q, ks[0], v, logfs[0], s0[:, 0])
    o_b, s_b = chunked_recurrence(jnp.flip(q, 1), jnp.flip(ks[1], 1), jnp.flip(v, 1),
                                  jnp.flip(logfs[1], 1), s0[:, 1])
    return o_f + jnp.flip(o_b, 1), jnp.stack([s_f, s_b], axis=1)


def head_rmsnorm(o, w):
    h, d = o.shape[-2:]
    return o * lax.rsqrt(jnp.mean(o * o, -1, keepdims=True) + EPS) * w.astype(F32).reshape(h, d)


def head_layernorm(o, w):
    h, d = o.shape[-2:]
    oc = o - jnp.mean(o, -1, keepdims=True)
    return oc * lax.rsqrt(jnp.mean(oc * oc, -1, keepdims=True) + EPS) * w.astype(F32).reshape(h, d)


def hgrn_ret_mixer(h, w_in, w_out, lb, hgrn_norm_w, ret_decay, ret_norm_w, rope, s_hgrn0, s_ret0):
    b, n, _ = h.shape
    cq, cf_fwd, cf_bwd, ci, cg, rq, rk, rv, rg = jnp.split(h @ w_in, 9, axis=-1)

    def heads(t, nh, hd):
        return t.reshape(b, n, nh, hd)

    q_c = heads(cq, HGRN_HEADS, HGRN_HEAD_DIM).astype(F32)
    v_c = heads(ci, HGRN_HEADS, HGRN_HEAD_DIM).astype(F32)
    ks_c, logfs_c = [], []
    for dr, fz in enumerate((cf_fwd, cf_bwd)):
        lbd = lb[dr].reshape(HGRN_HEADS, HGRN_HEAD_DIM)
        zf = heads(fz, HGRN_HEADS, HGRN_HEAD_DIM).astype(F32)
        logf = jnp.logaddexp(jnp.log1p(-lbd) + jax.nn.log_sigmoid(zf), jnp.log(lbd))
        logfs_c.append(logf)
        ks_c.append(-jnp.expm1(logf))
    o_c, s_c = bidirectional_recurrence(q_c, ks_c, v_c, logfs_c, s_hgrn0)
    o_c = head_rmsnorm(o_c, hgrn_norm_w).reshape(b, n, HGRN_WIDTH) * jax.nn.sigmoid(cg.astype(F32))
    q_r = heads(rq, RET_HEADS, RET_HEAD_DIM)
    k_r = heads(rk, RET_HEADS, RET_HEAD_DIM)
    if rope is not None:
        q_r = apply_rope(q_r, *rope)
        k_r = apply_rope(k_r, *rope)
    k_r = k_r.astype(F32) * RET_HEAD_DIM ** -0.5
    logfs_r = [jnp.broadcast_to(-jnp.exp(ret_decay[dr].astype(F32))[:, None], (b, n, RET_HEADS, RET_HEAD_DIM))
               for dr in range(2)]
    o_r, s_r = bidirectional_recurrence(q_r.astype(F32), [k_r, k_r],
                                        heads(rv, RET_HEADS, RET_HEAD_DIM).astype(F32), logfs_r, s_ret0)
    o_r = head_layernorm(o_r, ret_norm_w).reshape(b, n, RET_WIDTH) * jax.nn.silu(rg.astype(F32))
    y = jnp.concatenate([o_c, o_r], axis=-1).astype(h.dtype) @ w_out
    return y, s_c, s_r


def expert_choice_ffn(h, w_router, w1, w3, w2):
    b, n, d = h.shape
    cap = CAPACITY_FACTOR * n // N_EXPERTS
    affinity = jax.nn.softmax((h @ w_router).astype(F32), axis=-1)
    gate, idx = lax.top_k(affinity.transpose(0, 2, 1), cap)
    xs = jax.vmap(lambda hb, ib: hb[ib])(h, idx)
    hid = jax.nn.silu(jnp.einsum('becd,edf->becf', xs, w1)) * jnp.einsum('becd,edf->becf', xs, w3)
    ys = jnp.einsum('becf,efd->becd', hid, w2) * gate[..., None].astype(h.dtype)
    return jax.vmap(lambda yb, ib: jax.ops.segment_sum(yb.reshape(-1, d), ib.reshape(-1), num_segments=n))(ys, idx)


def trunk(x, cond, prm, ctx):
    latent = ctx is not None
    b, n, _ = x.shape
    if latent:
        rope_att = grid_rope(n, ATT_HEAD_DIM)
        rope_ret = grid_rope(n, RET_HEAD_DIM)
    lb_all = jnp.cumsum(jax.nn.softmax(prm['hgrn_lb_logits'].astype(F32), axis=0), axis=0)
    lb_all = lb_all - lb_all[:1]
    s5_out, k_out, v_out, hgrn_out, ret_out = [], [], [], [], []
    for layer in range(DEPTH):
        sh1, sc1, g1, sh2, sc2, g2 = adaln(cond, prm['mod_w'][layer], prm['mod_b'][layer])
        h = rmsnorm(x, prm['norm1_w'][layer]) * (1 + sc1) + sh1
        if layer % 2 == 0:
            e = layer // 2
            u, aq, ak, av = jnp.split(h @ prm['ab_w_in'][e],
                                      [S5_WIDTH, S5_WIDTH + ATT_Q_WIDTH, S5_WIDTH + ATT_Q_WIDTH + ATT_KV_WIDTH],
                                      axis=-1)
            s5_h0 = ctx['state_s5'][:, e] if latent else jnp.zeros((b, 2, 2, S5_GROUPS, S5_STATE), F32)
            y_s5, s5_fin = s5_mixer(u, prm['s5_lambda_re'][e], prm['s5_lambda_im'][e], prm['s5_log_dt'][e],
                                    prm['s5_b_re'][e], prm['s5_b_im'][e], prm['s5_c_re'][e], prm['s5_c_im'][e],
                                    prm['s5_d'][e], prm['s5_glu_w'][e], s5_h0)
            aq = aq.reshape(b, n, ATT_HEADS, ATT_HEAD_DIM)
            ak = ak.reshape(b, n, ATT_KV_HEADS, ATT_HEAD_DIM)
            av = av.reshape(b, n, ATT_KV_HEADS, ATT_HEAD_DIM)
            sink = prm['attn_sink'][e].reshape(ATT_KV_HEADS, ATT_GROUP)
            if latent:
                y_att = window_attention_latent(apply_rope(aq, *rope_att), apply_rope(ak, *rope_att), av,
                                                ctx['cache_k'][:, e], ctx['cache_v'][:, e], sink)
            else:
                y_att = context_attention(aq, ak, av, sink)
                s5_out.append(s5_fin)
                k_out.append(ak)
                v_out.append(av)
            y = jnp.concatenate([y_s5, y_att.astype(x.dtype)], axis=-1) @ prm['ab_w_out'][e]
        else:
            o = layer // 2
            if latent:
                s_h0, s_r0 = ctx['state_hgrn'][:, o], ctx['state_ret'][:, o]
            else:
                s_h0 = jnp.zeros((b, 2, HGRN_HEADS, HGRN_HEAD_DIM, HGRN_HEAD_DIM), F32)
                s_r0 = jnp.zeros((b, 2, RET_HEADS, RET_HEAD_DIM, RET_HEAD_DIM), F32)
            y, s_h, s_r = hgrn_ret_mixer(h, prm['cd_w_in'][o], prm['cd_w_out'][o], lb_all[o],
                                         prm['hgrn_norm_w'][o], prm['ret_decay'][o], prm['ret_norm_w'][o],
                                         rope_ret if latent else None, s_h0, s_r0)
            if not latent:
                hgrn_out.append(s_h)
                ret_out.append(s_r)
        x = x + g1 * y
        h = rmsnorm(x, prm['norm2_w'][layer]) * (1 + sc2) + sh2
        x = x + g2 * expert_choice_ffn(h, prm['router_w'][layer], prm['moe_w1'][layer],
                                       prm['moe_w3'][layer], prm['moe_w2'][layer])
    y_out = rmsnorm(x, prm['final_norm_w'])
    if latent:
        return y_out, None
    return y_out, (jnp.stack(s5_out, axis=1), jnp.stack(k_out, axis=1), jnp.stack(v_out, axis=1),
                   jnp.stack(hgrn_out, axis=1), jnp.stack(ret_out, axis=1))


def setup_inputs(seed: int = 0) -> dict:
    key = jax.random.key(seed)
    ks = iter(jax.random.split(key, 40))

    def nrm(shape, scale=1.0):
        return scale * jax.random.normal(next(ks), shape, F32)

    lam_im0 = jnp.broadcast_to(jnp.pi * jnp.arange(S5_STATE, dtype=F32), (N_EVEN, 2, S5_GROUPS, S5_STATE))
    ret0 = jnp.log(-jnp.log1p(-jnp.power(2.0, -5.0 - jnp.arange(RET_HEADS, dtype=F32))))
    return {
        'x_prompt': nrm((BATCH, SEQ, D_MODEL)),
        'x_sample': nrm((DEC_BATCH, DEC_SEQ, D_MODEL)),
        'state_s5': nrm((DEC_BATCH, N_EVEN, 2, 2, S5_GROUPS, S5_STATE), 0.1),
        'cache_k': nrm((DEC_BATCH, N_EVEN, PAST_LEN, ATT_KV_HEADS, ATT_HEAD_DIM)),
        'cache_v': nrm((DEC_BATCH, N_EVEN, PAST_LEN, ATT_KV_HEADS, ATT_HEAD_DIM)),
        'state_hgrn': nrm((DEC_BATCH, N_ODD, 2, HGRN_HEADS, HGRN_HEAD_DIM, HGRN_HEAD_DIM), 0.5),
        'state_ret': nrm((DEC_BATCH, N_ODD, 2, RET_HEADS, RET_HEAD_DIM, RET_HEAD_DIM), 1.0),
        'c': nrm((DEC_BATCH, D_MODEL)),
        'c_ctx': nrm((D_MODEL,)),
        'mod_w': nrm((DEPTH, D_MODEL, 6 * D_MODEL), 0.5 * D_MODEL ** -0.5),
        'mod_b': nrm((DEPTH, 6 * D_MODEL), 0.02),
        'norm1_w': 1.0 + nrm((DEPTH, D_MODEL), 0.02),
        'norm2_w': 1.0 + nrm((DEPTH, D_MODEL), 0.02),
        'final_norm_w': 1.0 + nrm((D_MODEL,), 0.02),
        'ab_w_in': nrm((N_EVEN, D_MODEL, AB_IN), D_MODEL ** -0.5),
        'ab_w_out': nrm((N_EVEN, AB_OUT, D_MODEL), AB_OUT ** -0.5),
        's5_lambda_re': -0.5 + nrm((N_EVEN, 2, S5_GROUPS, S5_STATE), 0.01),
        's5_lambda_im': lam_im0 + nrm((N_EVEN, 2, S5_GROUPS, S5_STATE), 0.01),
        's5_log_dt': jax.random.uniform(next(ks), (N_EVEN, 2, S5_GROUPS), F32, math.log(1e-3), math.log(1e-1)),
        's5_b_re': nrm((N_EVEN, 2, S5_GROUPS, S5_STATE, S5_GROUP), (2 * S5_GROUP) ** -0.5),
        's5_b_im': nrm((N_EVEN, 2, S5_GROUPS, S5_STATE, S5_GROUP), (2 * S5_GROUP) ** -0.5),
        's5_c_re': nrm((N_EVEN, 2, S5_GROUPS, S5_GROUP, S5_STATE), 1.0),
        's5_c_im': nrm((N_EVEN, 2, S5_GROUPS, S5_GROUP, S5_STATE), 1.0),
        's5_d': nrm((N_EVEN, S5_WIDTH), 0.5),
        's5_glu_w': nrm((N_EVEN, S5_WIDTH, S5_WIDTH), S5_WIDTH ** -0.5),
        'attn_sink': nrm((N_EVEN, ATT_HEADS), 0.5),
        'cd_w_in': nrm((N_ODD, D_MODEL, CD_IN), D_MODEL ** -0.5),
        'cd_w_out': nrm((N_ODD, CD_OUT, D_MODEL), CD_OUT ** -0.5),
        'hgrn_lb_logits': nrm((N_ODD, 2, HGRN_WIDTH), 0.1),
        'hgrn_norm_w': 1.0 + nrm((N_ODD, HGRN_WIDTH), 0.02),
        'ret_decay': ret0 + nrm((N_ODD, 2, RET_HEADS), 0.02),
        'ret_norm_w': 1.0 + nrm((N_ODD, RET_WIDTH), 0.02),
        'router_w': nrm((DEPTH, D_MODEL, N_EXPERTS), D_MODEL ** -0.5),
        'moe_w1': nrm((DEPTH, N_EXPERTS, D_MODEL, EXPERT_FF), D_MODEL ** -0.5),
        'moe_w3': nrm((DEPTH, N_EXPERTS, D_MODEL, EXPERT_FF), D_MODEL ** -0.5),
        'moe_w2': nrm((DEPTH, N_EXPERTS, EXPERT_FF, D_MODEL), EXPERT_FF ** -0.5),
    }


def reference(x_prompt, x_sample, state_s5, cache_k, cache_v, state_hgrn, state_ret, c, c_ctx,
              mod_w, mod_b, norm1_w, norm2_w, final_norm_w, ab_w_in, ab_w_out,
              s5_lambda_re, s5_lambda_im, s5_log_dt, s5_b_re, s5_b_im, s5_c_re, s5_c_im, s5_d, s5_glu_w,
              attn_sink, cd_w_in, cd_w_out, hgrn_lb_logits, hgrn_norm_w, ret_decay, ret_norm_w,
              router_w, moe_w1, moe_w3, moe_w2):
    prm = dict(mod_w=mod_w, mod_b=mod_b, norm1_w=norm1_w, norm2_w=norm2_w, final_norm_w=final_norm_w,
               ab_w_in=ab_w_in, ab_w_out=ab_w_out, s5_lambda_re=s5_lambda_re, s5_lambda_im=s5_lambda_im,
               s5_log_dt=s5_log_dt, s5_b_re=s5_b_re, s5_b_im=s5_b_im, s5_c_re=s5_c_re, s5_c_im=s5_c_im,
               s5_d=s5_d, s5_glu_w=s5_glu_w, attn_sink=attn_sink, cd_w_in=cd_w_in, cd_w_out=cd_w_out,
               hgrn_lb_logits=hgrn_lb_logits, hgrn_norm_w=hgrn_norm_w, ret_decay=ret_decay,
               ret_norm_w=ret_norm_w, router_w=router_w, moe_w1=moe_w1, moe_w3=moe_w3, moe_w2=moe_w2)
    y_prompt, ctx_tensors = trunk(x_prompt, c_ctx[None, :], prm, None)
    new_state_s5, new_cache_k, new_cache_v, new_state_hgrn, new_state_ret = ctx_tensors
    ctx = dict(state_s5=state_s5, cache_k=cache_k, cache_v=cache_v, state_hgrn=state_hgrn, state_ret=state_ret)
    y_sample, _ = trunk(x_sample, c, prm, ctx)
    return (y_prompt, y_sample, new_state_s5, new_cache_k, new_cache_v, new_state_hgrn, new_state_ret)
```

```python
import functools
import math
from typing import NamedTuple

import numpy as np
import jax
import jax.numpy as jnp
from jax import lax
from jax.experimental import pallas as pl
from jax.experimental.pallas import tpu as pltpu

F32 = jnp.float32
BF16 = jnp.bfloat16
I32 = jnp.int32

D_MODEL = 1024
DEPTH = 4
N_EVEN = 2
N_ODD = 2
EPS = 1e-6
GRID_W = 64
S5_WIDTH = 512
S5_GROUP = 16
S5_GROUPS = 32
S5_STATE = 64
S5_CHUNK = 16
S5_PAIRS = S5_GROUPS // 2
ATT_HEAD_DIM = 64
ATT_HEADS = 8
ATT_KV_HEADS = 2
ATT_GROUP = 4
WINDOW = 128
ROPE_BASE = 10000.0
AB_IN = 1280
HEAD_DIM = 128
REC_HEADS = 4
CD_IN = 4608
N_EXPERTS = 16
EXPERT_FF = 1536
CAPACITY_FACTOR = 2

LANES = 128
ROW_TILE = 256
FF_TILE = 512
F32_INF_BITS = 0x7F800000
F32_MIN_NORMAL_BITS = 0x00800000
VMEM_LIMIT = 56 * 1024 * 1024


class Cfg(NamedTuple):
    ctx_b: int
    ctx_n: int
    lat_b: int
    lat_n: int
    past: int

    @property
    def tc(self):
        return self.ctx_b * self.ctx_n

    @property
    def t(self):
        return self.tc + self.lat_b * self.lat_n

    @property
    def gs(self):
        return self.lat_n

    @property
    def groups(self):
        return 1 + self.lat_b

    @property
    def slots(self):
        return CAPACITY_FACTOR * self.gs // N_EXPERTS


def _cp(sem, vmem=VMEM_LIMIT):
    return pltpu.CompilerParams(dimension_semantics=sem, vmem_limit_bytes=vmem)


def _nt(a, b):
    return lax.dot_general(a, b, (((1,), (1,)), ((), ())), preferred_element_type=F32)


def _dot(a, b, precision=None):
    return jnp.dot(a, b, preferred_element_type=F32, precision=precision)


_HI = lax.Precision.HIGHEST


def _adaln_kernel(c_ref, w_ref, b_ref, o_ref):
    c = c_ref[...]
    s = c * jax.nn.sigmoid(c)
    o_ref[0] = _dot(s.astype(BF16), w_ref[0].astype(BF16)) + b_ref[0]


def adaln_table(conds, mod_w, mod_b):
    n6 = 6 * D_MODEL
    tn = 1536
    return pl.pallas_call(
        _adaln_kernel,
        grid=(DEPTH, n6 // tn),
        in_specs=[pl.BlockSpec((8, D_MODEL), lambda l, j: (0, 0)),
                  pl.BlockSpec((1, D_MODEL, tn), lambda l, j: (l, 0, j)),
                  pl.BlockSpec((1, 1, tn), lambda l, j: (l, 0, j))],
        out_specs=pl.BlockSpec((1, 8, tn), lambda l, j: (l, 0, j)),
        out_shape=jax.ShapeDtypeStruct((DEPTH, 8, n6), F32),
        compiler_params=_cp(("parallel", "parallel")),
        name="adaln_table",
    )(conds, mod_w, mod_b.reshape(DEPTH, 1, n6))


def _mod_spec(cfg, layer, part):
    tiles_per_group = cfg.gs // ROW_TILE
    return pl.BlockSpec((None, None, None, 1, D_MODEL),
                        lambda i: (layer, i // tiles_per_group, part, 0, 0))


def _rms(x, w):
    return x * lax.rsqrt(jnp.mean(x * x, axis=-1, keepdims=True) + EPS) * w


def _nmm_kernel(x_ref, nw_ref, sh_ref, sc_ref, w_ref, o_ref):
    h = _rms(x_ref[...], nw_ref[...]) * (1.0 + sc_ref[...]) + sh_ref[...]
    o_ref[...] = _dot(h.astype(BF16), w_ref[...])


def norm_mod_matmul(cfg, x, nw, mod5, layer, w_bf16):
    n_out = w_bf16.shape[1]
    return pl.pallas_call(
        _nmm_kernel,
        grid=(cfg.t // ROW_TILE,),
        in_specs=[pl.BlockSpec((ROW_TILE, D_MODEL), lambda i: (i, 0)),
                  pl.BlockSpec((1, D_MODEL), lambda i: (0, 0)),
                  _mod_spec(cfg, layer, 0), _mod_spec(cfg, layer, 1),
                  pl.BlockSpec((D_MODEL, n_out), lambda i: (0, 0))],
        out_specs=pl.BlockSpec((ROW_TILE, n_out), lambda i: (i, 0)),
        out_shape=jax.ShapeDtypeStruct((cfg.t, n_out), F32),
        compiler_params=_cp(("parallel",)),
        name="norm_mod_matmul",
    )(x, nw.reshape(1, D_MODEL), mod5, mod5, w_bf16)


def _s5_tables(lam_re, lam_im, log_dt, b_re, b_im, c_re, c_im):
    L = S5_CHUNK
    lr, li = lam_re.astype(F32), lam_im.astype(F32)
    dt = jnp.exp(log_dt.astype(F32))[..., None]
    mag = jnp.exp(lr * dt)
    ar, ai = mag * jnp.cos(li * dt), mag * jnp.sin(li * dt)
    den = lr * lr + li * li
    cr = ((ar - 1.0) * lr + ai * li) / den
    ci = (ai * lr - (ar - 1.0) * li) / den
    br, bi = b_re.astype(F32), b_im.astype(F32)
    bbr = cr[..., None] * br - ci[..., None] * bi
    bbi = cr[..., None] * bi + ci[..., None] * br

    def pw_step(carry, _):
        pr, pi = carry
        return (pr * ar - pi * ai, pr * ai + pi * ar), (pr, pi)

    (_, _), (pr, pi) = lax.scan(pw_step, (jnp.ones_like(ar), jnp.zeros_like(ar)), None, length=L + 1)
    er = pr[:L, ..., None] * bbr[None] - pi[:L, ..., None] * bbi[None]
    ei = pr[:L, ..., None] * bbi[None] + pi[:L, ..., None] * bbr[None]
    ccr, cci = c_re.astype(F32), c_im.astype(F32)
    kt = (jnp.einsum('dgcp,tdgpe->tdgce', ccr, er, precision=_HI)
          - jnp.einsum('dgcp,tdgpe->tdgce', cci, ei, precision=_HI))
    s = np.arange(L)
    lag_f = s[:, None] - s[None, :]
    m_f = jnp.where((lag_f >= 0)[:, :, None, None, None], kt[np.clip(lag_f, 0, L - 1), 0], 0.0)
    m_b = jnp.where((lag_f <= 0)[:, :, None, None, None], kt[np.clip(-lag_f, 0, L - 1), 1], 0.0)
    m = (m_f + m_b).transpose(2, 1, 4, 0, 3)
    m = m.reshape(S5_GROUPS, L * S5_GROUP, L * S5_GROUP)
    wf_r, wf_i = er[::-1, 0], ei[::-1, 0]
    wb_r, wb_i = er[:, 1], ei[:, 1]

    def w_mat(w):
        return w.transpose(1, 0, 3, 2).reshape(S5_GROUPS, L * S5_GROUP, S5_STATE)

    def g_mats(d, p_r, p_i):
        g_r = ccr[d][None] * p_r[:, :, None, :] - cci[d][None] * p_i[:, :, None, :]
        g_i = -ccr[d][None] * p_i[:, :, None, :] - cci[d][None] * p_r[:, :, None, :]
        to = lambda g: g.transpose(1, 3, 0, 2).reshape(S5_GROUPS, S5_STATE, L * S5_GROUP)
        return to(g_r), to(g_i)

    gf_r, gf_i = g_mats(0, pr[1:L + 1, 0], pi[1:L + 1, 0])
    gb_r, gb_i = g_mats(1, pr[L:0:-1, 1], pi[L:0:-1, 1])

    def pair_blockdiag(a):
        g, r, c = a.shape
        a = a.reshape(g // 2, 2, r, c)
        z = jnp.zeros_like(a[:, 0])
        top = jnp.concatenate([a[:, 0], z], axis=2)
        bot = jnp.concatenate([z, a[:, 1]], axis=2)
        return jnp.concatenate([top, bot], axis=1)

    wcat = jnp.concatenate([pair_blockdiag(w_mat(w)) for w in (wf_r, wf_i, wb_r, wb_i)], axis=2)
    mmat = pair_blockdiag(m)
    gmat = jnp.concatenate([pair_blockdiag(g) for g in (gf_r, gf_i, gb_r, gb_i)], axis=1)

    def pair_lanes(a):
        return a.reshape(S5_PAIRS, 2 * S5_STATE)

    a16 = jnp.stack([pair_lanes(pr[L, 0]), pair_lanes(pi[L, 0]),
                     pair_lanes(pr[L, 1]), pair_lanes(pi[L, 1])], axis=1)
    return wcat.astype(BF16), mmat.astype(BF16), gmat.astype(BF16), a16


def _s5_kernel(cfg, u_ref, w_ref, m_ref, g_ref, a_ref, h0_ref, y_ref, fin_ref, xs_ref):
    rc = cfg.tc // S5_CHUNK
    lc = cfg.ctx_n // S5_CHUNK
    ll = cfg.lat_n // S5_CHUNK
    r_all = cfg.t // S5_CHUNK
    u = u_ref[...].astype(BF16)
    w = _dot(u, w_ref[...])
    a = a_ref[...]
    row = lax.broadcasted_iota(I32, (r_all, LANES), 0)
    is_ctx = row < rc
    pos = jnp.where(is_ctx, row % lc, (row - rc) % ll)
    seg = jnp.where(is_ctx, lc, ll)
    prev_parts = []
    for d in range(2):
        xr = w[:, (2 * d) * LANES:(2 * d + 1) * LANES]
        xi = w[:, (2 * d + 1) * LANES:(2 * d + 2) * LANES]
        ar, ai = a[2 * d:2 * d + 1], a[2 * d + 1:2 * d + 2]
        first = 0 if d == 0 else seg - 1
        h0r = jnp.zeros((r_all, LANES), F32)
        h0i = jnp.zeros((r_all, LANES), F32)
        for b in range(cfg.lat_b):
            sel = (row >= rc + b * ll) & (row < rc + (b + 1) * ll)
            h0r = jnp.where(sel, h0_ref[b, 2 * d:2 * d + 1, :], h0r)
            h0i = jnp.where(sel, h0_ref[b, 2 * d + 1:2 * d + 2, :], h0i)
        at_first = pos == first
        xr = xr + jnp.where(at_first, ar * h0r - ai * h0i, 0.0)
        xi = xi + jnp.where(at_first, ar * h0i + ai * h0r, 0.0)
        pr, pi = ar, ai
        step = 1
        while step < max(lc, ll):
            if d == 0:
                sr, si = pltpu.roll(xr, step, 0), pltpu.roll(xi, step, 0)
                ok = (pos >= step)
            else:
                sr, si = pltpu.roll(xr, r_all - step, 0), pltpu.roll(xi, r_all - step, 0)
                ok = (pos < seg - step)
            xr, xi = (xr + jnp.where(ok, pr * sr - pi * si, 0.0),
                      xi + jnp.where(ok, pr * si + pi * sr, 0.0))
            pr, pi = pr * pr - pi * pi, 2.0 * pr * pi
            step *= 2
        if d == 0:
            nr, ni = pltpu.roll(xr, 1, 0), pltpu.roll(xi, 1, 0)
        else:
            nr, ni = pltpu.roll(xr, r_all - 1, 0), pltpu.roll(xi, r_all - 1, 0)
        prev_parts += [jnp.where(at_first, h0r, nr), jnp.where(at_first, h0i, ni)]
        xs_ref[2 * d] = xr
        xs_ref[2 * d + 1] = xi
    xin = jnp.concatenate(prev_parts, axis=1).astype(BF16)
    y_ref[...] = _dot(u, m_ref[...]) + _dot(xin, g_ref[...])
    for d in range(2):
        start = lc - 1 if d == 0 else 0
        for k in range(2):
            fin_ref[2 * d + k] = xs_ref[2 * d + k, pl.ds(start, cfg.ctx_b, stride=lc), :]


def s5_mix(cfg, u_pairs, wcat, mmat, gmat, a16, h0):
    r_all = cfg.t // S5_CHUNK
    wide = 2 * S5_CHUNK * S5_GROUP
    mat = pl.BlockSpec((None, wide, wide), lambda p: (p, 0, 0))
    return pl.pallas_call(
        functools.partial(_s5_kernel, cfg),
        grid=(S5_PAIRS,),
        in_specs=[pl.BlockSpec((None, r_all, wide), lambda p: (p, 0, 0)), mat, mat, mat,
                  pl.BlockSpec((None, 4, LANES), lambda p: (p, 0, 0)),
                  pl.BlockSpec((None, cfg.lat_b, 4, LANES), lambda p: (p, 0, 0, 0))],
        out_specs=[pl.BlockSpec((None, r_all, wide), lambda p: (p, 0, 0)),
                   pl.BlockSpec((None, 4, cfg.ctx_b, LANES), lambda p: (p, 0, 0, 0))],
        out_shape=[jax.ShapeDtypeStruct((S5_PAIRS, r_all, wide), F32),
                   jax.ShapeDtypeStruct((S5_PAIRS, 4, cfg.ctx_b, LANES), F32)],
        scratch_shapes=[pltpu.VMEM((4, r_all, LANES), F32)],
        compiler_params=_cp(("parallel",)),
        name="s5_mix",
    )(u_pairs, wcat, mmat, gmat, a16, h0)


def _to_pairs(cfg, u):
    r = cfg.t // S5_CHUNK
    u = u.reshape(r, S5_CHUNK, S5_PAIRS, 2, S5_GROUP)
    return u.transpose(2, 0, 3, 1, 4).reshape(S5_PAIRS, r, 2 * S5_CHUNK * S5_GROUP)


def _from_pairs(cfg, y):
    r = cfg.t // S5_CHUNK
    y = y.reshape(S5_PAIRS, r, 2, S5_CHUNK, S5_GROUP)
    return y.transpose(1, 3, 0, 2, 4).reshape(cfg.t, S5_WIDTH)


def _rope(x, cos, sin_signed, half):
    w = x.shape[-1]
    lane = lax.broadcasted_iota(I32, x.shape, x.ndim - 1)
    swapped = jnp.where((lane % (2 * half)) < half,
                        pltpu.roll(x, w - half, x.ndim - 1), pltpu.roll(x, half, x.ndim - 1))
    return x * cos + swapped * sin_signed


def _attn_kernel(n_parts, latent, n_q_blocks, sink_ref, q_ref, *refs):
    hd = ATT_HEAD_DIM
    scale = hd ** -0.5
    if latent:
        (kp_ref, kc_ref, kn_ref, vp_ref, vc_ref, vn_ref, ck_ref, cv_ref,
         cq_ref, sq_ref, ckp_ref, skp_ref, ckc_ref, skc_ref, ckn_ref, skn_ref, o_ref) = refs
        i = pl.program_id(1)
        q = _rope(q_ref[...], cq_ref[...], sq_ref[...], hd // 2)
        kparts = [_rope(kp_ref[...], ckp_ref[...], skp_ref[...], hd // 2),
                  _rope(kc_ref[...], ckc_ref[...], skc_ref[...], hd // 2),
                  _rope(kn_ref[...], ckn_ref[...], skn_ref[...], hd // 2)]
        kband = jnp.concatenate(kparts, axis=0)
        vband = jnp.concatenate([vp_ref[...], vc_ref[...], vn_ref[...]], axis=0)
        keys = [kband, ck_ref[...]]
        vals = [vband, cv_ref[...]]
        nrow = ATT_GROUP * LANES
        qpos = lax.broadcasted_iota(I32, (nrow, 3 * LANES), 0) % LANES
        kpos = lax.broadcasted_iota(I32, (nrow, 3 * LANES), 1) - LANES
        kabs = kpos + i * LANES
        band_ok = (jnp.abs(kpos - qpos) <= WINDOW) & (kabs >= 0) & (kabs < n_q_blocks * LANES)
        masks = [band_ok, None]
    else:
        k_ref, v_ref, o_ref = refs
        q = q_ref[...]
        keys = [k_ref[...]]
        vals = [v_ref[...]]
        masks = [None]
    for kv in range(ATT_KV_HEADS):
        qs = jnp.concatenate([q[:, (kv * ATT_GROUP + g) * hd:(kv * ATT_GROUP + g + 1) * hd]
                              for g in range(ATT_GROUP)], axis=0).astype(BF16)
        sink_col = jnp.concatenate([jnp.full((LANES, 1), sink_ref[kv * ATT_GROUP + g], F32)
                                    for g in range(ATT_GROUP)], axis=0)
        logits = []
        mx = sink_col
        for kk, msk in zip(keys, masks):
            s = _nt(qs, kk[:, kv * hd:(kv + 1) * hd].astype(BF16)) * scale
            if msk is not None:
                s = jnp.where(msk, s, -1e30)
            logits.append(s)
            mx = jnp.maximum(mx, jnp.max(s, axis=-1, keepdims=True))
        den = jnp.exp(sink_col - mx)
        acc = None
        for s, vv in zip(logits, vals):
            p = jnp.exp(s - mx)
            den = den + jnp.sum(p, axis=-1, keepdims=True)
            o = _dot(p.astype(BF16), vv[:, kv * hd:(kv + 1) * hd].astype(BF16))
            acc = o if acc is None else acc + o
        out = acc / den
        for g in range(ATT_GROUP):
            h = kv * ATT_GROUP + g
            o_ref[:, h * hd:(h + 1) * hd] = out[g * LANES:(g + 1) * LANES, :]


def attention(cfg, proj, sink, cache_k, cache_v, rope_cs):
    qcol = S5_WIDTH // 512
    kcol = (S5_WIDTH + 512) // LANES
    vcol = kcol + 1
    smem = pl.BlockSpec(memory_space=pltpu.SMEM)
    nqc = cfg.ctx_n // LANES
    y_ctx = pl.pallas_call(
        functools.partial(_attn_kernel, 1, False, nqc),
        grid=(cfg.ctx_b, nqc),
        in_specs=[smem,
                  pl.BlockSpec((LANES, 512), lambda b, i: (b * nqc + i, qcol)),
                  pl.BlockSpec((cfg.ctx_n, LANES), lambda b, i: (b, kcol)),
                  pl.BlockSpec((cfg.ctx_n, LANES), lambda b, i: (b, vcol))],
        out_specs=pl.BlockSpec((LANES, 512), lambda b, i: (b * nqc + i, 0)),
        out_shape=jax.ShapeDtypeStruct((cfg.tc, 512), F32),
        compiler_params=_cp(("parallel", "parallel")),
        name="attn_context",
    )(sink, proj, proj, proj)
    nql = cfg.lat_n // LANES
    off = cfg.tc // LANES
    cos_t, sin_t = rope_cs

    def rb(b, i):
        return off + b * nql + i

    def prev(i):
        return jnp.maximum(i - 1, 0)

    def nxt(i):
        return jnp.minimum(i + 1, nql - 1)

    kspec = lambda f, col: pl.BlockSpec((LANES, LANES), lambda b, i: (rb(b, f(i)), col))
    tspec = lambda f: pl.BlockSpec((LANES, LANES), lambda b, i: (f(i), 0))
    same = lambda i: i
    y_lat = pl.pallas_call(
        functools.partial(_attn_kernel, 2, True, nql),
        grid=(cfg.lat_b, nql),
        in_specs=[smem,
                  pl.BlockSpec((LANES, 512), lambda b, i: (rb(b, i), qcol)),
                  kspec(prev, kcol), kspec(same, kcol), kspec(nxt, kcol),
                  kspec(prev, vcol), kspec(same, vcol), kspec(nxt, vcol),
                  pl.BlockSpec((None, cfg.past, LANES), lambda b, i: (b, 0, 0)),
                  pl.BlockSpec((None, cfg.past, LANES), lambda b, i: (b, 0, 0)),
                  pl.BlockSpec((LANES, 512), lambda b, i: (i, 0)),
                  pl.BlockSpec((LANES, 512), lambda b, i: (i, 0)),
                  tspec(prev), tspec(prev), tspec(same), tspec(same), tspec(nxt), tspec(nxt)],
        out_specs=pl.BlockSpec((LANES, 512), lambda b, i: (b * nql + i, 0)),
        out_shape=jax.ShapeDtypeStruct((cfg.lat_b * cfg.lat_n, 512), F32),
        compiler_params=_cp(("parallel", "parallel")),
        name="attn_latent",
    )(sink, proj, proj, proj, proj, proj, proj, proj, cache_k, cache_v,
      cos_t, sin_t, cos_t, sin_t, cos_t, sin_t, cos_t, sin_t)
    return jnp.concatenate([y_ctx, y_lat], axis=0)


def _rope_tables(n, head, width):
    rows = jnp.repeat(jnp.arange(n // GRID_W, dtype=F32), GRID_W)
    cols = (jnp.arange(n) % GRID_W).astype(F32)
    nf = head // 4
    inv = ROPE_BASE ** (-jnp.arange(nf, dtype=F32) / nf)
    ang = jnp.concatenate([rows[:, None] * inv, cols[:, None] * inv], axis=-1)
    c, s = jnp.cos(ang), jnp.sin(ang)
    cos_h = jnp.concatenate([c, c], axis=-1)
    sin_h = jnp.concatenate([-s, s], axis=-1)
    reps = width // head
    return jnp.tile(cos_h, (1, reps)), jnp.tile(sin_h, (1, reps))


def _even_out_kernel(x_ref, u_ref, yd_ref, ya_ref, d_ref, glu_ref, wo_ref, g_ref, o_ref):
    y = jax.nn.gelu(u_ref[...] * d_ref[...] + yd_ref[...])
    z = y * jax.nn.sigmoid(_dot(y.astype(BF16), glu_ref[...]))
    cat = jnp.concatenate([z, ya_ref[...]], axis=-1).astype(BF16)
    o_ref[...] = x_ref[...] + g_ref[...] * _dot(cat, wo_ref[...])


def even_out(cfg, x, proj, yd, yatt, s5_d, glu_bf16, wo_bf16, mod5, layer):
    tile = lambda w: pl.BlockSpec((ROW_TILE, w), lambda i: (i, 0))
    return pl.pallas_call(
        _even_out_kernel,
        grid=(cfg.t // ROW_TILE,),
        in_specs=[tile(D_MODEL), tile(S5_WIDTH), tile(S5_WIDTH), tile(512),
                  pl.BlockSpec((1, S5_WIDTH), lambda i: (0, 0)),
                  pl.BlockSpec((S5_WIDTH, S5_WIDTH), lambda i: (0, 0)),
                  pl.BlockSpec((D_MODEL, D_MODEL), lambda i: (0, 0)),
                  _mod_spec(cfg, layer, 2)],
        out_specs=tile(D_MODEL),
        out_shape=jax.ShapeDtypeStruct((cfg.t, D_MODEL), F32),
        compiler_params=_cp(("parallel",)),
        name="even_out",
    )(x, proj, yd, yatt, s5_d.reshape(1, S5_WIDTH), glu_bf16, wo_bf16, mod5)


REC_CHUNK = 128
REC_LEVELS = 7


def _rec_tables():
    c = REC_CHUNK
    t = np.arange(c)[:, None]
    u = np.arange(c)[None, :]
    out = np.zeros((2, (REC_LEVELS + 2) * c, c), np.float32)
    for d in range(2):
        for lv in range(REC_LEVELS):
            h = 1 << lv
            mid = (t // (2 * h)) * (2 * h) + h
            second = (t % (2 * h)) >= h
            if d == 0:
                a = np.where(second, (u >= mid) & (u <= t), (u > t) & (u < mid))
            else:
                a = np.where(second, (u >= mid) & (u < t), (u >= t) & (u < mid))
            out[d, lv * c:(lv + 1) * c] = a
        if d == 0:
            out[d, REC_LEVELS * c:(REC_LEVELS + 1) * c] = u <= t
            out[d, (REC_LEVELS + 1) * c:] = u > t
        else:
            out[d, REC_LEVELS * c:(REC_LEVELS + 1) * c] = u >= t
            out[d, (REC_LEVELS + 1) * c:] = u < t
    return out


def _rec_kernel(cfg, layer_o, q_ref, zf_ref, zb_ref, v_ref, rq_ref, rk_ref, rv_ref,
                amat_ref, acnt_ref, lbl_ref, rdec_ref, cos_ref, sin_ref, s0_ref, o_ref, st_ref):
    c = REC_CHUNK
    d = pl.program_id(0)
    step = pl.program_id(1)
    n_chunks = cfg.t // c
    chunk = jnp.where(d == 0, step, n_chunks - 1 - step)
    cpc = cfg.ctx_n // c
    cpl = cfg.lat_n // c
    n_ctx_chunks = cfg.tc // c
    is_lat = chunk >= n_ctx_chunks
    pos = jnp.where(is_lat, (chunk - n_ctx_chunks) % cpl, chunk % cpc)
    last = jnp.where(is_lat, cpl - 1, cpc - 1)
    first_processed = jnp.where(d == 0, pos == 0, pos == last)

    @pl.when(first_processed & jnp.logical_not(is_lat))
    def _():
        st_ref[...] = jnp.zeros_like(st_ref)

    @pl.when(first_processed & is_lat)
    def _():
        st_ref[...] = s0_ref[...]

    row = lax.broadcasted_iota(I32, (c, c), 0)
    col = lax.broadcasted_iota(I32, (c, c), 1)
    fwd = d == 0
    amat = amat_ref[...]

    def mix_head(q, k, v, dsum, hidx):
        e = jnp.exp(dsum)
        qb, kb = q.astype(BF16), k.astype(BF16)
        scores = jnp.where(row == col, _nt(qb, kb), 0.0)
        for lv in range(REC_LEVELS):
            h = 1 << lv
            el = e[lv * c:(lv + 1) * c]
            second = (row % (2 * h)) >= h
            is_q = second == fwd
            qt = jnp.where(is_q, q * el, 0.0).astype(BF16)
            kt = jnp.where(is_q, 0.0, k * el).astype(BF16)
            same = (row // (2 * h)) == (col // (2 * h))
            scores = scores + jnp.where(same, _nt(qt, kt), 0.0)
        o = _dot(scores.astype(BF16), v.astype(BF16))
        st = st_ref[hidx]
        qi = (q * e[REC_LEVELS * c:(REC_LEVELS + 1) * c]).astype(BF16)
        o = o + _nt(qi, st.astype(BF16))
        kd = (k * e[(REC_LEVELS + 1) * c:]).astype(BF16)
        inc = e[REC_LEVELS * c:(REC_LEVELS + 1) * c]
        total = jnp.where(fwd, inc[c - 1:c], inc[0:1])
        st_ref[hidx] = st * total + _dot(v.T.astype(BF16), kd)
        return o

    lbl = lbl_ref[...]
    mx = jnp.max(lbl, axis=0, keepdims=True)
    ex = jnp.exp(lbl - mx)
    sm = ex / jnp.sum(ex, axis=0, keepdims=True)
    lb_all = jnp.zeros_like(sm[0])
    for i in range(1, layer_o + 1):
        lb_all = lb_all + sm[i]
    lb = jnp.where(fwd, lb_all[0:1], lb_all[1:2])
    z = jnp.where(fwd, zf_ref[...], zb_ref[...])
    log_sig = jnp.minimum(z, 0.0) - jnp.log1p(jnp.exp(-jnp.abs(z)))
    a_ = jnp.log1p(-lb) + log_sig
    b_ = jnp.log(lb)
    big = jnp.maximum(a_, b_)
    logf = big + jnp.log1p(jnp.exp(-jnp.abs(a_ - b_)))
    kh = (1.0 - lb) * jax.nn.sigmoid(-z)
    l1 = logf.astype(BF16)
    r1 = logf - l1.astype(F32)
    l2 = r1.astype(BF16)
    l3 = (r1 - l2.astype(F32)).astype(BF16)
    qh, vh = q_ref[...], v_ref[...]
    outs = []
    for h in range(REC_HEADS):
        sl = slice(h * HEAD_DIM, (h + 1) * HEAD_DIM)
        dsum = _dot(amat, l1[:, sl]) + _dot(amat, l2[:, sl]) + _dot(amat, l3[:, sl])
        outs.append(mix_head(qh[:, sl], kh[:, sl], vh[:, sl], dsum, h))
    rq, rk = rq_ref[...], rk_ref[...]
    rq_rot = _rope(rq, cos_ref[...], sin_ref[...], HEAD_DIM // 2)
    rk_rot = _rope(rk, cos_ref[...], sin_ref[...], HEAD_DIM // 2)
    rq = jnp.where(is_lat, rq_rot, rq)
    rk = jnp.where(is_lat, rk_rot, rk) * (HEAD_DIM ** -0.5)
    rv = rv_ref[...]
    gam = jnp.exp(jnp.where(fwd, rdec_ref[0:1], rdec_ref[1:2]))
    cnt = acnt_ref[...]
    for h in range(REC_HEADS):
        sl = slice(h * HEAD_DIM, (h + 1) * HEAD_DIM)
        dsum = -cnt * gam[:, sl]
        outs.append(mix_head(rq[:, sl], rk[:, sl], rv[:, sl], dsum, REC_HEADS + h))
    o_ref[...] = jnp.concatenate(outs, axis=1)


def recurrences(cfg, layer_o, proj, amat, acnt, lb_logits, ret_decay_lanes, rope_cs, s0):
    c = REC_CHUNK
    n_chunks = cfg.t // c
    n_ctx_chunks = cfg.tc // c
    cpc, cpl = cfg.ctx_n // c, cfg.lat_n // c
    n_seq = cfg.ctx_b + cfg.lat_b

    def chunk_of(d, s):
        return jnp.where(d == 0, s, n_chunks - 1 - s)

    def seq_of(d, s):
        ch = chunk_of(d, s)
        return jnp.where(ch >= n_ctx_chunks, cfg.ctx_b + (ch - n_ctx_chunks) // cpl, ch // cpc)

    def lat_of(d, s):
        return jnp.maximum(seq_of(d, s) - cfg.ctx_b, 0)

    def lat_pos(d, s):
        ch = chunk_of(d, s)
        return jnp.where(ch >= n_ctx_chunks, (ch - n_ctx_chunks) % cpl, 0)

    col = lambda k: pl.BlockSpec((c, 512), lambda d, s: (chunk_of(d, s), k))
    nrow = (REC_LEVELS + 2) * c
    cos_t, sin_t = rope_cs
    return pl.pallas_call(
        functools.partial(_rec_kernel, cfg, layer_o),
        grid=(2, n_chunks),
        in_specs=[col(0), col(1), col(2), col(3), col(5), col(6), col(7),
                  pl.BlockSpec((None, nrow, c), lambda d, s: (d, 0, 0)),
                  pl.BlockSpec((None, nrow, c), lambda d, s: (d, 0, 0)),
                  pl.BlockSpec((N_ODD, 2, 512), lambda d, s: (0, 0, 0)),
                  pl.BlockSpec((None, 2, 512), lambda d, s: (0, 0, 0)),
                  pl.BlockSpec((c, 512), lambda d, s: (lat_pos(d, s), 0)),
                  pl.BlockSpec((c, 512), lambda d, s: (lat_pos(d, s), 0)),
                  pl.BlockSpec((None, None, 2 * REC_HEADS, HEAD_DIM, HEAD_DIM),
                               lambda d, s: (d, lat_of(d, s), 0, 0, 0))],
        out_specs=[pl.BlockSpec((None, c, D_MODEL), lambda d, s: (d, chunk_of(d, s), 0)),
                   pl.BlockSpec((None, None, 2 * REC_HEADS, HEAD_DIM, HEAD_DIM),
                                lambda d, s: (d, seq_of(d, s), 0, 0, 0))],
        out_shape=[jax.ShapeDtypeStruct((2, cfg.t, D_MODEL), F32),
                   jax.ShapeDtypeStruct((2, n_seq, 2 * REC_HEADS, HEAD_DIM, HEAD_DIM), F32)],
        compiler_params=_cp(("parallel", "arbitrary")),
        name="recurrences",
    )(proj, proj, proj, proj, proj, proj, proj, amat, acnt, lb_logits, ret_decay_lanes, cos_t, sin_t, s0)


def _odd_out_kernel(x_ref, of_ref, ob_ref, cg_ref, rg_ref, hw_ref, rw_ref, wo_ref, g_ref, o_ref):
    o = of_ref[...] + ob_ref[...]
    hw, rw = hw_ref[...], rw_ref[...]
    parts = []
    for h in range(REC_HEADS):
        sl = slice(h * HEAD_DIM, (h + 1) * HEAD_DIM)
        oh = o[:, sl]
        parts.append(oh * lax.rsqrt(jnp.mean(oh * oh, -1, keepdims=True) + EPS) * hw[:, sl])
    oc = jnp.concatenate(parts, axis=1) * jax.nn.sigmoid(cg_ref[...])
    parts = []
    for h in range(REC_HEADS):
        sl = slice(h * HEAD_DIM, (h + 1) * HEAD_DIM)
        oh = o[:, 512 + h * HEAD_DIM:512 + (h + 1) * HEAD_DIM]
        oh = oh - jnp.mean(oh, -1, keepdims=True)
        parts.append(oh * lax.rsqrt(jnp.mean(oh * oh, -1, keepdims=True) + EPS) * rw[:, sl])
    rg = rg_ref[...]
    orr = jnp.concatenate(parts, axis=1) * (rg * jax.nn.sigmoid(rg))
    cat = jnp.concatenate([oc, orr], axis=1).astype(BF16)
    o_ref[...] = x_ref[...] + g_ref[...] * _dot(cat, wo_ref[...])


def odd_out(cfg, x, o2, proj, hgrn_norm_w, ret_norm_w, wo_bf16, mod5, layer):
    tile = lambda w, k=0: pl.BlockSpec((ROW_TILE, w), lambda i: (i, k))
    return pl.pallas_call(
        _odd_out_kernel,
        grid=(cfg.t // ROW_TILE,),
        in_specs=[tile(D_MODEL),
                  pl.BlockSpec((None, ROW_TILE, D_MODEL), lambda i: (0, i, 0)),
                  pl.BlockSpec((None, ROW_TILE, D_MODEL), lambda i: (1, i, 0)),
                  tile(512, 4), tile(512, 8),
                  pl.BlockSpec((1, 512), lambda i: (0, 0)), pl.BlockSpec((1, 512), lambda i: (0, 0)),
                  pl.BlockSpec((D_MODEL, D_MODEL), lambda i: (0, 0)),
                  _mod_spec(cfg, layer, 2)],
        out_specs=tile(D_MODEL),
        out_shape=jax.ShapeDtypeStruct((cfg.t, D_MODEL), F32),
        compiler_params=_cp(("parallel",)),
        name="odd_out",
    )(x, o2, o2, proj, proj, hgrn_norm_w.reshape(1, 512), ret_norm_w.reshape(1, 512), wo_bf16, mod5)


def _route_kernel(cfg, x_ref, nw_ref, sh_ref, sc_ref, wr_ref, h_ref, pos_ref, aff_ref, cj_ref):
    gs = cfg.gs
    g = pl.program_id(0)
    h = (_rms(x_ref[...], nw_ref[...]) * (1.0 + sc_ref[...]) + sh_ref[...]).astype(BF16)
    h_ref[...] = h
    logits = _nt(wr_ref[...], h)
    mx = jnp.max(logits, axis=0, keepdims=True)
    ex = jnp.exp(logits - mx)
    aff = ex / jnp.sum(ex, axis=0, keepdims=True)
    aff_ref[...] = aff

    def seg_sum(x, seg):
        parts = []
        for r in range(gs // seg):
            s = jnp.sum(x[:, r * seg:(r + 1) * seg], axis=1, keepdims=True)
            parts.append(jnp.broadcast_to(s, (N_EXPERTS, seg)))
        return parts[0] if len(parts) == 1 else jnp.concatenate(parts, axis=1)

    tri = (lax.broadcasted_iota(I32, (LANES, LANES), 0) < lax.broadcasted_iota(I32, (LANES, LANES), 1)).astype(BF16)
    lane = lax.broadcasted_iota(I32, (N_EXPERTS, LANES), 1)

    def prefix(mask, seg):
        parts = []
        carry = jnp.zeros((N_EXPERTS, 1), F32)
        starts = jnp.zeros((N_EXPERTS, LANES), F32)
        for j in range(gs // LANES):
            if (j * LANES) % seg == 0:
                carry = jnp.zeros((N_EXPERTS, 1), F32)
            m = mask[:, j * LANES:(j + 1) * LANES]
            starts = starts + jnp.where(lane == j, carry, 0.0)
            parts.append(_dot(m.astype(BF16), tri) + carry)
            carry = carry + jnp.sum(m, axis=1, keepdims=True)
        starts = starts + jnp.where(lane == gs // LANES, carry, 0.0)
        return jnp.concatenate(parts, axis=1), starts

    def select(seg, cap):
        def at_least_cap(mid):
            return seg_sum((aff >= mid).astype(F32), seg) >= cap

        def halve_bits(_, lohi):
            lo, hi = lohi
            mid = lo + ((hi - lo) >> 1)
            ok = at_least_cap(pltpu.bitcast(mid, F32))
            return jnp.where(ok, mid, lo), jnp.where(ok, hi, mid)

        def halve_value(_, lohi):
            lo, hi = lohi
            mid = 0.5 * (lo + hi)
            ok = at_least_cap(mid)
            return jnp.where(ok, mid, lo), jnp.where(ok, hi, mid)

        lo0 = jnp.zeros((N_EXPERTS, gs), I32)
        hi0 = jnp.full((N_EXPERTS, gs), F32_INF_BITS, I32)
        lo_b, hi_b = lax.fori_loop(0, 31, halve_bits, (lo0, hi0))
        lo_f = jnp.where(lo_b < F32_MIN_NORMAL_BITS, 0.0, pltpu.bitcast(lo_b, F32))
        lo_f, hi_f = lax.fori_loop(0, 30, halve_value, (lo_f, pltpu.bitcast(hi_b, F32)))
        above = aff >= hi_f
        tied = jnp.logical_and(aff >= lo_f, aff < hi_f)
        need = cap - seg_sum(above.astype(F32), seg)
        before, _ = prefix(tied.astype(F32), seg)
        return jnp.logical_or(above, jnp.logical_and(tied, before < need))

    def finish(sel):
        self_f = sel.astype(F32)
        slot, starts = prefix(self_f, gs)
        pos_ref[...] = jnp.where(sel, slot.astype(I32), -1)
        cj_ref[...] = starts.astype(I32)

    @pl.when(g == 0)
    def _():
        finish(select(cfg.ctx_n, CAPACITY_FACTOR * cfg.ctx_n // N_EXPERTS))

    @pl.when(g > 0)
    def _():
        finish(select(cfg.lat_n, CAPACITY_FACTOR * cfg.lat_n // N_EXPERTS))


def moe_route(cfg, x, nw, mod5, layer, wr_t_bf16):
    gs = cfg.gs
    mod = lambda part: pl.BlockSpec((None, None, None, 1, D_MODEL), lambda g: (layer, g, part, 0, 0))
    return pl.pallas_call(
        functools.partial(_route_kernel, cfg),
        grid=(cfg.groups,),
        in_specs=[pl.BlockSpec((gs, D_MODEL), lambda g: (g, 0)),
                  pl.BlockSpec((1, D_MODEL), lambda g: (0, 0)),
                  mod(3), mod(4),
                  pl.BlockSpec((N_EXPERTS, D_MODEL), lambda g: (0, 0))],
        out_specs=[pl.BlockSpec((gs, D_MODEL), lambda g: (g, 0)),
                   pl.BlockSpec((None, N_EXPERTS, gs), lambda g: (g, 0, 0)),
                   pl.BlockSpec((None, N_EXPERTS, gs), lambda g: (g, 0, 0)),
                   pl.BlockSpec((None, N_EXPERTS, LANES), lambda g: (g, 0, 0))],
        out_shape=[jax.ShapeDtypeStruct((cfg.t, D_MODEL), BF16),
                   jax.ShapeDtypeStruct((cfg.groups, N_EXPERTS, gs), I32),
                   jax.ShapeDtypeStruct((cfg.groups, N_EXPERTS, gs), F32),
                   jax.ShapeDtypeStruct((cfg.groups, N_EXPERTS, LANES), I32)],
        compiler_params=_cp(("parallel",)),
        name="moe_route",
    )(x, nw.reshape(1, D_MODEL), mod5, mod5, wr_t_bf16)


def _slot_block(cfg):
    return min(LANES, cfg.slots)


def _gather_kernel(cfg, cj_ref, h_ref, pos_ref, aff_ref, x_ref, gate_ref, acc_ref, gacc_ref):
    g, e = pl.program_id(0), pl.program_id(1)
    sb = _slot_block(cfg)
    acc_ref[...] = jnp.zeros_like(acc_ref)
    gacc_ref[...] = jnp.zeros_like(gacc_ref)
    base = (g * N_EXPERTS + e) * LANES
    sub = lax.broadcasted_iota(I32, (sb, LANES), 0)
    for tb in range(cfg.gs // LANES):
        s0, s1 = cj_ref[base + tb], cj_ref[base + tb + 1]
        for jb in range(cfg.slots // sb):
            @pl.when((s1 > s0) & (s0 < (jb + 1) * sb) & (s1 > jb * sb))
            def _():
                p = pos_ref[:, tb * LANES:(tb + 1) * LANES]
                hit = jnp.broadcast_to(p, (sb, LANES)) == (sub + jb * sb)
                acc_ref[jb * sb:(jb + 1) * sb, :] += _dot(hit.astype(BF16), h_ref[tb * LANES:(tb + 1) * LANES, :])
                a = aff_ref[:, tb * LANES:(tb + 1) * LANES]
                gacc_ref[jb * sb:(jb + 1) * sb, :] += jnp.sum(jnp.where(hit, a, 0.0), axis=1, keepdims=True)
    x_ref[...] = acc_ref[...].astype(BF16)
    gate_ref[...] = gacc_ref[...]


def moe_gather(cfg, cj, h, pos, aff):
    gs, slots = cfg.gs, cfg.slots
    pos4 = pos.reshape(cfg.groups, N_EXPERTS, 1, gs)
    aff4 = aff.reshape(cfg.groups, N_EXPERTS, 1, gs)
    grid_spec = pltpu.PrefetchScalarGridSpec(
        num_scalar_prefetch=1,
        grid=(cfg.groups, N_EXPERTS),
        in_specs=[pl.BlockSpec((gs, D_MODEL), lambda g, e, cj: (g, 0)),
                  pl.BlockSpec((None, None, 1, gs), lambda g, e, cj: (g, e, 0, 0)),
                  pl.BlockSpec((None, None, 1, gs), lambda g, e, cj: (g, e, 0, 0))],
        out_specs=[pl.BlockSpec((None, slots, D_MODEL), lambda g, e, cj: (e, g, 0)),
                   pl.BlockSpec((None, slots, 1), lambda g, e, cj: (e, g, 0))],
        scratch_shapes=[pltpu.VMEM((slots, D_MODEL), F32), pltpu.VMEM((slots, 1), F32)])
    return pl.pallas_call(
        functools.partial(_gather_kernel, cfg),
        grid_spec=grid_spec,
        out_shape=[jax.ShapeDtypeStruct((N_EXPERTS, cfg.groups * slots, D_MODEL), BF16),
                   jax.ShapeDtypeStruct((N_EXPERTS, cfg.groups * slots, 1), F32)],
        compiler_params=_cp(("parallel", "arbitrary")),
        name="moe_gather",
    )(cj.reshape(-1), h, pos4, aff4)


def _ffn_kernel(rows, x_ref, gate_ref, w1_ref, w3_ref, w2_ref, y_ref, acc_ref, w1b, w3b, w2b):
    c = pl.program_id(1)
    w1b[...] = w1_ref[...].astype(BF16)
    w3b[...] = w3_ref[...].astype(BF16)
    w2b[...] = w2_ref[...].astype(BF16)
    rt = min(ROW_TILE, rows)

    def body(i, _):
        r = pl.multiple_of(i * rt, rt)
        x = x_ref[pl.ds(r, rt), :]
        a = _dot(x, w1b[...])
        b = _dot(x, w3b[...])
        hid = (a * jax.nn.sigmoid(a) * b).astype(BF16)
        y = _dot(hid, w2b[...])

        @pl.when(c == 0)
        def _():
            acc_ref[pl.ds(r, rt), :] = y

        @pl.when(c > 0)
        def _():
            acc_ref[pl.ds(r, rt), :] += y

        return 0

    lax.fori_loop(0, rows // rt, body, 0)

    @pl.when(c == pl.num_programs(1) - 1)
    def _():
        y_ref[...] = (acc_ref[...] * gate_ref[...]).astype(BF16)


def moe_ffn(cfg, layer, xg, gate, w1, w3, w2):
    rows = cfg.groups * cfg.slots
    nff = EXPERT_FF // FF_TILE
    return pl.pallas_call(
        functools.partial(_ffn_kernel, rows),
        grid=(N_EXPERTS, nff),
        in_specs=[pl.BlockSpec((None, rows, D_MODEL), lambda e, c: (e, 0, 0)),
                  pl.BlockSpec((None, rows, 1), lambda e, c: (e, 0, 0)),
                  pl.BlockSpec((None, None, D_MODEL, FF_TILE), lambda e, c: (layer, e, 0, c)),
                  pl.BlockSpec((None, None, D_MODEL, FF_TILE), lambda e, c: (layer, e, 0, c)),
                  pl.BlockSpec((None, None, FF_TILE, D_MODEL), lambda e, c: (layer, e, c, 0))],
        out_specs=pl.BlockSpec((None, rows, D_MODEL), lambda e, c: (e, 0, 0)),
        out_shape=jax.ShapeDtypeStruct((N_EXPERTS, rows, D_MODEL), BF16),
        scratch_shapes=[pltpu.VMEM((rows, D_MODEL), F32),
                        pltpu.VMEM((D_MODEL, FF_TILE), BF16), pltpu.VMEM((D_MODEL, FF_TILE), BF16),
                        pltpu.VMEM((FF_TILE, D_MODEL), BF16)],
        compiler_params=_cp(("parallel", "arbitrary")),
        name="moe_ffn",
    )(xg, gate, w1, w3, w2)


def _scatter_kernel(cfg, final, cj_ref, x_ref, pos_ref, y_ref, g_ref, fw_ref, o_ref, acc_ref):
    g, tb = pl.program_id(0), pl.program_id(1)
    sb = _slot_block(cfg)
    acc_ref[...] = jnp.zeros_like(acc_ref)
    lane = lax.broadcasted_iota(I32, (LANES, sb), 1)
    for e in range(N_EXPERTS):
        base = (g * N_EXPERTS + e) * LANES
        s0, s1 = cj_ref[base + tb], cj_ref[base + tb + 1]
        for jb in range(cfg.slots // sb):
            @pl.when((s1 > s0) & (s0 < (jb + 1) * sb) & (s1 > jb * sb))
            def _():
                hit = jnp.broadcast_to(pos_ref[e], (LANES, sb)) == (lane + jb * sb)
                acc_ref[...] += _dot(hit.astype(BF16), y_ref[e, jb * sb:(jb + 1) * sb, :])
    out = x_ref[...] + g_ref[...] * acc_ref[...]
    if final:
        out = _rms(out, fw_ref[...])
    o_ref[...] = out


def moe_scatter(cfg, layer, cj, x, pos, y, mod5, final_w, final):
    gs, slots = cfg.gs, cfg.slots
    nt = gs // LANES
    pos_col = pos.reshape(cfg.groups, N_EXPERTS, gs, 1)
    grid_spec = pltpu.PrefetchScalarGridSpec(
        num_scalar_prefetch=1,
        grid=(cfg.groups, nt),
        in_specs=[pl.BlockSpec((LANES, D_MODEL), lambda g, tb, cj: (g * nt + tb, 0)),
                  pl.BlockSpec((None, N_EXPERTS, LANES, 1), lambda g, tb, cj: (g, 0, tb, 0)),
                  pl.BlockSpec((N_EXPERTS, slots, D_MODEL), lambda g, tb, cj: (0, g, 0)),
                  pl.BlockSpec((None, None, None, 1, D_MODEL), lambda g, tb, cj: (layer, g, 5, 0, 0)),
                  pl.BlockSpec((1, D_MODEL), lambda g, tb, cj: (0, 0))],
        out_specs=pl.BlockSpec((LANES, D_MODEL), lambda g, tb, cj: (g * nt + tb, 0)),
        scratch_shapes=[pltpu.VMEM((LANES, D_MODEL), F32)])
    return pl.pallas_call(
        functools.partial(_scatter_kernel, cfg, final),
        grid_spec=grid_spec,
        out_shape=jax.ShapeDtypeStruct((cfg.t, D_MODEL), F32),
        compiler_params=_cp(("parallel", "arbitrary")),
        name="moe_scatter",
    )(cj.reshape(-1), x, pos_col, y, mod5, final_w.reshape(1, D_MODEL))


def _forward(cfg, x_prompt, x_sample, state_s5, cache_k, cache_v, state_hgrn, state_ret, c, c_ctx, p):
    assert cfg.tc == cfg.lat_n and cfg.ctx_n % ROW_TILE == 0 and cfg.lat_n % (GRID_W * 2) == 0
    x = jnp.concatenate([x_prompt.reshape(cfg.tc, D_MODEL), x_sample.reshape(-1, D_MODEL)], axis=0)
    conds = jnp.zeros((8, D_MODEL), F32).at[0].set(c_ctx).at[1:1 + cfg.lat_b].set(c)
    mod = adaln_table(conds, p['mod_w'], p['mod_b'])
    mod5 = mod[:, :cfg.groups].reshape(DEPTH, cfg.groups, 6, 1, D_MODEL)
    rope_att = _rope_tables(cfg.lat_n, ATT_HEAD_DIM, 512)
    rope_ret = _rope_tables(cfg.lat_n, HEAD_DIM, 512)
    rec_a = _rec_tables()
    amat = jnp.asarray(rec_a, BF16)
    acnt = jnp.broadcast_to(jnp.asarray(rec_a.sum(-1, keepdims=True), F32), rec_a.shape)
    s5_fin, k_out, v_out, hgrn_fin, ret_fin = [], [], [], [], []
    for layer in range(DEPTH):
        if layer % 2 == 0:
            e = layer // 2
            proj = norm_mod_matmul(cfg, x, p['norm1_w'][layer], mod5, layer, p['ab_w_in'][e].astype(BF16))
            wcat, mmat, gmat, a16 = _s5_tables(p['s5_lambda_re'][e], p['s5_lambda_im'][e], p['s5_log_dt'][e],
                                               p['s5_b_re'][e], p['s5_b_im'][e], p['s5_c_re'][e], p['s5_c_im'][e])
            h0 = state_s5[:, e].reshape(cfg.lat_b, 4, S5_PAIRS, 2 * S5_STATE).transpose(2, 0, 1, 3)
            yd_pairs, fin = s5_mix(cfg, _to_pairs(cfg, proj[:, :S5_WIDTH]), wcat, mmat, gmat, a16, h0)
            yd = _from_pairs(cfg, yd_pairs)
            s5_fin.append(fin.transpose(2, 1, 0, 3).reshape(cfg.ctx_b, 2, 2, S5_GROUPS, S5_STATE))
            ck = cache_k[:, e].reshape(cfg.lat_b, cfg.past, LANES)
            cv = cache_v[:, e].reshape(cfg.lat_b, cfg.past, LANES)
            yatt = attention(cfg, proj, p['attn_sink'][e], ck, cv, rope_att)
            k_out.append(proj[:cfg.tc, 1024:1152].reshape(cfg.ctx_b, cfg.ctx_n, ATT_KV_HEADS, ATT_HEAD_DIM))
            v_out.append(proj[:cfg.tc, 1152:1280].reshape(cfg.ctx_b, cfg.ctx_n, ATT_KV_HEADS, ATT_HEAD_DIM))
            x = even_out(cfg, x, proj, yd, yatt, p['s5_d'][e], p['s5_glu_w'][e].astype(BF16),
                         p['ab_w_out'][e].astype(BF16), mod5, layer)
        else:
            o = layer // 2
            proj = norm_mod_matmul(cfg, x, p['norm1_w'][layer], mod5, layer, p['cd_w_in'][o].astype(BF16))
            s0 = jnp.concatenate([state_hgrn[:, o], state_ret[:, o]], axis=2).transpose(1, 0, 2, 4, 3)
            rdec = jnp.repeat(p['ret_decay'][o], HEAD_DIM, axis=-1)
            o2, st = recurrences(cfg, o, proj, amat, acnt, p['hgrn_lb_logits'], rdec[None], rope_ret, s0)
            st = st[:, :cfg.ctx_b].transpose(1, 0, 2, 4, 3)
            hgrn_fin.append(st[:, :, :REC_HEADS])
            ret_fin.append(st[:, :, REC_HEADS:])
            x = odd_out(cfg, x, o2, proj, p['hgrn_norm_w'][o], p['ret_norm_w'][o],
                        p['cd_w_out'][o].astype(BF16), mod5, layer)
        h2, pos, aff, cj = moe_route(cfg, x, p['norm2_w'][layer], mod5, layer,
                                     p['router_w'][layer].T.astype(BF16))
        xg, gate = moe_gather(cfg, cj, h2, pos, aff)
        yg = moe_ffn(cfg, layer, xg, gate, p['moe_w1'], p['moe_w3'], p['moe_w2'])
        x = moe_scatter(cfg, layer, cj, x, pos, yg, mod5, p['final_norm_w'], layer == DEPTH - 1)
    y_prompt = x[:cfg.tc].reshape(cfg.ctx_b, cfg.ctx_n, D_MODEL)
    y_sample = x[cfg.tc:].reshape(cfg.lat_b, cfg.lat_n, D_MODEL)
    return (y_prompt, y_sample, jnp.stack(s5_fin, axis=1), jnp.stack(k_out, axis=1), jnp.stack(v_out, axis=1),
            jnp.stack(hgrn_fin, axis=1), jnp.stack(ret_fin, axis=1))


def kernel(x_prompt, x_sample, state_s5, cache_k, cache_v, state_hgrn, state_ret, c, c_ctx, mod_w, mod_b, norm1_w, norm2_w, final_norm_w, ab_w_in, ab_w_out, s5_lambda_re, s5_lambda_im, s5_log_dt, s5_b_re, s5_b_im, s5_c_re, s5_c_im, s5_d, s5_glu_w, attn_sink, cd_w_in, cd_w_out, hgrn_lb_logits, hgrn_norm_w, ret_decay, ret_norm_w, router_w, moe_w1, moe_w3, moe_w2):
    cfg = Cfg(ctx_b=x_prompt.shape[0], ctx_n=x_prompt.shape[1], lat_b=x_sample.shape[0],
              lat_n=x_sample.shape[1], past=cache_k.shape[2])
    p = dict(mod_w=mod_w, mod_b=mod_b, norm1_w=norm1_w, norm2_w=norm2_w, final_norm_w=final_norm_w,
             ab_w_in=ab_w_in, ab_w_out=ab_w_out, s5_lambda_re=s5_lambda_re, s5_lambda_im=s5_lambda_im,
             s5_log_dt=s5_log_dt, s5_b_re=s5_b_re, s5_b_im=s5_b_im, s5_c_re=s5_c_re, s5_c_im=s5_c_im,
             s5_d=s5_d, s5_glu_w=s5_glu_w, attn_sink=attn_sink, cd_w_in=cd_w_in, cd_w_out=cd_w_out,
             hgrn_lb_logits=hgrn_lb_logits, hgrn_norm_w=hgrn_norm_w, ret_decay=ret_decay,
             ret_norm_w=ret_norm_w, router_w=router_w, moe_w1=moe_w1, moe_w3=moe_w3, moe_w2=moe_w2)
    return _forward(cfg, x_prompt, x_sample, state_s5, cache_k, cache_v, state_hgrn, state_ret, c, c_ctx, p)
```

```python
import functools
import math
from typing import NamedTuple

import numpy as np
import jax
import jax.numpy as jnp
from jax import lax
from jax.experimental import pallas as pl
from jax.experimental.pallas import tpu as pltpu

F32 = jnp.float32
BF16 = jnp.bfloat16
I32 = jnp.int32

D_MODEL = 1024
DEPTH = 4
N_EVEN = 2
N_ODD = 2
EPS = 1e-6
GRID_W = 64
S5_WIDTH = 512
S5_GROUP = 16
S5_GROUPS = 32
S5_STATE = 64
S5_CHUNK = 16
S5_PAIRS = S5_GROUPS // 2
S5_QUAD = 4
ATT_HEAD_DIM = 64
ATT_HEADS = 8
ATT_KV_HEADS = 2
ATT_GROUP = 4
WINDOW = 128
ROPE_BASE = 10000.0
AB_IN = 1280
HEAD_DIM = 128
REC_HEADS = 4
CD_IN = 4608
N_EXPERTS = 16
EXPERT_FF = 1536
CAPACITY_FACTOR = 2

LANES = 128
ROW_TILE = 256
FF_TILE = 512
F32_INF_BITS = 0x7F800000
F32_MIN_NORMAL_BITS = 0x00800000
VMEM_LIMIT = 56 * 1024 * 1024


class Cfg(NamedTuple):
    ctx_b: int
    ctx_n: int
    lat_b: int
    lat_n: int
    past: int

    @property
    def tc(self):
        return self.ctx_b * self.ctx_n

    @property
    def t(self):
        return self.tc + self.lat_b * self.lat_n

    @property
    def gs(self):
        return self.lat_n

    @property
    def groups(self):
        return 1 + self.lat_b

    @property
    def slots(self):
        return CAPACITY_FACTOR * self.gs // N_EXPERTS


def _cp(sem, vmem=VMEM_LIMIT):
    return pltpu.CompilerParams(dimension_semantics=sem, vmem_limit_bytes=vmem)


def _nt(a, b):
    return lax.dot_general(a, b, (((1,), (1,)), ((), ())), preferred_element_type=F32)


def _dot(a, b, precision=None):
    return jnp.dot(a, b, preferred_element_type=F32, precision=precision)


_HI = lax.Precision.HIGHEST


def _adaln_kernel(c_ref, w_ref, b_ref, o_ref):
    c = c_ref[...]
    s = c * jax.nn.sigmoid(c)
    o_ref[0] = _dot(s.astype(BF16), w_ref[0].astype(BF16)) + b_ref[0]


def adaln_table(conds, mod_w, mod_b):
    n6 = 6 * D_MODEL
    tn = 1536
    return pl.pallas_call(
        _adaln_kernel,
        grid=(DEPTH, n6 // tn),
        in_specs=[pl.BlockSpec((8, D_MODEL), lambda l, j: (0, 0)),
                  pl.BlockSpec((1, D_MODEL, tn), lambda l, j: (l, 0, j)),
                  pl.BlockSpec((1, 1, tn), lambda l, j: (l, 0, j))],
        out_specs=pl.BlockSpec((1, 8, tn), lambda l, j: (l, 0, j)),
        out_shape=jax.ShapeDtypeStruct((DEPTH, 8, n6), F32),
        compiler_params=_cp(("parallel", "parallel")),
        name="adaln_table",
    )(conds, mod_w, mod_b.reshape(DEPTH, 1, n6))


def _mod_spec(cfg, layer, part):
    tiles_per_group = cfg.gs // ROW_TILE
    return pl.BlockSpec((None, None, None, 1, D_MODEL),
                        lambda i: (layer, i // tiles_per_group, part, 0, 0))


def _rms(x, w):
    return x * lax.rsqrt(jnp.mean(x * x, axis=-1, keepdims=True) + EPS) * w


def _nmm_kernel(x_ref, nw_ref, sh_ref, sc_ref, w_ref, o_ref):
    h = _rms(x_ref[...], nw_ref[...]) * (1.0 + sc_ref[...]) + sh_ref[...]
    o_ref[...] = _dot(h.astype(BF16), w_ref[...])


def norm_mod_matmul(cfg, x, nw, mod5, layer, w_bf16):
    n_out = w_bf16.shape[1]
    return pl.pallas_call(
        _nmm_kernel,
        grid=(cfg.t // ROW_TILE,),
        in_specs=[pl.BlockSpec((ROW_TILE, D_MODEL), lambda i: (i, 0)),
                  pl.BlockSpec((1, D_MODEL), lambda i: (0, 0)),
                  _mod_spec(cfg, layer, 0), _mod_spec(cfg, layer, 1),
                  pl.BlockSpec((D_MODEL, n_out), lambda i: (0, 0))],
        out_specs=pl.BlockSpec((ROW_TILE, n_out), lambda i: (i, 0)),
        out_shape=jax.ShapeDtypeStruct((cfg.t, n_out), F32),
        compiler_params=_cp(("parallel",)),
        name="norm_mod_matmul",
    )(x, nw.reshape(1, D_MODEL), mod5, mod5, w_bf16)


def _s5_tables(lam_re, lam_im, log_dt, b_re, b_im, c_re, c_im):
    L = S5_CHUNK
    lr, li = lam_re.astype(F32), lam_im.astype(F32)
    dt = jnp.exp(log_dt.astype(F32))[..., None]
    mag = jnp.exp(lr * dt)
    ar, ai = mag * jnp.cos(li * dt), mag * jnp.sin(li * dt)
    den = lr * lr + li * li
    cr = ((ar - 1.0) * lr + ai * li) / den
    ci = (ai * lr - (ar - 1.0) * li) / den
    br, bi = b_re.astype(F32), b_im.astype(F32)
    bbr = cr[..., None] * br - ci[..., None] * bi
    bbi = cr[..., None] * bi + ci[..., None] * br

    def pw_step(carry, _):
        pr, pi = carry
        return (pr * ar - pi * ai, pr * ai + pi * ar), (pr, pi)

    (_, _), (pr, pi) = lax.scan(pw_step, (jnp.ones_like(ar), jnp.zeros_like(ar)), None, length=L + 1)
    er = pr[:L, ..., None] * bbr[None] - pi[:L, ..., None] * bbi[None]
    ei = pr[:L, ..., None] * bbi[None] + pi[:L, ..., None] * bbr[None]
    ccr, cci = c_re.astype(F32), c_im.astype(F32)
    kt = (jnp.einsum('dgcp,tdgpe->tdgce', ccr, er, precision=_HI)
          - jnp.einsum('dgcp,tdgpe->tdgce', cci, ei, precision=_HI))
    s = np.arange(L)
    lag_f = s[:, None] - s[None, :]
    m_f = jnp.where((lag_f >= 0)[:, :, None, None, None], kt[np.clip(lag_f, 0, L - 1), 0], 0.0)
    m_b = jnp.where((lag_f <= 0)[:, :, None, None, None], kt[np.clip(-lag_f, 0, L - 1), 1], 0.0)
    m = (m_f + m_b).transpose(2, 1, 4, 0, 3)
    m = m.reshape(S5_GROUPS, L * S5_GROUP, L * S5_GROUP)
    wf_r, wf_i = er[::-1, 0], ei[::-1, 0]
    wb_r, wb_i = er[:, 1], ei[:, 1]

    def w_mat(w):
        return w.transpose(1, 0, 3, 2).reshape(S5_GROUPS, L * S5_GROUP, S5_STATE)

    def g_mats(d, p_r, p_i):
        g_r = ccr[d][None] * p_r[:, :, None, :] - cci[d][None] * p_i[:, :, None, :]
        g_i = -ccr[d][None] * p_i[:, :, None, :] - cci[d][None] * p_r[:, :, None, :]
        to = lambda g: g.transpose(1, 3, 0, 2).reshape(S5_GROUPS, S5_STATE, L * S5_GROUP)
        return to(g_r), to(g_i)

    gf_r, gf_i = g_mats(0, pr[1:L + 1, 0], pi[1:L + 1, 0])
    gb_r, gb_i = g_mats(1, pr[L:0:-1, 1], pi[L:0:-1, 1])

    def pair_blockdiag(a):
        g, r, c = a.shape
        a = a.reshape(g // 2, 2, r, c)
        z = jnp.zeros_like(a[:, 0])
        top = jnp.concatenate([a[:, 0], z], axis=2)
        bot = jnp.concatenate([z, a[:, 1]], axis=2)
        return jnp.concatenate([top, bot], axis=1)

    wcat = jnp.concatenate([pair_blockdiag(w_mat(w)) for w in (wf_r, wf_i, wb_r, wb_i)], axis=2)
    mmat = pair_blockdiag(m)
    gmat = jnp.concatenate([pair_blockdiag(g) for g in (gf_r, gf_i, gb_r, gb_i)], axis=1)

    def pair_lanes(a):
        return a.reshape(S5_PAIRS, 2 * S5_STATE)

    a16 = jnp.stack([pair_lanes(pr[L, 0]), pair_lanes(pi[L, 0]),
                     pair_lanes(pr[L, 1]), pair_lanes(pi[L, 1])], axis=1)
    perm = np.arange(2 * L * S5_GROUP).reshape(2, L, S5_GROUP).transpose(1, 0, 2).reshape(-1)
    wcat = wcat[:, perm, :]
    mmat = mmat[:, perm, :][:, :, perm]
    gmat = gmat[:, :, perm]
    return wcat.astype(BF16), mmat.astype(BF16), gmat.astype(BF16), a16


def _s5_pair(cfg, u, w_mat, m_mat, g_mat, a, h0_ref, pp, fin_ref, xs_ref):
    rc = cfg.tc // S5_CHUNK
    lc = cfg.ctx_n // S5_CHUNK
    ll = cfg.lat_n // S5_CHUNK
    r_all = cfg.t // S5_CHUNK
    w = _dot(u, w_mat)
    row =lax.broadcasted_iota(I32, (r_all, LANES), 0)
    is_ctx = row < rc
    pos = jnp.where(is_ctx, row % lc, (row - rc) % ll)
    seg = jnp.where(is_ctx, lc, ll)
    prev_parts = []
    for d in range(2):
        xr = w[:, (2 * d) * LANES:(2 * d + 1) * LANES]
        xi = w[:, (2 * d + 1) * LANES:(2 * d + 2) * LANES]
        ar, ai = a[2 * d:2 * d + 1], a[2 * d + 1:2 * d + 2]
        first = 0 if d == 0 else seg - 1
        h0r = jnp.zeros((r_all, LANES), F32)
        h0i = jnp.zeros((r_all, LANES), F32)
        for b in range(cfg.lat_b):
            sel = (row >= rc + b * ll) & (row < rc + (b + 1) * ll)
            h0r = jnp.where(sel, h0_ref[pp, b, 2 * d:2 * d + 1, :], h0r)
            h0i = jnp.where(sel, h0_ref[pp, b, 2 * d + 1:2 * d + 2, :], h0i)
        at_first = pos == first
        xr = xr + jnp.where(at_first, ar * h0r - ai * h0i, 0.0)
        xi = xi + jnp.where(at_first, ar * h0i + ai * h0r, 0.0)
        pr, pi = ar, ai
        step = 1
        while step < max(lc, ll):
            if d == 0:
                sr, si = pltpu.roll(xr, step, 0), pltpu.roll(xi, step, 0)
                ok = (pos >= step)
            else:
                sr, si = pltpu.roll(xr, r_all - step, 0), pltpu.roll(xi, r_all - step, 0)
                ok = (pos < seg - step)
            xr, xi = (xr + jnp.where(ok, pr * sr - pi * si, 0.0),
                      xi + jnp.where(ok, pr * si + pi * sr, 0.0))
            pr, pi = pr * pr - pi * pi, 2.0 * pr * pi
            step *= 2
        if d == 0:
            nr, ni = pltpu.roll(xr, 1, 0), pltpu.roll(xi, 1, 0)
        else:
            nr, ni = pltpu.roll(xr, r_all - 1, 0), pltpu.roll(xi, r_all - 1, 0)
        prev_parts += [jnp.where(at_first, h0r, nr), jnp.where(at_first, h0i, ni)]
        xs_ref[2 * d] = xr
        xs_ref[2 * d + 1] = xi
    xin = jnp.concatenate(prev_parts, axis=1).astype(BF16)
    for d in range(2):
        start = lc - 1 if d == 0 else 0
        for k in range(2):
            fin_ref[pp, 2 * d + k] = xs_ref[2 * d + k, pl.ds(start, cfg.ctx_b, stride=lc), :]
    return _dot(u, m_mat) + _dot(xin, g_mat)


def _s5_kernel(cfg, u_ref, w_ref, m_ref, g_ref, a_ref, h0_ref, y_ref, fin_ref, uflat_ref, ystage_ref, xs_ref):
    r_all = cfg.t // S5_CHUNK
    sw = 2 * S5_GROUP
    for s in range(S5_CHUNK):
        blk = u_ref[pl.ds(s, r_all, stride=S5_CHUNK), :]
        for pp in range(S5_QUAD):
            uflat_ref[pp, :, s * sw:(s + 1) * sw] = blk[:, pp * sw:(pp + 1) * sw]
    for pp in range(S5_QUAD):
        y = _s5_pair(cfg, uflat_ref[pp].astype(BF16), w_ref[pp], m_ref[pp], g_ref[pp], a_ref[pp],
                     h0_ref, pp, fin_ref, xs_ref)
        for s in range(S5_CHUNK):
            ystage_ref[s, :, pp * sw:(pp + 1) * sw] = y[:, s * sw:(s + 1) * sw]
    for s in range(S5_CHUNK):
        y_ref[pl.ds(s, r_all, stride=S5_CHUNK), :] = ystage_ref[s]


def s5_mix(cfg, proj, wcat, mmat, gmat, a16, h0):
    r_all = cfg.t // S5_CHUNK
    wide = 2 * S5_CHUNK * S5_GROUP
    mat = pl.BlockSpec((S5_QUAD, wide, wide), lambda q: (q, 0, 0))
    return pl.pallas_call(
        functools.partial(_s5_kernel, cfg),
        grid=(S5_PAIRS // S5_QUAD,),
        in_specs=[pl.BlockSpec((cfg.t, LANES), lambda q: (0, q)), mat, mat, mat,
                  pl.BlockSpec((S5_QUAD, 4, LANES), lambda q: (q, 0, 0)),
                  pl.BlockSpec((S5_QUAD, cfg.lat_b, 4, LANES), lambda q: (q, 0, 0, 0))],
        out_specs=[pl.BlockSpec((cfg.t, LANES), lambda q: (0, q)),
                   pl.BlockSpec((S5_QUAD, 4, cfg.ctx_b, LANES), lambda q: (q, 0, 0, 0))],
        out_shape=[jax.ShapeDtypeStruct((cfg.t, S5_WIDTH), F32),
                   jax.ShapeDtypeStruct((S5_PAIRS, 4, cfg.ctx_b, LANES), F32)],
        scratch_shapes=[pltpu.VMEM((S5_QUAD, r_all, wide), F32),
                        pltpu.VMEM((S5_CHUNK, r_all, LANES), F32),
                        pltpu.VMEM((4, r_all, LANES), F32)],
        compiler_params=_cp(("parallel",)),
        name="s5_mix",
    )(proj, wcat, mmat, gmat, a16, h0)


def _rope(x, cos, sin_signed, half):
    w = x.shape[-1]
    lane = lax.broadcasted_iota(I32, x.shape, x.ndim - 1)
    swapped = jnp.where((lane % (2 * half)) < half,
                        pltpu.roll(x, w - half, x.ndim - 1), pltpu.roll(x, half, x.ndim - 1))
    return x * cos + swapped * sin_signed


def _attn_kernel(n_parts, latent, n_q_blocks, sink_ref, q_ref, *refs):
    hd = ATT_HEAD_DIM
    scale = hd ** -0.5
    if latent:
        (kp_ref, kc_ref, kn_ref, vp_ref, vc_ref, vn_ref, ck_ref, cv_ref,
         cq_ref, sq_ref, ckp_ref, skp_ref, ckc_ref, skc_ref, ckn_ref, skn_ref, o_ref) = refs
        i = pl.program_id(1)
        q = _rope(q_ref[...], cq_ref[...], sq_ref[...], hd // 2)
        kparts = [_rope(kp_ref[...], ckp_ref[...], skp_ref[...], hd // 2),
                  _rope(kc_ref[...], ckc_ref[...], skc_ref[...], hd // 2),
                  _rope(kn_ref[...], ckn_ref[...], skn_ref[...], hd // 2)]
        kband = jnp.concatenate(kparts, axis=0)
        vband = jnp.concatenate([vp_ref[...], vc_ref[...], vn_ref[...]], axis=0)
        keys = [kband, ck_ref[...]]
        vals = [vband, cv_ref[...]]
        nrow = ATT_GROUP * LANES
        qpos = lax.broadcasted_iota(I32, (nrow, 3 * LANES), 0) % LANES
        kpos = lax.broadcasted_iota(I32, (nrow, 3 * LANES), 1) - LANES
        kabs = kpos + i * LANES
        band_ok = (jnp.abs(kpos - qpos) <= WINDOW) & (kabs >= 0) & (kabs < n_q_blocks * LANES)
        masks = [band_ok, None]
    else:
        k_ref, v_ref, o_ref = refs
        q = q_ref[...]
        keys = [k_ref[...]]
        vals = [v_ref[...]]
        masks = [None]
    for kv in range(ATT_KV_HEADS):
        qs = jnp.concatenate([q[:, (kv * ATT_GROUP + g) * hd:(kv * ATT_GROUP + g + 1) * hd]
                              for g in range(ATT_GROUP)], axis=0).astype(BF16)
        sink_col = jnp.concatenate([jnp.full((LANES, 1), sink_ref[kv * ATT_GROUP + g], F32)
                                    for g in range(ATT_GROUP)], axis=0)
        logits = []
        mx = sink_col
        for kk, msk in zip(keys, masks):
            s = _nt(qs, kk[:, kv * hd:(kv + 1) * hd].astype(BF16)) * scale
            if msk is not None:
                s = jnp.where(msk, s, -1e30)
            logits.append(s)
            mx = jnp.maximum(mx, jnp.max(s, axis=-1, keepdims=True))
        den = jnp.exp(sink_col - mx)
        acc = None
        for s, vv in zip(logits, vals):
            p = jnp.exp(s - mx)
            den = den + jnp.sum(p, axis=-1, keepdims=True)
            o = _dot(p.astype(BF16), vv[:, kv * hd:(kv + 1) * hd].astype(BF16))
            acc = o if acc is None else acc + o
        out = acc / den
        for g in range(ATT_GROUP):
            h = kv * ATT_GROUP + g
            o_ref[:, h * hd:(h + 1) * hd] = out[g * LANES:(g + 1) * LANES, :]


def attention(cfg, proj, sink, cache_k, cache_v, rope_cs):
    qcol = S5_WIDTH // 512
    kcol = (S5_WIDTH + 512) // LANES
    vcol = kcol + 1
    smem = pl.BlockSpec(memory_space=pltpu.SMEM)
    nqc = cfg.ctx_n // LANES
    y_ctx = pl.pallas_call(
        functools.partial(_attn_kernel, 1, False, nqc),
        grid=(cfg.ctx_b, nqc),
        in_specs=[smem,
                  pl.BlockSpec((LANES, 512), lambda b, i: (b * nqc + i, qcol)),
                  pl.BlockSpec((cfg.ctx_n, LANES), lambda b, i: (b, kcol)),
                  pl.BlockSpec((cfg.ctx_n, LANES), lambda b, i: (b, vcol))],
        out_specs=pl.BlockSpec((LANES, 512), lambda b, i: (b * nqc + i, 0)),
        out_shape=jax.ShapeDtypeStruct((cfg.tc, 512), F32),
        compiler_params=_cp(("parallel", "parallel")),
        name="attn_context",
    )(sink, proj, proj, proj)
    nql = cfg.lat_n // LANES
    off = cfg.tc // LANES
    cos_t, sin_t = rope_cs

    def rb(b, i):
        return off + b * nql + i

    def prev(i):
        return jnp.maximum(i - 1, 0)

    def nxt(i):
        return jnp.minimum(i + 1, nql - 1)

    kspec = lambda f, col: pl.BlockSpec((LANES, LANES), lambda b, i: (rb(b, f(i)), col))
    tspec = lambda f: pl.BlockSpec((LANES, LANES), lambda b, i: (f(i), 0))
    same = lambda i: i
    y_lat = pl.pallas_call(
        functools.partial(_attn_kernel, 2, True, nql),
        grid=(cfg.lat_b, nql),
        in_specs=[smem,
                  pl.BlockSpec((LANES, 512), lambda b, i: (rb(b, i), qcol)),
                  kspec(prev, kcol), kspec(same, kcol), kspec(nxt, kcol),
                  kspec(prev, vcol), kspec(same, vcol), kspec(nxt, vcol),
                  pl.BlockSpec((None, cfg.past, LANES), lambda b, i: (b, 0, 0)),
                  pl.BlockSpec((None, cfg.past, LANES), lambda b, i: (b, 0, 0)),
                  pl.BlockSpec((LANES, 512), lambda b, i: (i, 0)),
                  pl.BlockSpec((LANES, 512), lambda b, i: (i, 0)),
                  tspec(prev), tspec(prev), tspec(same), tspec(same), tspec(nxt), tspec(nxt)],
        out_specs=pl.BlockSpec((LANES, 512), lambda b, i: (b * nql + i, 0)),
        out_shape=jax.ShapeDtypeStruct((cfg.lat_b * cfg.lat_n, 512), F32),
        compiler_params=_cp(("parallel", "parallel")),
        name="attn_latent",
    )(sink, proj, proj, proj, proj, proj, proj, proj, cache_k, cache_v,
      cos_t, sin_t, cos_t, sin_t, cos_t, sin_t, cos_t, sin_t)
    return jnp.concatenate([y_ctx, y_lat], axis=0)


def _rope_tables(n, head, width):
    rows = jnp.repeat(jnp.arange(n // GRID_W, dtype=F32), GRID_W)
    cols = (jnp.arange(n) % GRID_W).astype(F32)
    nf = head // 4
    inv = ROPE_BASE ** (-jnp.arange(nf, dtype=F32) / nf)
    ang = jnp.concatenate([rows[:, None] * inv, cols[:, None] * inv], axis=-1)
    c, s = jnp.cos(ang), jnp.sin(ang)
    cos_h = jnp.concatenate([c, c], axis=-1)
    sin_h = jnp.concatenate([-s, s], axis=-1)
    reps = width // head
    return jnp.tile(cos_h, (1, reps)), jnp.tile(sin_h, (1, reps))


def _even_out_kernel(x_ref, u_ref, yd_ref, ya_ref, d_ref, glu_ref, wo_ref, g_ref, o_ref):
    y = jax.nn.gelu(u_ref[...] * d_ref[...] + yd_ref[...])
    z = y * jax.nn.sigmoid(_dot(y.astype(BF16), glu_ref[...]))
    cat = jnp.concatenate([z, ya_ref[...]], axis=-1).astype(BF16)
    o_ref[...] = x_ref[...] + g_ref[...] * _dot(cat, wo_ref[...])


def even_out(cfg, x, proj, yd, yatt, s5_d, glu_bf16, wo_bf16, mod5, layer):
    tile = lambda w: pl.BlockSpec((ROW_TILE, w), lambda i: (i, 0))
    return pl.pallas_call(
        _even_out_kernel,
        grid=(cfg.t // ROW_TILE,),
        in_specs=[tile(D_MODEL), tile(S5_WIDTH), tile(S5_WIDTH), tile(512),
                  pl.BlockSpec((1, S5_WIDTH), lambda i: (0, 0)),
                  pl.BlockSpec((S5_WIDTH, S5_WIDTH), lambda i: (0, 0)),
                  pl.BlockSpec((D_MODEL, D_MODEL), lambda i: (0, 0)),
                  _mod_spec(cfg, layer, 2)],
        out_specs=tile(D_MODEL),
        out_shape=jax.ShapeDtypeStruct((cfg.t, D_MODEL), F32),
        compiler_params=_cp(("parallel",)),
        name="even_out",
    )(x, proj, yd, yatt, s5_d.reshape(1, S5_WIDTH), glu_bf16, wo_bf16, mod5)


REC_CHUNK = 128
REC_LEVELS = 7


def _rec_tables():
    c = REC_CHUNK
    t = np.arange(c)[:, None]
    u = np.arange(c)[None, :]
    out = np.zeros((2, (REC_LEVELS + 2) * c, c), np.float32)
    for d in range(2):
        for lv in range(REC_LEVELS):
            h = 1 << lv
            mid = (t // (2 * h)) * (2 * h) + h
            second = (t % (2 * h)) >= h
            if d == 0:
                a = np.where(second, (u >= mid) & (u <= t), (u > t) & (u < mid))
            else:
                a = np.where(second, (u >= mid) & (u < t), (u >= t) & (u < mid))
            out[d, lv * c:(lv + 1) * c] = a
        if d == 0:
            out[d, REC_LEVELS * c:(REC_LEVELS + 1) * c] = u <= t
            out[d, (REC_LEVELS + 1) * c:] = u > t
        else:
            out[d, REC_LEVELS * c:(REC_LEVELS + 1) * c] = u >= t
            out[d, (REC_LEVELS + 1) * c:] = u < t
    return out


def _rec_kernel(cfg, layer_o, q_ref, zf_ref, zb_ref, v_ref, rq_ref, rk_ref, rv_ref,
                amat_ref, lbl_ref, rdec_ref, cos_ref, sin_ref, s0_ref, o_ref, st_ref):
    c = REC_CHUNK
    d = pl.program_id(0)
    step = pl.program_id(1)
    n_chunks = cfg.t // c
    chunk = jnp.where(d == 0, step, n_chunks - 1 - step)
    cpc = cfg.ctx_n // c
    cpl = cfg.lat_n // c
    n_ctx_chunks = cfg.tc // c
    is_lat = chunk >= n_ctx_chunks
    pos = jnp.where(is_lat, (chunk - n_ctx_chunks) % cpl, chunk % cpc)
    last = jnp.where(is_lat, cpl - 1, cpc - 1)
    first_processed = jnp.where(d == 0, pos == 0, pos == last)

    @pl.when(first_processed & jnp.logical_not(is_lat))
    def _():
        st_ref[...] = jnp.zeros_like(st_ref)

    @pl.when(first_processed & is_lat)
    def _():
        st_ref[...] = s0_ref[...]

    row = lax.broadcasted_iota(I32, (c, c), 0)
    col = lax.broadcasted_iota(I32, (c, c), 1)
    fwd = d == 0
    amat = amat_ref[...]

    def mix_head(q, k, v, scores, e_q, e_k, hidx):
        o = _dot(scores.astype(BF16), v.astype(BF16))
        st = st_ref[hidx]
        o = o + _nt((q * e_q).astype(BF16), st.astype(BF16))
        kd = (k * e_k).astype(BF16)
        total = jnp.where(fwd, e_q[c - 1:c], e_q[0:1])
        st_ref[hidx] = st * total + _dot(v.T.astype(BF16), kd)
        return o

    def hier_scores(q, k, e):
        scores = jnp.where(row == col, _nt(q.astype(BF16), k.astype(BF16)), 0.0)
        for lv in range(REC_LEVELS):
            h = 1 << lv
            el = e[lv * c:(lv + 1) * c]
            second = (row % (2 * h)) >= h
            is_q = second == fwd
            qt = jnp.where(is_q, q * el, 0.0).astype(BF16)
            kt = jnp.where(is_q, 0.0, k * el).astype(BF16)
            same = (row // (2 * h)) == (col // (2 * h))
            scores = scores + jnp.where(same, _nt(qt, kt), 0.0)
        return scores

    lbl = lbl_ref[...]
    mx = jnp.max(lbl, axis=0, keepdims=True)
    ex = jnp.exp(lbl - mx)
    sm = ex / jnp.sum(ex, axis=0, keepdims=True)
    lb_all = jnp.zeros_like(sm[0])
    for i in range(1, layer_o + 1):
        lb_all = lb_all + sm[i]
    lb = jnp.where(fwd, lb_all[0:1], lb_all[1:2])
    z = jnp.where(fwd, zf_ref[...], zb_ref[...])
    log_sig = jnp.minimum(z, 0.0) - jnp.log1p(jnp.exp(-jnp.abs(z)))
    a_ = jnp.log1p(-lb) + log_sig
    b_ = jnp.log(lb)
    big = jnp.maximum(a_, b_)
    logf = big + jnp.log1p(jnp.exp(-jnp.abs(a_ - b_)))
    kh = (1.0 - lb) * jax.nn.sigmoid(-z)
    l1 = logf.astype(BF16)
    r1 = logf - l1.astype(F32)
    l2 = r1.astype(BF16)
    l3 = (r1 - l2.astype(F32)).astype(BF16)
    qh, vh = q_ref[...], v_ref[...]
    outs = []
    for h in range(REC_HEADS):
        sl = slice(h * HEAD_DIM, (h + 1) * HEAD_DIM)
        e = jnp.exp(_dot(amat, l1[:, sl]) + _dot(amat, l2[:, sl]) + _dot(amat, l3[:, sl]))
        q, k = qh[:, sl], kh[:, sl]
        outs.append(mix_head(q, k, vh[:, sl], hier_scores(q, k, e),
                             e[REC_LEVELS * c:(REC_LEVELS + 1) * c], e[(REC_LEVELS + 1) * c:], h))
    rq, rk = rq_ref[...], rk_ref[...]
    rq_rot = _rope(rq, cos_ref[...], sin_ref[...], HEAD_DIM // 2)
    rk_rot = _rope(rk, cos_ref[...], sin_ref[...], HEAD_DIM // 2)
    rq = jnp.where(is_lat, rq_rot, rq)
    rk = jnp.where(is_lat, rk_rot, rk) * (HEAD_DIM ** -0.5)
    rv = rv_ref[...]
    gam = jnp.exp(jnp.where(fwd, rdec_ref[0:1], rdec_ref[1:2]))
    lag = jnp.where(fwd, row - col, col - row)
    lag_f = jnp.maximum(lag, 0).astype(F32)
    steps_q = jnp.where(fwd, row + 1, c - row).astype(F32)
    steps_k = jnp.where(fwd, c - 1 - row, row).astype(F32)
    for h in range(REC_HEADS):
        sl = slice(h * HEAD_DIM, (h + 1) * HEAD_DIM)
        g = gam[:, sl]
        q, k = rq[:, sl], rk[:, sl]
        scores = jnp.where(lag >= 0, _nt(q.astype(BF16), k.astype(BF16)) * jnp.exp(-g * lag_f), 0.0)
        outs.append(mix_head(q, k, rv[:, sl], scores, jnp.exp(-g * steps_q), jnp.exp(-g * steps_k),
                             REC_HEADS + h))
    o_ref[...] = jnp.concatenate(outs, axis=1)


def recurrences(cfg, layer_o, proj, amat, lb_logits, ret_decay_lanes, rope_cs, s0):
    c = REC_CHUNK
    n_chunks = cfg.t // c
    n_ctx_chunks = cfg.tc // c
    cpc, cpl = cfg.ctx_n // c, cfg.lat_n // c
    n_seq = cfg.ctx_b + cfg.lat_b

    def chunk_of(d, s):
        return jnp.where(d == 0, s, n_chunks - 1 - s)

    def seq_of(d, s):
        ch = chunk_of(d, s)
        return jnp.where(ch >= n_ctx_chunks, cfg.ctx_b + (ch - n_ctx_chunks) // cpl, ch // cpc)

    def lat_of(d, s):
        return jnp.maximum(seq_of(d, s) - cfg.ctx_b, 0)

    def lat_pos(d, s):
        ch = chunk_of(d, s)
        return jnp.where(ch >= n_ctx_chunks, (ch - n_ctx_chunks) % cpl, 0)

    col = lambda k: pl.BlockSpec((c, 512), lambda d, s: (chunk_of(d, s), k))
    nrow = (REC_LEVELS + 2) * c
    cos_t, sin_t = rope_cs
    return pl.pallas_call(
        functools.partial(_rec_kernel, cfg, layer_o),
        grid=(2, n_chunks),
        in_specs=[col(0), col(1), col(2), col(3), col(5), col(6), col(7),
                  pl.BlockSpec((None, nrow, c), lambda d, s: (d, 0, 0)),
                  pl.BlockSpec((N_ODD, 2, 512), lambda d, s: (0, 0, 0)),
                  pl.BlockSpec((None, 2, 512), lambda d, s: (0, 0, 0)),
                  pl.BlockSpec((c, 512), lambda d, s: (lat_pos(d, s), 0)),
                  pl.BlockSpec((c, 512), lambda d, s: (lat_pos(d, s), 0)),
                  pl.BlockSpec((None, None, 2 * REC_HEADS, HEAD_DIM, HEAD_DIM),
                               lambda d, s: (d, lat_of(d, s), 0, 0, 0))],
        out_specs=[pl.BlockSpec((None, c, D_MODEL), lambda d, s: (d, chunk_of(d, s), 0)),
                   pl.BlockSpec((None, None, 2 * REC_HEADS, HEAD_DIM, HEAD_DIM),
                                lambda d, s: (d, seq_of(d, s), 0, 0, 0))],
        out_shape=[jax.ShapeDtypeStruct((2, cfg.t, D_MODEL), F32),
                   jax.ShapeDtypeStruct((2, n_seq, 2 * REC_HEADS, HEAD_DIM, HEAD_DIM), F32)],
        compiler_params=_cp(("parallel", "arbitrary")),
        name="recurrences",
    )(proj, proj, proj, proj, proj, proj, proj, amat, lb_logits, ret_decay_lanes, cos_t, sin_t, s0)


def _odd_out_kernel(x_ref, of_ref, ob_ref, cg_ref, rg_ref, hw_ref, rw_ref, wo_ref, g_ref, o_ref):
    o = of_ref[...] + ob_ref[...]
    hw, rw = hw_ref[...], rw_ref[...]
    parts = []
    for h in range(REC_HEADS):
        sl = slice(h * HEAD_DIM, (h + 1) * HEAD_DIM)
        oh = o[:, sl]
        parts.append(oh * lax.rsqrt(jnp.mean(oh * oh, -1, keepdims=True) + EPS) * hw[:, sl])
    oc = jnp.concatenate(parts, axis=1) * jax.nn.sigmoid(cg_ref[...])
    parts = []
    for h in range(REC_HEADS):
        sl = slice(h * HEAD_DIM, (h + 1) * HEAD_DIM)
        oh = o[:, 512 + h * HEAD_DIM:512 + (h + 1) * HEAD_DIM]
        oh = oh - jnp.mean(oh, -1, keepdims=True)
        parts.append(oh * lax.rsqrt(jnp.mean(oh * oh, -1, keepdims=True) + EPS) * rw[:, sl])
    rg = rg_ref[...]
    orr = jnp.concatenate(parts, axis=1) * (rg * jax.nn.sigmoid(rg))
    cat = jnp.concatenate([oc, orr], axis=1).astype(BF16)
    o_ref[...] = x_ref[...] + g_ref[...] * _dot(cat, wo_ref[...])


def odd_out(cfg, x, o2, proj, hgrn_norm_w, ret_norm_w, wo_bf16, mod5, layer):
    tile = lambda w, k=0: pl.BlockSpec((ROW_TILE, w), lambda i: (i, k))
    return pl.pallas_call(
        _odd_out_kernel,
        grid=(cfg.t // ROW_TILE,),
        in_specs=[tile(D_MODEL),
                  pl.BlockSpec((None, ROW_TILE, D_MODEL), lambda i: (0, i, 0)),
                  pl.BlockSpec((None, ROW_TILE, D_MODEL), lambda i: (1, i, 0)),
                  tile(512, 4), tile(512, 8),
                  pl.BlockSpec((1, 512), lambda i: (0, 0)), pl.BlockSpec((1, 512), lambda i: (0, 0)),
                  pl.BlockSpec((D_MODEL, D_MODEL), lambda i: (0, 0)),
                  _mod_spec(cfg, layer, 2)],
        out_specs=tile(D_MODEL),
        out_shape=jax.ShapeDtypeStruct((cfg.t, D_MODEL), F32),
        compiler_params=_cp(("parallel",)),
        name="odd_out",
    )(x, o2, o2, proj, proj, hgrn_norm_w.reshape(1, 512), ret_norm_w.reshape(1, 512), wo_bf16, mod5)


def _route_kernel(cfg, x_ref, nw_ref, sh_ref, sc_ref, wr_ref, h_ref, pos_ref, aff_ref, cj_ref):
    gs = cfg.gs
    g = pl.program_id(0)
    h = (_rms(x_ref[...], nw_ref[...]) * (1.0 + sc_ref[...]) + sh_ref[...]).astype(BF16)
    h_ref[...] = h
    logits = _nt(wr_ref[...], h)
    mx = jnp.max(logits, axis=0, keepdims=True)
    ex = jnp.exp(logits - mx)
    aff = ex / jnp.sum(ex, axis=0, keepdims=True)
    aff_ref[...] = aff

    def seg_sum(x, seg):
        parts = []
        for r in range(gs // seg):
            s = jnp.sum(x[:, r * seg:(r + 1) * seg], axis=1, keepdims=True)
            parts.append(jnp.broadcast_to(s, (N_EXPERTS, seg)))
        return parts[0] if len(parts) == 1 else jnp.concatenate(parts, axis=1)

    tri = (lax.broadcasted_iota(I32, (LANES, LANES), 0) < lax.broadcasted_iota(I32, (LANES, LANES), 1)).astype(BF16)
    lane = lax.broadcasted_iota(I32, (N_EXPERTS, LANES), 1)

    def prefix(mask, seg):
        parts = []
        carry = jnp.zeros((N_EXPERTS, 1), F32)
        starts = jnp.zeros((N_EXPERTS, LANES), F32)
        for j in range(gs // LANES):
            if (j * LANES) % seg == 0:
                carry = jnp.zeros((N_EXPERTS, 1), F32)
            m = mask[:, j * LANES:(j + 1) * LANES]
            starts = starts + jnp.where(lane == j, carry, 0.0)
            parts.append(_dot(m.astype(BF16), tri) + carry)
            carry = carry + jnp.sum(m, axis=1, keepdims=True)
        starts = starts + jnp.where(lane == gs // LANES, carry, 0.0)
        return jnp.concatenate(parts, axis=1), starts

    def select(seg, cap):
        def at_least_cap(mid):
            return seg_sum((aff >= mid).astype(F32), seg) >= cap

        def halve_bits(_, lohi):
            lo, hi = lohi
            mid = lo + ((hi - lo) >> 1)
            ok = at_least_cap(pltpu.bitcast(mid, F32))
            return jnp.where(ok, mid, lo), jnp.where(ok, hi, mid)

        def halve_value(_, lohi):
            lo, hi = lohi
            mid = 0.5 * (lo + hi)
            ok = at_least_cap(mid)
            return jnp.where(ok, mid, lo), jnp.where(ok, hi, mid)

        lo0 = jnp.zeros((N_EXPERTS, gs), I32)
        hi0 = jnp.full((N_EXPERTS, gs), F32_INF_BITS, I32)
        lo_b, hi_b = lax.fori_loop(0, 31, halve_bits, (lo0, hi0))
        lo_f = jnp.where(lo_b < F32_MIN_NORMAL_BITS, 0.0, pltpu.bitcast(lo_b, F32))
        lo_f, hi_f = lax.fori_loop(0, 30, halve_value, (lo_f, pltpu.bitcast(hi_b, F32)))
        above = aff >= hi_f
        tied = jnp.logical_and(aff >= lo_f, aff < hi_f)
        need = cap - seg_sum(above.astype(F32), seg)
        before, _ = prefix(tied.astype(F32), seg)
        return jnp.logical_or(above, jnp.logical_and(tied, before < need))

    def finish(sel):
        self_f = sel.astype(F32)
        slot, starts = prefix(self_f, gs)
        pos_ref[...] = jnp.where(sel, slot.astype(I32), -1)
        cj_ref[...] = starts.astype(I32)

    @pl.when(g == 0)
    def _():
        finish(select(cfg.ctx_n, CAPACITY_FACTOR * cfg.ctx_n // N_EXPERTS))

    @pl.when(g > 0)
    def _():
        finish(select(cfg.lat_n, CAPACITY_FACTOR * cfg.lat_n // N_EXPERTS))


def moe_route(cfg, x, nw, mod5, layer, wr_t_bf16):
    gs = cfg.gs
    mod = lambda part: pl.BlockSpec((None, None, None, 1, D_MODEL), lambda g: (layer, g, part, 0, 0))
    return pl.pallas_call(
        functools.partial(_route_kernel, cfg),
        grid=(cfg.groups,),
        in_specs=[pl.BlockSpec((gs, D_MODEL), lambda g: (g, 0)),
                  pl.BlockSpec((1, D_MODEL), lambda g: (0, 0)),
                  mod(3), mod(4),
                  pl.BlockSpec((N_EXPERTS, D_MODEL), lambda g: (0, 0))],
        out_specs=[pl.BlockSpec((gs, D_MODEL), lambda g: (g, 0)),
                   pl.BlockSpec((None, N_EXPERTS, gs), lambda g: (g, 0, 0)),
                   pl.BlockSpec((None, N_EXPERTS, gs), lambda g: (g, 0, 0)),
                   pl.BlockSpec((None, N_EXPERTS, LANES), lambda g: (g, 0, 0))],
        out_shape=[jax.ShapeDtypeStruct((cfg.t, D_MODEL), BF16),
                   jax.ShapeDtypeStruct((cfg.groups, N_EXPERTS, gs), I32),
                   jax.ShapeDtypeStruct((cfg.groups, N_EXPERTS, gs), F32),
                   jax.ShapeDtypeStruct((cfg.groups, N_EXPERTS, LANES), I32)],
        compiler_params=_cp(("parallel",)),
        name="moe_route",
    )(x, nw.reshape(1, D_MODEL), mod5, mod5, wr_t_bf16)


def _slot_block(cfg):
    return min(LANES, cfg.slots)


def _gather_kernel(cfg, cj_ref, h_ref, pos_ref, aff_ref, x_ref, gate_ref, acc_ref, gacc_ref):
    g, e = pl.program_id(0), pl.program_id(1)
    sb = _slot_block(cfg)
    acc_ref[...] = jnp.zeros_like(acc_ref)
    gacc_ref[...] = jnp.zeros_like(gacc_ref)
    base = (g * N_EXPERTS + e) * LANES
    sub = lax.broadcasted_iota(I32, (sb, LANES), 0)
    for tb in range(cfg.gs // LANES):
        s0, s1 = cj_ref[base + tb], cj_ref[base + tb + 1]
        for jb in range(cfg.slots // sb):
            @pl.when((s1 > s0) & (s0 < (jb + 1) * sb) & (s1 > jb * sb))
            def _():
                p = pos_ref[:, tb * LANES:(tb + 1) * LANES]
                hit = jnp.broadcast_to(p, (sb, LANES)) == (sub + jb * sb)
                acc_ref[jb * sb:(jb + 1) * sb, :] += _dot(hit.astype(BF16), h_ref[tb * LANES:(tb + 1) * LANES, :])
                a = aff_ref[:, tb * LANES:(tb + 1) * LANES]
                gacc_ref[jb * sb:(jb + 1) * sb, :] += jnp.sum(jnp.where(hit, a, 0.0), axis=1, keepdims=True)
    x_ref[...] = acc_ref[...].astype(BF16)
    gate_ref[...] = gacc_ref[...]


def moe_gather(cfg, cj, h, pos, aff):
    gs, slots = cfg.gs, cfg.slots
    pos4 = pos.reshape(cfg.groups, N_EXPERTS, 1, gs)
    aff4 = aff.reshape(cfg.groups, N_EXPERTS, 1, gs)
    grid_spec = pltpu.PrefetchScalarGridSpec(
        num_scalar_prefetch=1,
        grid=(cfg.groups, N_EXPERTS),
        in_specs=[pl.BlockSpec((gs, D_MODEL), lambda g, e, cj: (g, 0)),
                  pl.BlockSpec((None, None, 1, gs), lambda g, e, cj: (g, e, 0, 0)),
                  pl.BlockSpec((None, None, 1, gs), lambda g, e, cj: (g, e, 0, 0))],
        out_specs=[pl.BlockSpec((None, slots, D_MODEL), lambda g, e, cj: (e, g, 0)),
                   pl.BlockSpec((None, slots, 1), lambda g, e, cj: (e, g, 0))],
        scratch_shapes=[pltpu.VMEM((slots, D_MODEL), F32), pltpu.VMEM((slots, 1), F32)])
    return pl.pallas_call(
        functools.partial(_gather_kernel, cfg),
        grid_spec=grid_spec,
        out_shape=[jax.ShapeDtypeStruct((N_EXPERTS, cfg.groups * slots, D_MODEL), BF16),
                   jax.ShapeDtypeStruct((N_EXPERTS, cfg.groups * slots, 1), F32)],
        compiler_params=_cp(("parallel", "arbitrary")),
        name="moe_gather",
    )(cj.reshape(-1), h, pos4, aff4)


def _ffn_kernel(rows, x_ref, gate_ref, w1_ref, w3_ref, w2_ref, y_ref, acc_ref, w1b, w3b, w2b):
    c = pl.program_id(1)
    w1b[...] = w1_ref[...].astype(BF16)
    w3b[...] = w3_ref[...].astype(BF16)
    w2b[...] = w2_ref[...].astype(BF16)
    rt = min(ROW_TILE, rows)

    def body(i, _):
        r = pl.multiple_of(i * rt, rt)
        x = x_ref[pl.ds(r, rt), :]
        a = _dot(x, w1b[...])
        b = _dot(x, w3b[...])
        hid = (a * jax.nn.sigmoid(a) * b).astype(BF16)
        y = _dot(hid, w2b[...])

        @pl.when(c == 0)
        def _():
            acc_ref[pl.ds(r, rt), :] = y

        @pl.when(c > 0)
        def _():
            acc_ref[pl.ds(r, rt), :] += y

        return 0

    lax.fori_loop(0, rows // rt, body, 0)

    @pl.when(c == pl.num_programs(1) - 1)
    def _():
        y_ref[...] = (acc_ref[...] * gate_ref[...]).astype(BF16)


def moe_ffn(cfg, layer, xg, gate, w1, w3, w2):
    rows = cfg.groups * cfg.slots
    nff = EXPERT_FF // FF_TILE
    return pl.pallas_call(
        functools.partial(_ffn_kernel, rows),
        grid=(N_EXPERTS, nff),
        in_specs=[pl.BlockSpec((None, rows, D_MODEL), lambda e, c: (e, 0, 0)),
                  pl.BlockSpec((None, rows, 1), lambda e, c: (e, 0, 0)),
                  pl.BlockSpec((None, None, D_MODEL, FF_TILE), lambda e, c: (layer, e, 0, c)),
                  pl.BlockSpec((None, None, D_MODEL, FF_TILE), lambda e, c: (layer, e, 0, c)),
                  pl.BlockSpec((None, None, FF_TILE, D_MODEL), lambda e, c: (layer, e, c, 0))],
        out_specs=pl.BlockSpec((None, rows, D_MODEL), lambda e, c: (e, 0, 0)),
        out_shape=jax.ShapeDtypeStruct((N_EXPERTS, rows, D_MODEL), BF16),
        scratch_shapes=[pltpu.VMEM((rows, D_MODEL), F32),
                        pltpu.VMEM((D_MODEL, FF_TILE), BF16), pltpu.VMEM((D_MODEL, FF_TILE), BF16),
                        pltpu.VMEM((FF_TILE, D_MODEL), BF16)],
        compiler_params=_cp(("parallel", "arbitrary")),
        name="moe_ffn",
    )(xg, gate, w1, w3, w2)


def _scatter_kernel(cfg, final, cj_ref, x_ref, pos_ref, y_ref, g_ref, fw_ref, o_ref, acc_ref):
    g, tb = pl.program_id(0), pl.program_id(1)
    sb = _slot_block(cfg)
    acc_ref[...] = jnp.zeros_like(acc_ref)
    lane = lax.broadcasted_iota(I32, (LANES, sb), 1)
    for e in range(N_EXPERTS):
        base = (g * N_EXPERTS + e) * LANES
        s0, s1 = cj_ref[base + tb], cj_ref[base + tb + 1]
        for jb in range(cfg.slots // sb):
            @pl.when((s1 > s0) & (s0 < (jb + 1) * sb) & (s1 > jb * sb))
            def _():
                hit = jnp.broadcast_to(pos_ref[e], (LANES, sb)) == (lane + jb * sb)
                acc_ref[...] += _dot(hit.astype(BF16), y_ref[e, jb * sb:(jb + 1) * sb, :])
    out = x_ref[...] + g_ref[...] * acc_ref[...]
    if final:
        out = _rms(out, fw_ref[...])
    o_ref[...] = out


def moe_scatter(cfg, layer, cj, x, pos, y, mod5, final_w, final):
    gs, slots = cfg.gs, cfg.slots
    nt = gs // LANES
    pos_col = pos.reshape(cfg.groups, N_EXPERTS, gs, 1)
    grid_spec = pltpu.PrefetchScalarGridSpec(
        num_scalar_prefetch=1,
        grid=(cfg.groups, nt),
        in_specs=[pl.BlockSpec((LANES, D_MODEL), lambda g, tb, cj: (g * nt + tb, 0)),
                  pl.BlockSpec((None, N_EXPERTS, LANES, 1), lambda g, tb, cj: (g, 0, tb, 0)),
                  pl.BlockSpec((N_EXPERTS, slots, D_MODEL), lambda g, tb, cj: (0, g, 0)),
                  pl.BlockSpec((None, None, None, 1, D_MODEL), lambda g, tb, cj: (layer, g, 5, 0, 0)),
                  pl.BlockSpec((1, D_MODEL), lambda g, tb, cj: (0, 0))],
        out_specs=pl.BlockSpec((LANES, D_MODEL), lambda g, tb, cj: (g * nt + tb, 0)),
        scratch_shapes=[pltpu.VMEM((LANES, D_MODEL), F32)])
    return pl.pallas_call(
        functools.partial(_scatter_kernel, cfg, final),
        grid_spec=grid_spec,
        out_shape=jax.ShapeDtypeStruct((cfg.t, D_MODEL), F32),
        compiler_params=_cp(("parallel", "arbitrary")),
        name="moe_scatter",
    )(cj.reshape(-1), x, pos_col, y, mod5, final_w.reshape(1, D_MODEL))


def _forward(cfg, x_prompt, x_sample, state_s5, cache_k, cache_v, state_hgrn, state_ret, c, c_ctx, p):
    assert cfg.tc == cfg.lat_n and cfg.ctx_n % ROW_TILE == 0 and cfg.lat_n % (GRID_W * 2) == 0
    x = jnp.concatenate([x_prompt.reshape(cfg.tc, D_MODEL), x_sample.reshape(-1, D_MODEL)], axis=0)
    conds = jnp.zeros((8, D_MODEL), F32).at[0].set(c_ctx).at[1:1 + cfg.lat_b].set(c)
    mod = adaln_table(conds, p['mod_w'], p['mod_b'])
    mod5 = mod[:, :cfg.groups].reshape(DEPTH, cfg.groups, 6, 1, D_MODEL)
    rope_att = _rope_tables(cfg.lat_n, ATT_HEAD_DIM, 512)
    rope_ret = _rope_tables(cfg.lat_n, HEAD_DIM, 512)
    amat = jnp.asarray(_rec_tables(), BF16)
    s5_fin, k_out, v_out, hgrn_fin, ret_fin = [], [], [], [], []
    for layer in range(DEPTH):
        if layer % 2 == 0:
            e = layer // 2
            proj = norm_mod_matmul(cfg, x, p['norm1_w'][layer], mod5, layer, p['ab_w_in'][e].astype(BF16))
            wcat, mmat, gmat, a16 = _s5_tables(p['s5_lambda_re'][e], p['s5_lambda_im'][e], p['s5_log_dt'][e],
                                               p['s5_b_re'][e], p['s5_b_im'][e], p['s5_c_re'][e], p['s5_c_im'][e])
            h0 = state_s5[:, e].reshape(cfg.lat_b, 4, S5_PAIRS, 2 * S5_STATE).transpose(2, 0, 1, 3)
            yd, fin = s5_mix(cfg, proj, wcat, mmat, gmat, a16, h0)
            s5_fin.append(fin.transpose(2, 1, 0, 3).reshape(cfg.ctx_b, 2, 2, S5_GROUPS, S5_STATE))
            ck = cache_k[:, e].reshape(cfg.lat_b, cfg.past, LANES)
            cv = cache_v[:, e].reshape(cfg.lat_b, cfg.past, LANES)
            yatt = attention(cfg, proj, p['attn_sink'][e], ck, cv, rope_att)
            k_out.append(proj[:cfg.tc, 1024:1152].reshape(cfg.ctx_b, cfg.ctx_n, ATT_KV_HEADS, ATT_HEAD_DIM))
            v_out.append(proj[:cfg.tc, 1152:1280].reshape(cfg.ctx_b, cfg.ctx_n, ATT_KV_HEADS, ATT_HEAD_DIM))
            x = even_out(cfg, x, proj, yd, yatt, p['s5_d'][e], p['s5_glu_w'][e].astype(BF16),
                         p['ab_w_out'][e].astype(BF16), mod5, layer)
        else:
            o = layer // 2
            proj = norm_mod_matmul(cfg, x, p['norm1_w'][layer], mod5, layer, p['cd_w_in'][o].astype(BF16))
            s0 = jnp.concatenate([state_hgrn[:, o], state_ret[:, o]], axis=2).transpose(1, 0, 2, 4, 3)
            rdec = jnp.repeat(p['ret_decay'][o], HEAD_DIM, axis=-1)
            o2, st = recurrences(cfg, o, proj, amat, p['hgrn_lb_logits'], rdec[None], rope_ret, s0)
            st = st[:, :cfg.ctx_b].transpose(1, 0, 2, 4, 3)
            hgrn_fin.append(st[:, :, :REC_HEADS])
            ret_fin.append(st[:, :, REC_HEADS:])
            x = odd_out(cfg, x, o2, proj, p['hgrn_norm_w'][o], p['ret_norm_w'][o],
                        p['cd_w_out'][o].astype(BF16), mod5, layer)
        h2, pos, aff, cj = moe_route(cfg, x, p['norm2_w'][layer], mod5, layer,
                                     p['router_w'][layer].T.astype(BF16))
        xg, gate = moe_gather(cfg, cj, h2, pos, aff)
        yg = moe_ffn(cfg, layer, xg, gate, p['moe_w1'], p['moe_w3'], p['moe_w2'])
        x = moe_scatter(cfg, layer, cj, x, pos, yg, mod5, p['final_norm_w'], layer == DEPTH - 1)
    y_prompt = x[:cfg.tc].reshape(cfg.ctx_b, cfg.ctx_n, D_MODEL)
    y_sample = x[cfg.tc:].reshape(cfg.lat_b, cfg.lat_n, D_MODEL)
    return (y_prompt, y_sample, jnp.stack(s5_fin, axis=1), jnp.stack(k_out, axis=1), jnp.stack(v_out, axis=1),
            jnp.stack(hgrn_fin, axis=1), jnp.stack(ret_fin, axis=1))


def kernel(x_prompt, x_sample, state_s5, cache_k, cache_v, state_hgrn, state_ret, c, c_ctx, mod_w, mod_b, norm1_w, norm2_w, final_norm_w, ab_w_in, ab_w_out, s5_lambda_re, s5_lambda_im, s5_log_dt, s5_b_re, s5_b_im, s5_c_re, s5_c_im, s5_d, s5_glu_w, attn_sink, cd_w_in, cd_w_out, hgrn_lb_logits, hgrn_norm_w, ret_decay, ret_norm_w, router_w, moe_w1, moe_w3, moe_w2):
    cfg = Cfg(ctx_b=x_prompt.shape[0], ctx_n=x_prompt.shape[1], lat_b=x_sample.shape[0],
              lat_n=x_sample.shape[1], past=cache_k.shape[2])
    p = dict(mod_w=mod_w, mod_b=mod_b, norm1_w=norm1_w, norm2_w=norm2_w, final_norm_w=final_norm_w,
             ab_w_in=ab_w_in, ab_w_out=ab_w_out, s5_lambda_re=s5_lambda_re, s5_lambda_im=s5_lambda_im,
             s5_log_dt=s5_log_dt, s5_b_re=s5_b_re, s5_b_im=s5_b_im, s5_c_re=s5_c_re, s5_c_im=s5_c_im,
             s5_d=s5_d, s5_glu_w=s5_glu_w, attn_sink=attn_sink, cd_w_in=cd_w_in, cd_w_out=cd_w_out,
             hgrn_lb_logits=hgrn_lb_logits, hgrn_norm_w=hgrn_norm_w, ret_decay=ret_decay,
             ret_norm_w=ret_norm_w, router_w=router_w, moe_w1=moe_w1, moe_w3=moe_w3, moe_w2=moe_w2)
    return _forward(cfg, x_prompt, x_sample, state_s5, cache_k, cache_v, state_hgrn, state_ret, c, c_ctx, p)
```

```python
import functools
import math
from typing import NamedTuple

import numpy as np
import jax
import jax.numpy as jnp
from jax import lax
from jax.experimental import pallas as pl
from jax.experimental.pallas import tpu as pltpu

F32 = jnp.float32
BF16 = jnp.bfloat16
I32 = jnp.int32

D_MODEL = 1024
DEPTH = 4
N_EVEN = 2
N_ODD = 2
EPS = 1e-6
GRID_W = 64
S5_WIDTH = 512
S5_GROUP = 16
S5_GROUPS = 32
S5_STATE = 64
S5_CHUNK = 16
S5_PAIRS = S5_GROUPS // 2
S5_QUAD = 4
ATT_HEAD_DIM = 64
ATT_HEADS = 8
ATT_KV_HEADS = 2
ATT_GROUP = 4
WINDOW = 128
ROPE_BASE = 10000.0
AB_IN = 1280
HEAD_DIM = 128
REC_HEADS = 4
CD_IN = 4608
N_EXPERTS = 16
EXPERT_FF = 1536
CAPACITY_FACTOR = 2

LANES = 128
ROW_TILE = 256
FF_TILE = 512
F32_INF_BITS = 0x7F800000
F32_MIN_NORMAL_BITS = 0x00800000
VMEM_LIMIT = 56 * 1024 * 1024


class Cfg(NamedTuple):
    ctx_b: int
    ctx_n: int
    lat_b: int
    lat_n: int
    past: int

    @property
    def tc(self):
        return self.ctx_b * self.ctx_n

    @property
    def t(self):
        return self.tc + self.lat_b * self.lat_n

    @property
    def gs(self):
        return self.lat_n

    @property
    def groups(self):
        return 1 + self.lat_b

    @property
    def slots(self):
        return CAPACITY_FACTOR * self.gs // N_EXPERTS


def _cp(sem, vmem=VMEM_LIMIT):
    return pltpu.CompilerParams(dimension_semantics=sem, vmem_limit_bytes=vmem)


def _nt(a, b):
    return lax.dot_general(a, b, (((1,), (1,)), ((), ())), preferred_element_type=F32)


def _dot(a, b, precision=None):
    return jnp.dot(a, b, preferred_element_type=F32, precision=precision)


_HI = lax.Precision.HIGHEST


def _adaln_kernel(c_ref, w_ref, b_ref, o_ref):
    c = c_ref[...]
    s = c * jax.nn.sigmoid(c)
    o_ref[0] = _dot(s.astype(BF16), w_ref[0].astype(BF16)) + b_ref[0]


def adaln_table(conds, mod_w, mod_b):
    n6 = 6 * D_MODEL
    tn = 1536
    return pl.pallas_call(
        _adaln_kernel,
        grid=(DEPTH, n6 // tn),
        in_specs=[pl.BlockSpec((8, D_MODEL), lambda l, j: (0, 0)),
                  pl.BlockSpec((1, D_MODEL, tn), lambda l, j: (l, 0, j)),
                  pl.BlockSpec((1, 1, tn), lambda l, j: (l, 0, j))],
        out_specs=pl.BlockSpec((1, 8, tn), lambda l, j: (l, 0, j)),
        out_shape=jax.ShapeDtypeStruct((DEPTH, 8, n6), F32),
        compiler_params=_cp(("parallel", "parallel")),
        name="adaln_table",
    )(conds, mod_w, mod_b.reshape(DEPTH, 1, n6))


def _mod_spec(cfg, layer, part):
    tiles_per_group = cfg.gs // ROW_TILE
    return pl.BlockSpec((None, None, None, 1, D_MODEL),
                        lambda i: (layer, i // tiles_per_group, part, 0, 0))


def _rms(x, w):
    return x * lax.rsqrt(jnp.mean(x * x, axis=-1, keepdims=True) + EPS) * w


def _nmm_kernel(x_ref, nw_ref, sh_ref, sc_ref, w_ref, o_ref):
    h = _rms(x_ref[...], nw_ref[...]) * (1.0 + sc_ref[...]) + sh_ref[...]
    o_ref[...] = _dot(h.astype(BF16), w_ref[...])


def norm_mod_matmul(cfg, x, nw, mod5, layer, w_bf16):
    n_out = w_bf16.shape[1]
    return pl.pallas_call(
        _nmm_kernel,
        grid=(cfg.t // ROW_TILE,),
        in_specs=[pl.BlockSpec((ROW_TILE, D_MODEL), lambda i: (i, 0)),
                  pl.BlockSpec((1, D_MODEL), lambda i: (0, 0)),
                  _mod_spec(cfg, layer, 0), _mod_spec(cfg, layer, 1),
                  pl.BlockSpec((D_MODEL, n_out), lambda i: (0, 0))],
        out_specs=pl.BlockSpec((ROW_TILE, n_out), lambda i: (i, 0)),
        out_shape=jax.ShapeDtypeStruct((cfg.t, n_out), F32),
        compiler_params=_cp(("parallel",)),
        name="norm_mod_matmul",
    )(x, nw.reshape(1, D_MODEL), mod5, mod5, w_bf16)


def _s5_tables(lam_re, lam_im, log_dt, b_re, b_im, c_re, c_im):
    L = S5_CHUNK
    lr, li = lam_re.astype(F32), lam_im.astype(F32)
    dt = jnp.exp(log_dt.astype(F32))[..., None]
    mag = jnp.exp(lr * dt)
    ar, ai = mag * jnp.cos(li * dt), mag * jnp.sin(li * dt)
    den = lr * lr + li * li
    cr = ((ar - 1.0) * lr + ai * li) / den
    ci = (ai * lr - (ar - 1.0) * li) / den
    br, bi = b_re.astype(F32), b_im.astype(F32)
    bbr = cr[..., None] * br - ci[..., None] * bi
    bbi = cr[..., None] * bi + ci[..., None] * br

    def pw_step(carry, _):
        pr, pi = carry
        return (pr * ar - pi * ai, pr * ai + pi * ar), (pr, pi)

    (_, _), (pr, pi) = lax.scan(pw_step, (jnp.ones_like(ar), jnp.zeros_like(ar)), None, length=L + 1)
    er = pr[:L, ..., None] * bbr[None] - pi[:L, ..., None] * bbi[None]
    ei = pr[:L, ..., None] * bbi[None] + pi[:L, ..., None] * bbr[None]
    ccr, cci = c_re.astype(F32), c_im.astype(F32)
    kt = (jnp.einsum('dgcp,tdgpe->tdgce', ccr, er, precision=_HI)
          - jnp.einsum('dgcp,tdgpe->tdgce', cci, ei, precision=_HI))
    s = np.arange(L)
    lag_f = s[:, None] - s[None, :]
    m_f = jnp.where((lag_f >= 0)[:, :, None, None, None], kt[np.clip(lag_f, 0, L - 1), 0], 0.0)
    m_b = jnp.where((lag_f <= 0)[:, :, None, None, None], kt[np.clip(-lag_f, 0, L - 1), 1], 0.0)
    m = (m_f + m_b).transpose(2, 1, 4, 0, 3)
    m = m.reshape(S5_GROUPS, L * S5_GROUP, L * S5_GROUP)
    wf_r, wf_i = er[::-1, 0], ei[::-1, 0]
    wb_r, wb_i = er[:, 1], ei[:, 1]

    def w_mat(w):
        return w.transpose(1, 0, 3, 2).reshape(S5_GROUPS, L * S5_GROUP, S5_STATE)

    def g_mats(d, p_r, p_i):
        g_r = ccr[d][None] * p_r[:, :, None, :] - cci[d][None] * p_i[:, :, None, :]
        g_i = -ccr[d][None] * p_i[:, :, None, :] - cci[d][None] * p_r[:, :, None, :]
        to = lambda g: g.transpose(1, 3, 0, 2).reshape(S5_GROUPS, S5_STATE, L * S5_GROUP)
        return to(g_r), to(g_i)

    gf_r, gf_i = g_mats(0, pr[1:L + 1, 0], pi[1:L + 1, 0])
    gb_r, gb_i = g_mats(1, pr[L:0:-1, 1], pi[L:0:-1, 1])

    def pair_blockdiag(a):
        g, r, c = a.shape
        a = a.reshape(g // 2, 2, r, c)
        z = jnp.zeros_like(a[:, 0])
        top = jnp.concatenate([a[:, 0], z], axis=2)
        bot = jnp.concatenate([z, a[:, 1]], axis=2)
        return jnp.concatenate([top, bot], axis=1)

    wcat = jnp.concatenate([pair_blockdiag(w_mat(w)) for w in (wf_r, wf_i, wb_r, wb_i)], axis=2)
    mmat = pair_blockdiag(m)
    gmat = jnp.concatenate([pair_blockdiag(g) for g in (gf_r, gf_i, gb_r, gb_i)], axis=1)

    def pair_lanes(a):
        return a.reshape(S5_PAIRS, 2 * S5_STATE)

    a16 = jnp.stack([pair_lanes(pr[L, 0]), pair_lanes(pi[L, 0]),
                     pair_lanes(pr[L, 1]), pair_lanes(pi[L, 1])], axis=1)
    perm = np.arange(2 * L * S5_GROUP).reshape(2, L, S5_GROUP).transpose(1, 0, 2).reshape(-1)
    wcat = wcat[:, perm, :]
    mmat = mmat[:, perm, :][:, :, perm]
    gmat = gmat[:, :, perm]
    return wcat.astype(BF16), mmat.astype(BF16), gmat.astype(BF16), a16


def _s5_pair(cfg, u, w_mat, m_mat, g_mat, a, h0_ref, pp, fin_ref, xs_ref):
    rc = cfg.tc // S5_CHUNK
    lc = cfg.ctx_n // S5_CHUNK
    ll = cfg.lat_n // S5_CHUNK
    r_all = cfg.t // S5_CHUNK
    w = _dot(u, w_mat)
    row =lax.broadcasted_iota(I32, (r_all, LANES), 0)
    is_ctx = row < rc
    pos = jnp.where(is_ctx, row % lc, (row - rc) % ll)
    seg = jnp.where(is_ctx, lc, ll)
    prev_parts = []
    for d in range(2):
        xr = w[:, (2 * d) * LANES:(2 * d + 1) * LANES]
        xi = w[:, (2 * d + 1) * LANES:(2 * d + 2) * LANES]
        ar, ai = a[2 * d:2 * d + 1], a[2 * d + 1:2 * d + 2]
        first = 0 if d == 0 else seg - 1
        h0r = jnp.zeros((r_all, LANES), F32)
        h0i = jnp.zeros((r_all, LANES), F32)
        for b in range(cfg.lat_b):
            sel = (row >= rc + b * ll) & (row < rc + (b + 1) * ll)
            h0r = jnp.where(sel, h0_ref[pp, b, 2 * d:2 * d + 1, :], h0r)
            h0i = jnp.where(sel, h0_ref[pp, b, 2 * d + 1:2 * d + 2, :], h0i)
        at_first = pos == first
        xr = xr + jnp.where(at_first, ar * h0r - ai * h0i, 0.0)
        xi = xi + jnp.where(at_first, ar * h0i + ai * h0r, 0.0)
        pr, pi = ar, ai
        step = 1
        while step < max(lc, ll):
            if d == 0:
                sr, si = pltpu.roll(xr, step, 0), pltpu.roll(xi, step, 0)
                ok = (pos >= step)
            else:
                sr, si = pltpu.roll(xr, r_all - step, 0), pltpu.roll(xi, r_all - step, 0)
                ok = (pos < seg - step)
            xr, xi = (xr + jnp.where(ok, pr * sr - pi * si, 0.0),
                      xi + jnp.where(ok, pr * si + pi * sr, 0.0))
            pr, pi = pr * pr - pi * pi, 2.0 * pr * pi
            step *= 2
        if d == 0:
            nr, ni = pltpu.roll(xr, 1, 0), pltpu.roll(xi, 1, 0)
        else:
            nr, ni = pltpu.roll(xr, r_all - 1, 0), pltpu.roll(xi, r_all - 1, 0)
        prev_parts += [jnp.where(at_first, h0r, nr), jnp.where(at_first, h0i, ni)]
        xs_ref[2 * d] = xr
        xs_ref[2 * d + 1] = xi
    xin = jnp.concatenate(prev_parts, axis=1).astype(BF16)
    for d in range(2):
        start = lc - 1 if d == 0 else 0
        for k in range(2):
            fin_ref[pp, 2 * d + k] = xs_ref[2 * d + k, pl.ds(start, cfg.ctx_b, stride=lc), :]
    return _dot(u, m_mat) + _dot(xin, g_mat)


def _s5_kernel(cfg, u_ref, w_ref, m_ref, g_ref, a_ref, h0_ref, y_ref, fin_ref, uflat_ref, ystage_ref, xs_ref):
    r_all = cfg.t // S5_CHUNK
    sw = 2 * S5_GROUP
    for s in range(S5_CHUNK):
        blk = u_ref[pl.ds(s, r_all, stride=S5_CHUNK), :]
        for pp in range(S5_QUAD):
            uflat_ref[pp, :, s * sw:(s + 1) * sw] = blk[:, pp * sw:(pp + 1) * sw]
    for pp in range(S5_QUAD):
        y = _s5_pair(cfg, uflat_ref[pp].astype(BF16), w_ref[pp], m_ref[pp], g_ref[pp], a_ref[pp],
                     h0_ref, pp, fin_ref, xs_ref)
        for s in range(S5_CHUNK):
            ystage_ref[s, :, pp * sw:(pp + 1) * sw] = y[:, s * sw:(s + 1) * sw]
    for s in range(S5_CHUNK):
        y_ref[pl.ds(s, r_all, stride=S5_CHUNK), :] = ystage_ref[s]


def s5_mix(cfg, proj, wcat, mmat, gmat, a16, h0):
    r_all = cfg.t // S5_CHUNK
    wide = 2 * S5_CHUNK * S5_GROUP
    mat = pl.BlockSpec((S5_QUAD, wide, wide), lambda q: (q, 0, 0))
    return pl.pallas_call(
        functools.partial(_s5_kernel, cfg),
        grid=(S5_PAIRS // S5_QUAD,),
        in_specs=[pl.BlockSpec((cfg.t, LANES), lambda q: (0, q)), mat, mat, mat,
                  pl.BlockSpec((S5_QUAD, 4, LANES), lambda q: (q, 0, 0)),
                  pl.BlockSpec((S5_QUAD, cfg.lat_b, 4, LANES), lambda q: (q, 0, 0, 0))],
        out_specs=[pl.BlockSpec((cfg.t, LANES), lambda q: (0, q)),
                   pl.BlockSpec((S5_QUAD, 4, cfg.ctx_b, LANES), lambda q: (q, 0, 0, 0))],
        out_shape=[jax.ShapeDtypeStruct((cfg.t, S5_WIDTH), F32),
                   jax.ShapeDtypeStruct((S5_PAIRS, 4, cfg.ctx_b, LANES), F32)],
        scratch_shapes=[pltpu.VMEM((S5_QUAD, r_all, wide), F32),
                        pltpu.VMEM((S5_CHUNK, r_all, LANES), F32),
                        pltpu.VMEM((4, r_all, LANES), F32)],
        compiler_params=_cp(("parallel",)),
        name="s5_mix",
    )(proj, wcat, mmat, gmat, a16, h0)


def _rope(x, cos, sin_signed, half):
    w = x.shape[-1]
    lane = lax.broadcasted_iota(I32, x.shape, x.ndim - 1)
    swapped = jnp.where((lane % (2 * half)) < half,
                        pltpu.roll(x, w - half, x.ndim - 1), pltpu.roll(x, half, x.ndim - 1))
    return x * cos + swapped * sin_signed


def _attn_kernel(n_parts, latent, n_q_blocks, sink_ref, q_ref, *refs):
    hd = ATT_HEAD_DIM
    scale = hd ** -0.5
    if latent:
        (kp_ref, kc_ref, kn_ref, vp_ref, vc_ref, vn_ref, ck_ref, cv_ref,
         cq_ref, sq_ref, ckp_ref, skp_ref, ckc_ref, skc_ref, ckn_ref, skn_ref, o_ref) = refs
        i = pl.program_id(1)
        q = _rope(q_ref[...], cq_ref[...], sq_ref[...], hd // 2)
        kparts = [_rope(kp_ref[...], ckp_ref[...], skp_ref[...], hd // 2),
                  _rope(kc_ref[...], ckc_ref[...], skc_ref[...], hd // 2),
                  _rope(kn_ref[...], ckn_ref[...], skn_ref[...], hd // 2)]
        kband = jnp.concatenate(kparts, axis=0)
        vband = jnp.concatenate([vp_ref[...], vc_ref[...], vn_ref[...]], axis=0)
        keys = [kband, ck_ref[...]]
        vals = [vband, cv_ref[...]]
        nrow = ATT_GROUP * LANES
        qpos = lax.broadcasted_iota(I32, (nrow, 3 * LANES), 0) % LANES
        kpos = lax.broadcasted_iota(I32, (nrow, 3 * LANES), 1) - LANES
        kabs = kpos + i * LANES
        band_ok = (jnp.abs(kpos - qpos) <= WINDOW) & (kabs >= 0) & (kabs < n_q_blocks * LANES)
        masks = [band_ok, None]
    else:
        k_ref, v_ref, o_ref = refs
        q = q_ref[...]
        keys = [k_ref[...]]
        vals = [v_ref[...]]
        masks = [None]
    for kv in range(ATT_KV_HEADS):
        qs = jnp.concatenate([q[:, (kv * ATT_GROUP + g) * hd:(kv * ATT_GROUP + g + 1) * hd]
                              for g in range(ATT_GROUP)], axis=0).astype(BF16)
        sink_col = jnp.concatenate([jnp.full((LANES, 1), sink_ref[kv * ATT_GROUP + g], F32)
                                    for g in range(ATT_GROUP)], axis=0)
        logits = []
        mx = sink_col
        for kk, msk in zip(keys, masks):
            s = _nt(qs, kk[:, kv * hd:(kv + 1) * hd].astype(BF16)) * scale
            if msk is not None:
                s = jnp.where(msk, s, -1e30)
            logits.append(s)
            mx = jnp.maximum(mx, jnp.max(s, axis=-1, keepdims=True))
        den = jnp.exp(sink_col - mx)
        acc = None
        for s, vv in zip(logits, vals):
            p = jnp.exp(s - mx)
            den = den + jnp.sum(p, axis=-1, keepdims=True)
            o = _dot(p.astype(BF16), vv[:, kv * hd:(kv + 1) * hd].astype(BF16))
            acc = o if acc is None else acc + o
        out = acc / den
        for g in range(ATT_GROUP):
            h = kv * ATT_GROUP + g
            o_ref[:, h * hd:(h + 1) * hd] = out[g * LANES:(g + 1) * LANES, :]


def attention(cfg, proj, sink, cache_k, cache_v, rope_cs):
    qcol = S5_WIDTH // 512
    kcol = (S5_WIDTH + 512) // LANES
    vcol = kcol + 1
    smem = pl.BlockSpec(memory_space=pltpu.SMEM)
    nqc = cfg.ctx_n // LANES
    y_ctx = pl.pallas_call(
        functools.partial(_attn_kernel, 1, False, nqc),
        grid=(cfg.ctx_b, nqc),
        in_specs=[smem,
                  pl.BlockSpec((LANES, 512), lambda b, i: (b * nqc + i, qcol)),
                  pl.BlockSpec((cfg.ctx_n, LANES), lambda b, i: (b, kcol)),
                  pl.BlockSpec((cfg.ctx_n, LANES), lambda b, i: (b, vcol))],
        out_specs=pl.BlockSpec((LANES, 512), lambda b, i: (b * nqc + i, 0)),
        out_shape=jax.ShapeDtypeStruct((cfg.tc, 512), F32),
        compiler_params=_cp(("parallel", "parallel")),
        name="attn_context",
    )(sink, proj, proj, proj)
    nql = cfg.lat_n // LANES
    off = cfg.tc // LANES
    cos_t, sin_t = rope_cs

    def rb(b, i):
        return off + b * nql + i

    def prev(i):
        return jnp.maximum(i - 1, 0)

    def nxt(i):
        return jnp.minimum(i + 1, nql - 1)

    kspec = lambda f, col: pl.BlockSpec((LANES, LANES), lambda b, i: (rb(b, f(i)), col))
    tspec = lambda f: pl.BlockSpec((LANES, LANES), lambda b, i: (f(i), 0))
    same = lambda i: i
    y_lat = pl.pallas_call(
        functools.partial(_attn_kernel, 2, True, nql),
        grid=(cfg.lat_b, nql),
        in_specs=[smem,
                  pl.BlockSpec((LANES, 512), lambda b, i: (rb(b, i), qcol)),
                  kspec(prev, kcol), kspec(same, kcol), kspec(nxt, kcol),
                  kspec(prev, vcol), kspec(same, vcol), kspec(nxt, vcol),
                  pl.BlockSpec((None, cfg.past, LANES), lambda b, i: (b, 0, 0)),
                  pl.BlockSpec((None, cfg.past, LANES), lambda b, i: (b, 0, 0)),
                  pl.BlockSpec((LANES, 512), lambda b, i: (i, 0)),
                  pl.BlockSpec((LANES, 512), lambda b, i: (i, 0)),
                  tspec(prev), tspec(prev), tspec(same), tspec(same), tspec(nxt), tspec(nxt)],
        out_specs=pl.BlockSpec((LANES, 512), lambda b, i: (b * nql + i, 0)),
        out_shape=jax.ShapeDtypeStruct((cfg.lat_b * cfg.lat_n, 512), F32),
        compiler_params=_cp(("parallel", "parallel")),
        name="attn_latent",
    )(sink, proj, proj, proj, proj, proj, proj, proj, cache_k, cache_v,
      cos_t, sin_t, cos_t, sin_t, cos_t, sin_t, cos_t, sin_t)
    return jnp.concatenate([y_ctx, y_lat], axis=0)


def _rope_tables(n, head, width):
    rows = jnp.repeat(jnp.arange(n // GRID_W, dtype=F32), GRID_W)
    cols = (jnp.arange(n) % GRID_W).astype(F32)
    nf = head // 4
    inv = ROPE_BASE ** (-jnp.arange(nf, dtype=F32) / nf)
    ang = jnp.concatenate([rows[:, None] * inv, cols[:, None] * inv], axis=-1)
    c, s = jnp.cos(ang), jnp.sin(ang)
    cos_h = jnp.concatenate([c, c], axis=-1)
    sin_h = jnp.concatenate([-s, s], axis=-1)
    reps = width // head
    return jnp.tile(cos_h, (1, reps)), jnp.tile(sin_h, (1, reps))


def _even_out_kernel(x_ref, u_ref, yd_ref, ya_ref, d_ref, glu_ref, wo_ref, g_ref, o_ref):
    y = jax.nn.gelu(u_ref[...] * d_ref[...] + yd_ref[...])
    z = y * jax.nn.sigmoid(_dot(y.astype(BF16), glu_ref[...]))
    cat = jnp.concatenate([z, ya_ref[...]], axis=-1).astype(BF16)
    o_ref[...] = x_ref[...] + g_ref[...] * _dot(cat, wo_ref[...])


def even_out(cfg, x, proj, yd, yatt, s5_d, glu_bf16, wo_bf16, mod5, layer):
    tile = lambda w: pl.BlockSpec((ROW_TILE, w), lambda i: (i, 0))
    return pl.pallas_call(
        _even_out_kernel,
        grid=(cfg.t // ROW_TILE,),
        in_specs=[tile(D_MODEL), tile(S5_WIDTH), tile(S5_WIDTH), tile(512),
                  pl.BlockSpec((1, S5_WIDTH), lambda i: (0, 0)),
                  pl.BlockSpec((S5_WIDTH, S5_WIDTH), lambda i: (0, 0)),
                  pl.BlockSpec((D_MODEL, D_MODEL), lambda i: (0, 0)),
                  _mod_spec(cfg, layer, 2)],
        out_specs=tile(D_MODEL),
        out_shape=jax.ShapeDtypeStruct((cfg.t, D_MODEL), F32),
        compiler_params=_cp(("parallel",)),
        name="even_out",
    )(x, proj, yd, yatt, s5_d.reshape(1, S5_WIDTH), glu_bf16, wo_bf16, mod5)


REC_CHUNK = 128
REC_LEVELS = 7


def _rec_tables():
    c = REC_CHUNK
    t = np.arange(c)[:, None]
    u = np.arange(c)[None, :]
    out = np.zeros((2, (REC_LEVELS + 2) * c, c), np.float32)
    for d in range(2):
        for lv in range(REC_LEVELS):
            h = 1 << lv
            mid = (t // (2 * h)) * (2 * h) + h
            second = (t % (2 * h)) >= h
            if d == 0:
                a = np.where(second, (u >= mid) & (u <= t), (u > t) & (u < mid))
            else:
                a = np.where(second, (u >= mid) & (u < t), (u >= t) & (u < mid))
            out[d, lv * c:(lv + 1) * c] = a
        if d == 0:
            out[d, REC_LEVELS * c:(REC_LEVELS + 1) * c] = u <= t
            out[d, (REC_LEVELS + 1) * c:] = u > t
        else:
            out[d, REC_LEVELS * c:(REC_LEVELS + 1) * c] = u >= t
            out[d, (REC_LEVELS + 1) * c:] = u < t
    return out


def _rec_kernel(cfg, layer_o, q_ref, zf_ref, zb_ref, v_ref, rq_ref, rk_ref, rv_ref,
                amat_ref, lbl_ref, rdec_ref, cos_ref, sin_ref, s0_ref, o_ref, st_ref):
    c = REC_CHUNK
    d = pl.program_id(0)
    step = pl.program_id(1)
    n_chunks = cfg.t // c
    chunk = jnp.where(d == 0, step, n_chunks - 1 - step)
    cpc = cfg.ctx_n // c
    cpl = cfg.lat_n // c
    n_ctx_chunks = cfg.tc // c
    is_lat = chunk >= n_ctx_chunks
    pos = jnp.where(is_lat, (chunk - n_ctx_chunks) % cpl, chunk % cpc)
    last = jnp.where(is_lat, cpl - 1, cpc - 1)
    first_processed = jnp.where(d == 0, pos == 0, pos == last)

    @pl.when(first_processed & jnp.logical_not(is_lat))
    def _():
        st_ref[...] = jnp.zeros_like(st_ref)

    @pl.when(first_processed & is_lat)
    def _():
        st_ref[...] = s0_ref[...]

    row = lax.broadcasted_iota(I32, (c, c), 0)
    col = lax.broadcasted_iota(I32, (c, c), 1)
    fwd = d == 0
    amat = amat_ref[...]

    def mix_head(q, k, v, scores, e_q, e_k, hidx):
        o = _dot(scores.astype(BF16), v.astype(BF16))
        st = st_ref[hidx]
        o = o + _nt((q * e_q).astype(BF16), st.astype(BF16))
        kd = (k * e_k).astype(BF16)
        total = jnp.where(fwd, e_q[c - 1:c], e_q[0:1])
        st_ref[hidx] = st * total + _dot(v.T.astype(BF16), kd)
        return o

    def hier_scores(q, k, e):
        scores = jnp.where(row == col, _nt(q.astype(BF16), k.astype(BF16)), 0.0)
        for lv in range(REC_LEVELS):
            h = 1 << lv
            el = e[lv * c:(lv + 1) * c]
            second = (row % (2 * h)) >= h
            is_q = second == fwd
            qt = jnp.where(is_q, q * el, 0.0).astype(BF16)
            kt = jnp.where(is_q, 0.0, k * el).astype(BF16)
            same = (row // (2 * h)) == (col // (2 * h))
            scores = scores + jnp.where(same, _nt(qt, kt), 0.0)
        return scores

    lbl = lbl_ref[...]
    mx = jnp.max(lbl, axis=0, keepdims=True)
    ex = jnp.exp(lbl - mx)
    sm = ex / jnp.sum(ex, axis=0, keepdims=True)
    lb_all = jnp.zeros_like(sm[0])
    for i in range(1, layer_o + 1):
        lb_all = lb_all + sm[i]
    lb = jnp.where(fwd, lb_all[0:1], lb_all[1:2])
    z = jnp.where(fwd, zf_ref[...], zb_ref[...])
    log_sig = jnp.minimum(z, 0.0) - jnp.log1p(jnp.exp(-jnp.abs(z)))
    a_ = jnp.log1p(-lb) + log_sig
    b_ = jnp.log(lb)
    big = jnp.maximum(a_, b_)
    logf = big + jnp.log1p(jnp.exp(-jnp.abs(a_ - b_)))
    kh = (1.0 - lb) * jax.nn.sigmoid(-z)
    l1 = logf.astype(BF16)
    r1 = logf - l1.astype(F32)
    l2 = r1.astype(BF16)
    l3 = (r1 - l2.astype(F32)).astype(BF16)
    qh, vh = q_ref[...], v_ref[...]
    outs = []
    for h in range(REC_HEADS):
        sl = slice(h * HEAD_DIM, (h + 1) * HEAD_DIM)
        e = jnp.exp(_dot(amat, l1[:, sl]) + _dot(amat, l2[:, sl]) + _dot(amat, l3[:, sl]))
        q, k = qh[:, sl], kh[:, sl]
        outs.append(mix_head(q, k, vh[:, sl], hier_scores(q, k, e),
                             e[REC_LEVELS * c:(REC_LEVELS + 1) * c], e[(REC_LEVELS + 1) * c:], h))
    rq, rk = rq_ref[...], rk_ref[...]
    rq_rot = _rope(rq, cos_ref[...], sin_ref[...], HEAD_DIM // 2)
    rk_rot = _rope(rk, cos_ref[...], sin_ref[...], HEAD_DIM // 2)
    rq = jnp.where(is_lat, rq_rot, rq)
    rk = jnp.where(is_lat, rk_rot, rk) * (HEAD_DIM ** -0.5)
    rv = rv_ref[...]
    gam = jnp.exp(jnp.where(fwd, rdec_ref[0:1], rdec_ref[1:2]))
    lag = jnp.where(fwd, row - col, col - row)
    lag_f = jnp.maximum(lag, 0).astype(F32)
    steps_q = jnp.where(fwd, row + 1, c - row).astype(F32)
    steps_k = jnp.where(fwd, c - 1 - row, row).astype(F32)
    for h in range(REC_HEADS):
        sl = slice(h * HEAD_DIM, (h + 1) * HEAD_DIM)
        g = gam[:, sl]
        q, k = rq[:, sl], rk[:, sl]
        scores = jnp.where(lag >= 0, _nt(q.astype(BF16), k.astype(BF16)) * jnp.exp(-g * lag_f), 0.0)
        outs.append(mix_head(q, k, rv[:, sl], scores, jnp.exp(-g * steps_q), jnp.exp(-g * steps_k),
                             REC_HEADS + h))
    o_ref[...] = jnp.concatenate(outs, axis=1)


def recurrences(cfg, layer_o, proj, amat, lb_logits, ret_decay_lanes, rope_cs, s0):
    c = REC_CHUNK
    n_chunks = cfg.t // c
    n_ctx_chunks = cfg.tc // c
    cpc, cpl = cfg.ctx_n // c, cfg.lat_n // c
    n_seq = cfg.ctx_b + cfg.lat_b

    def chunk_of(d, s):
        return jnp.where(d == 0, s, n_chunks - 1 - s)

    def seq_of(d, s):
        ch = chunk_of(d, s)
        return jnp.where(ch >= n_ctx_chunks, cfg.ctx_b + (ch - n_ctx_chunks) // cpl, ch // cpc)

    def lat_of(d, s):
        return jnp.maximum(seq_of(d, s) - cfg.ctx_b, 0)

    def lat_pos(d, s):
        ch = chunk_of(d, s)
        return jnp.where(ch >= n_ctx_chunks, (ch - n_ctx_chunks) % cpl, 0)

    col = lambda k: pl.BlockSpec((c, 512), lambda d, s: (chunk_of(d, s), k))
    nrow = (REC_LEVELS + 2) * c
    cos_t, sin_t = rope_cs
    return pl.pallas_call(
        functools.partial(_rec_kernel, cfg, layer_o),
        grid=(2, n_chunks),
        in_specs=[col(0), col(1), col(2), col(3), col(5), col(6), col(7),
                  pl.BlockSpec((None, nrow, c), lambda d, s: (d, 0, 0)),
                  pl.BlockSpec((N_ODD, 2, 512), lambda d, s: (0, 0, 0)),
                  pl.BlockSpec((None, 2, 512), lambda d, s: (0, 0, 0)),
                  pl.BlockSpec((c, 512), lambda d, s: (lat_pos(d, s), 0)),
                  pl.BlockSpec((c, 512), lambda d, s: (lat_pos(d, s), 0)),
                  pl.BlockSpec((None, None, 2 * REC_HEADS, HEAD_DIM, HEAD_DIM),
                               lambda d, s: (d, lat_of(d, s), 0, 0, 0))],
        out_specs=[pl.BlockSpec((None, c, D_MODEL), lambda d, s: (d, chunk_of(d, s), 0)),
                   pl.BlockSpec((None, None, 2 * REC_HEADS, HEAD_DIM, HEAD_DIM),
                                lambda d, s: (d, seq_of(d, s), 0, 0, 0))],
        out_shape=[jax.ShapeDtypeStruct((2, cfg.t, D_MODEL), F32),
                   jax.ShapeDtypeStruct((2, n_seq, 2 * REC_HEADS, HEAD_DIM, HEAD_DIM), F32)],
        compiler_params=_cp(("parallel", "arbitrary")),
        name="recurrences",
    )(proj, proj, proj, proj, proj, proj, proj, amat, lb_logits, ret_decay_lanes, cos_t, sin_t, s0)


def _odd_out_kernel(x_ref, of_ref, ob_ref, cg_ref, rg_ref, hw_ref, rw_ref, wo_ref, g_ref, o_ref):
    o = of_ref[...] + ob_ref[...]
    hw, rw = hw_ref[...], rw_ref[...]
    parts = []
    for h in range(REC_HEADS):
        sl = slice(h * HEAD_DIM, (h + 1) * HEAD_DIM)
        oh = o[:, sl]
        parts.append(oh * lax.rsqrt(jnp.mean(oh * oh, -1, keepdims=True) + EPS) * hw[:, sl])
    oc = jnp.concatenate(parts, axis=1) * jax.nn.sigmoid(cg_ref[...])
    parts = []
    for h in range(REC_HEADS):
        sl = slice(h * HEAD_DIM, (h + 1) * HEAD_DIM)
        oh = o[:, 512 + h * HEAD_DIM:512 + (h + 1) * HEAD_DIM]
        oh = oh - jnp.mean(oh, -1, keepdims=True)
        parts.append(oh * lax.rsqrt(jnp.mean(oh * oh, -1, keepdims=True) + EPS) * rw[:, sl])
    rg = rg_ref[...]
    orr = jnp.concatenate(parts, axis=1) * (rg * jax.nn.sigmoid(rg))
    cat = jnp.concatenate([oc, orr], axis=1).astype(BF16)
    o_ref[...] = x_ref[...] + g_ref[...] * _dot(cat, wo_ref[...])


def odd_out(cfg, x, o2, proj, hgrn_norm_w, ret_norm_w, wo_bf16, mod5, layer):
    tile = lambda w, k=0: pl.BlockSpec((ROW_TILE, w), lambda i: (i, k))
    return pl.pallas_call(
        _odd_out_kernel,
        grid=(cfg.t // ROW_TILE,),
        in_specs=[tile(D_MODEL),
                  pl.BlockSpec((None, ROW_TILE, D_MODEL), lambda i: (0, i, 0)),
                  pl.BlockSpec((None, ROW_TILE, D_MODEL), lambda i: (1, i, 0)),
                  tile(512, 4), tile(512, 8),
                  pl.BlockSpec((1, 512), lambda i: (0, 0)), pl.BlockSpec((1, 512), lambda i: (0, 0)),
                  pl.BlockSpec((D_MODEL, D_MODEL), lambda i: (0, 0)),
                  _mod_spec(cfg, layer, 2)],
        out_specs=tile(D_MODEL),
        out_shape=jax.ShapeDtypeStruct((cfg.t, D_MODEL), F32),
        compiler_params=_cp(("parallel",)),
        name="odd_out",
    )(x, o2, o2, proj, proj, hgrn_norm_w.reshape(1, 512), ret_norm_w.reshape(1, 512), wo_bf16, mod5)


SLAB = D_MODEL // LANES


def _norm_logits_kernel(x_ref, nw_ref, sh_ref, sc_ref, wr_ref, h_ref, lg_ref):
    h = _rms(x_ref[...], nw_ref[...]) * (1.0 + sc_ref[...]) + sh_ref[...]
    for s in range(SLAB):
        h_ref[pl.ds(s, ROW_TILE, stride=SLAB), :] = h[:, s * LANES:(s + 1) * LANES]
    lg_ref[...] = _nt(wr_ref[...], h.astype(BF16))


def moe_norm_logits(cfg, x, nw, mod5, layer, wr_t_bf16):
    return pl.pallas_call(
        _norm_logits_kernel,
        grid=(cfg.t // ROW_TILE,),
        in_specs=[pl.BlockSpec((ROW_TILE, D_MODEL), lambda i: (i, 0)),
                  pl.BlockSpec((1, D_MODEL), lambda i: (0, 0)),
                  _mod_spec(cfg, layer, 3), _mod_spec(cfg, layer, 4),
                  pl.BlockSpec((N_EXPERTS, D_MODEL), lambda i: (0, 0))],
        out_specs=[pl.BlockSpec((ROW_TILE * SLAB, LANES), lambda i: (i, 0)),
                   pl.BlockSpec((N_EXPERTS, ROW_TILE), lambda i: (0, i))],
        out_shape=[jax.ShapeDtypeStruct((cfg.t * SLAB, LANES), F32),
                   jax.ShapeDtypeStruct((N_EXPERTS, cfg.t), F32)],
        compiler_params=_cp(("parallel",)),
        name="moe_norm_logits",
    )(x, nw.reshape(1, D_MODEL), mod5, mod5, wr_t_bf16)


def _route_kernel(cfg, lg_ref, pos_ref, aff_ref, cj_ref):
    gs = cfg.gs
    g = pl.program_id(0)
    logits = lg_ref[...]
    mx = jnp.max(logits, axis=0, keepdims=True)
    ex = jnp.exp(logits - mx)
    aff = ex / jnp.sum(ex, axis=0, keepdims=True)
    aff_ref[...] = aff

    def seg_sum(x, seg):
        parts = []
        for r in range(gs // seg):
            s = jnp.sum(x[:, r * seg:(r + 1) * seg], axis=1, keepdims=True)
            parts.append(jnp.broadcast_to(s, (N_EXPERTS, seg)))
        return parts[0] if len(parts) == 1 else jnp.concatenate(parts, axis=1)

    tri = (lax.broadcasted_iota(I32, (LANES, LANES), 0) < lax.broadcasted_iota(I32, (LANES, LANES), 1)).astype(BF16)
    lane = lax.broadcasted_iota(I32, (N_EXPERTS, LANES), 1)

    def prefix(mask, seg):
        parts = []
        carry = jnp.zeros((N_EXPERTS, 1), F32)
        starts = jnp.zeros((N_EXPERTS, LANES), F32)
        for j in range(gs // LANES):
            if (j * LANES) % seg == 0:
                carry = jnp.zeros((N_EXPERTS, 1), F32)
            m = mask[:, j * LANES:(j + 1) * LANES]
            starts = starts + jnp.where(lane == j, carry, 0.0)
            parts.append(_dot(m.astype(BF16), tri) + carry)
            carry = carry + jnp.sum(m, axis=1, keepdims=True)
        starts = starts + jnp.where(lane == gs // LANES, carry, 0.0)
        return jnp.concatenate(parts, axis=1), starts

    def select(seg, cap):
        def at_least_cap(mid):
            return seg_sum((aff >= mid).astype(F32), seg) >= cap

        def halve_bits(_, lohi):
            lo, hi = lohi
            mid = lo + ((hi - lo) >> 1)
            ok = at_least_cap(pltpu.bitcast(mid, F32))
            return jnp.where(ok, mid, lo), jnp.where(ok, hi, mid)

        def halve_value(_, lohi):
            lo, hi = lohi
            mid = 0.5 * (lo + hi)
            ok = at_least_cap(mid)
            return jnp.where(ok, mid, lo), jnp.where(ok, hi, mid)

        lo0 = jnp.zeros((N_EXPERTS, gs), I32)
        hi0 = jnp.full((N_EXPERTS, gs), F32_INF_BITS, I32)
        lo_b, hi_b = lax.fori_loop(0, 31, halve_bits, (lo0, hi0))
        lo_f = jnp.where(lo_b < F32_MIN_NORMAL_BITS, 0.0, pltpu.bitcast(lo_b, F32))
        lo_f, hi_f = lax.fori_loop(0, 30, halve_value, (lo_f, pltpu.bitcast(hi_b, F32)))
        above = aff >= hi_f
        tied = jnp.logical_and(aff >= lo_f, aff < hi_f)
        need = cap - seg_sum(above.astype(F32), seg)
        before, _ = prefix(tied.astype(F32), seg)
        return jnp.logical_or(above, jnp.logical_and(tied, before < need))

    def finish(sel):
        self_f = sel.astype(F32)
        slot, starts = prefix(self_f, gs)
        pos_ref[...] = jnp.where(sel, slot.astype(I32), -1)
        cj_ref[...] = starts.astype(I32)

    @pl.when(g == 0)
    def _():
        finish(select(cfg.ctx_n, CAPACITY_FACTOR * cfg.ctx_n // N_EXPERTS))

    @pl.when(g > 0)
    def _():
        finish(select(cfg.lat_n, CAPACITY_FACTOR * cfg.lat_n // N_EXPERTS))


def moe_route(cfg, logits):
    gs = cfg.gs
    return pl.pallas_call(
        functools.partial(_route_kernel, cfg),
        grid=(cfg.groups,),
        in_specs=[pl.BlockSpec((N_EXPERTS, gs), lambda g: (0, g))],
        out_specs=[pl.BlockSpec((None, N_EXPERTS, gs), lambda g: (g, 0, 0)),
                   pl.BlockSpec((None, N_EXPERTS, gs), lambda g: (g, 0, 0)),
                   pl.BlockSpec((None, N_EXPERTS, LANES), lambda g: (g, 0, 0))],
        out_shape=[jax.ShapeDtypeStruct((cfg.groups, N_EXPERTS, gs), I32),
                   jax.ShapeDtypeStruct((cfg.groups, N_EXPERTS, gs), F32),
                   jax.ShapeDtypeStruct((cfg.groups, N_EXPERTS, LANES), I32)],
        compiler_params=_cp(("parallel",)),
        name="moe_route",
    )(logits)


def _slot_block(cfg):
    return min(LANES, cfg.slots)


def _index_kernel(cfg, cj_ref, pos_ref, aff_ref, idx_ref, gate_ref):
    g, e = pl.program_id(0), pl.program_id(1)
    sb = _slot_block(cfg)
    idx_ref[...] = jnp.zeros_like(idx_ref)
    gate_ref[...] = jnp.zeros_like(gate_ref)
    base = (g * N_EXPERTS + e) * LANES
    sub = lax.broadcasted_iota(I32, (sb, LANES), 0)
    lane = lax.broadcasted_iota(I32, (sb, LANES), 1)
    for tb in range(cfg.gs // LANES):
        s0, s1 = cj_ref[base + tb], cj_ref[base + tb + 1]
        for jb in range(cfg.slots // sb):
            @pl.when((s1 > s0) & (s0 < (jb + 1) * sb) & (s1 > jb * sb))
            def _():
                p = pos_ref[:, tb * LANES:(tb + 1) * LANES]
                hit = jnp.broadcast_to(p, (sb, LANES)) == (sub + jb * sb)
                tok = (lane + (g * cfg.gs + tb * LANES)).astype(F32)
                idx_ref[jb * sb:(jb + 1) * sb, :] += jnp.sum(jnp.where(hit, tok, 0.0), axis=1, keepdims=True)
                a = aff_ref[:, tb * LANES:(tb + 1) * LANES]
                gate_ref[jb * sb:(jb + 1) * sb, :] += jnp.sum(jnp.where(hit, a, 0.0), axis=1, keepdims=True)


def moe_index(cfg, cj, pos, aff):
    gs, slots = cfg.gs, cfg.slots
    pos4 = pos.reshape(cfg.groups, N_EXPERTS, 1, gs)
    aff4 = aff.reshape(cfg.groups, N_EXPERTS, 1, gs)
    grid_spec = pltpu.PrefetchScalarGridSpec(
        num_scalar_prefetch=1,
        grid=(cfg.groups, N_EXPERTS),
        in_specs=[pl.BlockSpec((None, None, 1, gs), lambda g, e, cj: (g, e, 0, 0)),
                  pl.BlockSpec((None, None, 1, gs), lambda g, e, cj: (g, e, 0, 0))],
        out_specs=[pl.BlockSpec((None, slots, 1), lambda g, e, cj: (e, g, 0)),
                   pl.BlockSpec((None, slots, 1), lambda g, e, cj: (e, g, 0))])
    return pl.pallas_call(
        functools.partial(_index_kernel, cfg),
        grid_spec=grid_spec,
        out_shape=[jax.ShapeDtypeStruct((N_EXPERTS, cfg.groups * slots, 1), F32),
                   jax.ShapeDtypeStruct((N_EXPERTS, cfg.groups * slots, 1), F32)],
        compiler_params=_cp(("parallel", "arbitrary")),
        name="moe_index",
    )(cj.reshape(-1), pos4, aff4)


GATHER_BATCH = 512


def _row_gather_kernel(n_steps, idx_ref, h_ref, x_ref, sems):
    k = pl.program_id(0)

    def copy(i, slot):
        t = idx_ref[i]
        return pltpu.make_async_copy(h_ref.at[pl.ds(pl.multiple_of(t * SLAB, SLAB), SLAB)],
                                     x_ref.at[pl.ds(pl.multiple_of(i * SLAB, SLAB), SLAB)], sems.at[slot])

    def issue(j, _):
        copy(k * GATHER_BATCH + j, k % 2).start()
        return 0

    lax.fori_loop(0, GATHER_BATCH, issue, 0, unroll=8)

    def drain(step):
        def wait(j, _):
            copy(step * GATHER_BATCH + j, step % 2).wait()
            return 0
        lax.fori_loop(0, GATHER_BATCH, wait, 0, unroll=8)

    @pl.when(k > 0)
    def _():
        drain(k - 1)

    @pl.when(k == n_steps - 1)
    def _():
        drain(k)


def moe_row_gather(cfg, idx, h_slabs):
    n_rows = idx.shape[0]
    n_steps = n_rows // GATHER_BATCH
    grid_spec = pltpu.PrefetchScalarGridSpec(
        num_scalar_prefetch=1,
        grid=(n_steps,),
        in_specs=[pl.BlockSpec(memory_space=pl.ANY)],
        out_specs=pl.BlockSpec(memory_space=pl.ANY),
        scratch_shapes=[pltpu.SemaphoreType.DMA((2,))])
    return pl.pallas_call(
        functools.partial(_row_gather_kernel, n_steps),
        grid_spec=grid_spec,
        out_shape=jax.ShapeDtypeStruct((n_rows * SLAB, LANES), F32),
        compiler_params=_cp(("arbitrary",)),
        name="moe_row_gather",
    )(idx, h_slabs)


def _ffn_kernel(rows, x_ref, gate_ref, w1_ref, w3_ref, w2_ref, y_ref, acc_ref, xb, w1b, w3b, w2b):
    c = pl.program_id(1)
    w1b[...] = w1_ref[...].astype(BF16)
    w3b[...] = w3_ref[...].astype(BF16)
    w2b[...] = w2_ref[...].astype(BF16)
    rt = min(ROW_TILE, rows)

    @pl.when(c == 0)
    def _():
        for s in range(SLAB):
            xb[:, s * LANES:(s + 1) * LANES] = x_ref[pl.ds(s, rows, stride=SLAB), :].astype(BF16)

    def body(i, _):
        r = pl.multiple_of(i * rt, rt)
        x = xb[pl.ds(r, rt), :]
        a = _dot(x, w1b[...])
        b = _dot(x, w3b[...])
        hid = (a * jax.nn.sigmoid(a) * b).astype(BF16)
        y = _dot(hid, w2b[...])

        @pl.when(c == 0)
        def _():
            acc_ref[pl.ds(r, rt), :] = y

        @pl.when(c > 0)
        def _():
            acc_ref[pl.ds(r, rt), :] += y

        return 0

    lax.fori_loop(0, rows // rt, body, 0)

    @pl.when(c == pl.num_programs(1) - 1)
    def _():
        y_ref[...] = (acc_ref[...] * gate_ref[...]).astype(BF16)


def moe_ffn(cfg, layer, xg, gate, w1, w3, w2):
    rows = cfg.groups * cfg.slots
    nff = EXPERT_FF // FF_TILE
    return pl.pallas_call(
        functools.partial(_ffn_kernel, rows),
        grid=(N_EXPERTS, nff),
        in_specs=[pl.BlockSpec((rows * SLAB, LANES), lambda e, c: (e, 0)),
                  pl.BlockSpec((None, rows, 1), lambda e, c: (e, 0, 0)),
                  pl.BlockSpec((None, None, D_MODEL, FF_TILE), lambda e, c: (layer, e, 0, c)),
                  pl.BlockSpec((None, None, D_MODEL, FF_TILE), lambda e, c: (layer, e, 0, c)),
                  pl.BlockSpec((None, None, FF_TILE, D_MODEL), lambda e, c: (layer, e, c, 0))],
        out_specs=pl.BlockSpec((None, rows, D_MODEL), lambda e, c: (e, 0, 0)),
        out_shape=jax.ShapeDtypeStruct((N_EXPERTS, rows, D_MODEL), BF16),
        scratch_shapes=[pltpu.VMEM((rows, D_MODEL), F32), pltpu.VMEM((rows, D_MODEL), BF16),
                        pltpu.VMEM((D_MODEL, FF_TILE), BF16), pltpu.VMEM((D_MODEL, FF_TILE), BF16),
                        pltpu.VMEM((FF_TILE, D_MODEL), BF16)],
        compiler_params=_cp(("parallel", "arbitrary")),
        name="moe_ffn",
    )(xg, gate, w1, w3, w2)


def _scatter_kernel(cfg, final, cj_ref, x_ref, pos_ref, y_ref, g_ref, fw_ref, o_ref, acc_ref):
    g, tb = pl.program_id(0), pl.program_id(1)
    sb = _slot_block(cfg)
    acc_ref[...] = jnp.zeros_like(acc_ref)
    sub = lax.broadcasted_iota(I32, (sb, LANES), 0)
    for e in range(N_EXPERTS):
        base = (g * N_EXPERTS + e) * LANES
        s0, s1 = cj_ref[base + tb], cj_ref[base + tb + 1]
        for jb in range(cfg.slots // sb):
            @pl.when((s1 > s0) & (s0 < (jb + 1) * sb) & (s1 > jb * sb))
            def _():
                hit = jnp.broadcast_to(pos_ref[e:e + 1, :], (sb, LANES)) == (sub + jb * sb)
                hit_t = jnp.where(hit, 1.0, 0.0).T.astype(BF16)
                acc_ref[...] += _dot(hit_t, y_ref[e, jb * sb:(jb + 1) * sb, :])
    out = x_ref[...] + g_ref[...] * acc_ref[...]
    if final:
        out = _rms(out, fw_ref[...])
    o_ref[...] = out


def moe_scatter(cfg, layer, cj, x, pos, y, mod5, final_w, final):
    gs, slots = cfg.gs, cfg.slots
    nt = gs // LANES
    grid_spec = pltpu.PrefetchScalarGridSpec(
        num_scalar_prefetch=1,
        grid=(cfg.groups, nt),
        in_specs=[pl.BlockSpec((LANES, D_MODEL), lambda g, tb, cj: (g * nt + tb, 0)),
                  pl.BlockSpec((None, N_EXPERTS, LANES), lambda g, tb, cj: (g, 0, tb)),
                  pl.BlockSpec((N_EXPERTS, slots, D_MODEL), lambda g, tb, cj: (0, g, 0)),
                  pl.BlockSpec((None, None, None, 1, D_MODEL), lambda g, tb, cj: (layer, g, 5, 0, 0)),
                  pl.BlockSpec((1, D_MODEL), lambda g, tb, cj: (0, 0))],
        out_specs=pl.BlockSpec((LANES, D_MODEL), lambda g, tb, cj: (g * nt + tb, 0)),
        scratch_shapes=[pltpu.VMEM((LANES, D_MODEL), F32)])
    return pl.pallas_call(
        functools.partial(_scatter_kernel, cfg, final),
        grid_spec=grid_spec,
        out_shape=jax.ShapeDtypeStruct((cfg.t, D_MODEL), F32),
        compiler_params=_cp(("parallel", "arbitrary")),
        name="moe_scatter",
    )(cj.reshape(-1), x, pos, y, mod5, final_w.reshape(1, D_MODEL))


def _forward(cfg, x_prompt, x_sample, state_s5, cache_k, cache_v, state_hgrn, state_ret, c, c_ctx, p):
    assert cfg.tc == cfg.lat_n and cfg.ctx_n % ROW_TILE == 0 and cfg.lat_n % (GRID_W * 2) == 0
    x = jnp.concatenate([x_prompt.reshape(cfg.tc, D_MODEL), x_sample.reshape(-1, D_MODEL)], axis=0)
    conds = jnp.zeros((8, D_MODEL), F32).at[0].set(c_ctx).at[1:1 + cfg.lat_b].set(c)
    mod = adaln_table(conds, p['mod_w'], p['mod_b'])
    mod5 = mod[:, :cfg.groups].reshape(DEPTH, cfg.groups, 6, 1, D_MODEL)
    rope_att = _rope_tables(cfg.lat_n, ATT_HEAD_DIM, 512)
    rope_ret = _rope_tables(cfg.lat_n, HEAD_DIM, 512)
    amat = jnp.asarray(_rec_tables(), BF16)
    s5_fin, k_out, v_out, hgrn_fin, ret_fin = [], [], [], [], []
    for layer in range(DEPTH):
        if layer % 2 == 0:
            e = layer // 2
            proj = norm_mod_matmul(cfg, x, p['norm1_w'][layer], mod5, layer, p['ab_w_in'][e].astype(BF16))
            wcat, mmat, gmat, a16 = _s5_tables(p['s5_lambda_re'][e], p['s5_lambda_im'][e], p['s5_log_dt'][e],
                                               p['s5_b_re'][e], p['s5_b_im'][e], p['s5_c_re'][e], p['s5_c_im'][e])
            h0 = state_s5[:, e].reshape(cfg.lat_b, 4, S5_PAIRS, 2 * S5_STATE).transpose(2, 0, 1, 3)
            yd, fin = s5_mix(cfg, proj, wcat, mmat, gmat, a16, h0)
            s5_fin.append(fin.transpose(2, 1, 0, 3).reshape(cfg.ctx_b, 2, 2, S5_GROUPS, S5_STATE))
            ck = cache_k[:, e].reshape(cfg.lat_b, cfg.past, LANES)
            cv = cache_v[:, e].reshape(cfg.lat_b, cfg.past, LANES)
            yatt = attention(cfg, proj, p['attn_sink'][e], ck, cv, rope_att)
            k_out.append(proj[:cfg.tc, 1024:1152].reshape(cfg.ctx_b, cfg.ctx_n, ATT_KV_HEADS, ATT_HEAD_DIM))
            v_out.append(proj[:cfg.tc, 1152:1280].reshape(cfg.ctx_b, cfg.ctx_n, ATT_KV_HEADS, ATT_HEAD_DIM))
            x = even_out(cfg, x, proj, yd, yatt, p['s5_d'][e], p['s5_glu_w'][e].astype(BF16),
                         p['ab_w_out'][e].astype(BF16), mod5, layer)
        else:
            o = layer // 2
            proj = norm_mod_matmul(cfg, x, p['norm1_w'][layer], mod5, layer, p['cd_w_in'][o].astype(BF16))
            s0 = jnp.concatenate([state_hgrn[:, o], state_ret[:, o]], axis=2).transpose(1, 0, 2, 4, 3)
            rdec = jnp.repeat(p['ret_decay'][o], HEAD_DIM, axis=-1)
            o2, st = recurrences(cfg, o, proj, amat, p['hgrn_lb_logits'], rdec[None], rope_ret, s0)
            st = st[:, :cfg.ctx_b].transpose(1, 0, 2, 4, 3)
            hgrn_fin.append(st[:, :, :REC_HEADS])
            ret_fin.append(st[:, :, REC_HEADS:])
            x = odd_out(cfg, x, o2, proj, p['hgrn_norm_w'][o], p['ret_norm_w'][o],
                        p['cd_w_out'][o].astype(BF16), mod5, layer)
        h_slabs, logits = moe_norm_logits(cfg, x, p['norm2_w'][layer], mod5, layer,
                                          p['router_w'][layer].T.astype(BF16))
        pos, aff, cj = moe_route(cfg, logits)
        idx, gate = moe_index(cfg, cj, pos, aff)
        xg = moe_row_gather(cfg, idx.reshape(-1).astype(I32), h_slabs)
        yg = moe_ffn(cfg, layer, xg, gate, p['moe_w1'], p['moe_w3'], p['moe_w2'])
        x = moe_scatter(cfg, layer, cj, x, pos, yg, mod5, p['final_norm_w'], layer == DEPTH - 1)
    y_prompt = x[:cfg.tc].reshape(cfg.ctx_b, cfg.ctx_n, D_MODEL)
    y_sample = x[cfg.tc:].reshape(cfg.lat_b, cfg.lat_n, D_MODEL)
    return (y_prompt, y_sample, jnp.stack(s5_fin, axis=1), jnp.stack(k_out, axis=1), jnp.stack(v_out, axis=1),
            jnp.stack(hgrn_fin, axis=1), jnp.stack(ret_fin, axis=1))


def kernel(x_prompt, x_sample, state_s5, cache_k, cache_v, state_hgrn, state_ret, c, c_ctx, mod_w, mod_b, norm1_w, norm2_w, final_norm_w, ab_w_in, ab_w_out, s5_lambda_re, s5_lambda_im, s5_log_dt, s5_b_re, s5_b_im, s5_c_re, s5_c_im, s5_d, s5_glu_w, attn_sink, cd_w_in, cd_w_out, hgrn_lb_logits, hgrn_norm_w, ret_decay, ret_norm_w, router_w, moe_w1, moe_w3, moe_w2):
    cfg = Cfg(ctx_b=x_prompt.shape[0], ctx_n=x_prompt.shape[1], lat_b=x_sample.shape[0],
              lat_n=x_sample.shape[1], past=cache_k.shape[2])
    p = dict(mod_w=mod_w, mod_b=mod_b, norm1_w=norm1_w, norm2_w=norm2_w, final_norm_w=final_norm_w,
             ab_w_in=ab_w_in, ab_w_out=ab_w_out, s5_lambda_re=s5_lambda_re, s5_lambda_im=s5_lambda_im,
             s5_log_dt=s5_log_dt, s5_b_re=s5_b_re, s5_b_im=s5_b_im, s5_c_re=s5_c_re, s5_c_im=s5_c_im,
             s5_d=s5_d, s5_glu_w=s5_glu_w, attn_sink=attn_sink, cd_w_in=cd_w_in, cd_w_out=cd_w_out,
             hgrn_lb_logits=hgrn_lb_logits, hgrn_norm_w=hgrn_norm_w, ret_decay=ret_decay,
             ret_norm_w=ret_norm_w, router_w=router_w, moe_w1=moe_w1, moe_w3=moe_w3, moe_w2=moe_w2)
    return _forward(cfg, x_prompt, x_sample, state_s5, cache_k, cache_v, state_hgrn, state_ret, c, c_ctx, p)
```

```python
import functools
import math
from typing import NamedTuple

import numpy as np
import jax
import jax.numpy as jnp
from jax import lax
from jax.experimental import pallas as pl
from jax.experimental.pallas import tpu as pltpu

F32 = jnp.float32
BF16 = jnp.bfloat16
I32 = jnp.int32

D_MODEL = 1024
DEPTH = 4
N_EVEN = 2
N_ODD = 2
EPS = 1e-6
GRID_W = 64
S5_WIDTH = 512
S5_GROUP = 16
S5_GROUPS = 32
S5_STATE = 64
S5_CHUNK = 16
S5_PAIRS = S5_GROUPS // 2
S5_QUAD = 4
ATT_HEAD_DIM = 64
ATT_HEADS = 8
ATT_KV_HEADS = 2
ATT_GROUP = 4
WINDOW = 128
ROPE_BASE = 10000.0
AB_IN = 1280
HEAD_DIM = 128
REC_HEADS = 4
CD_IN = 4608
N_EXPERTS = 16
EXPERT_FF = 1536
CAPACITY_FACTOR = 2

LANES = 128
ROW_TILE = 256
FFN_ROWS = 512
FF_TILE = 512
F32_INF_BITS = 0x7F800000
F32_MIN_NORMAL_BITS = 0x00800000
VMEM_LIMIT = 56 * 1024 * 1024


class Cfg(NamedTuple):
    ctx_b: int
    ctx_n: int
    lat_b: int
    lat_n: int
    past: int

    @property
    def tc(self):
        return self.ctx_b * self.ctx_n

    @property
    def t(self):
        return self.tc + self.lat_b * self.lat_n

    @property
    def gs(self):
        return self.lat_n

    @property
    def groups(self):
        return 1 + self.lat_b

    @property
    def slots(self):
        return CAPACITY_FACTOR * self.gs // N_EXPERTS


def _cp(sem, vmem=VMEM_LIMIT):
    return pltpu.CompilerParams(dimension_semantics=sem, vmem_limit_bytes=vmem)


def _nt(a, b):
    return lax.dot_general(a, b, (((1,), (1,)), ((), ())), preferred_element_type=F32)


def _dot(a, b, precision=None):
    return jnp.dot(a, b, preferred_element_type=F32, precision=precision)


_HI = lax.Precision.HIGHEST


def _adaln_kernel(c_ref, w_ref, b_ref, o_ref):
    c = c_ref[...]
    s = c * jax.nn.sigmoid(c)
    o_ref[0] = _dot(s.astype(BF16), w_ref[0].astype(BF16)) + b_ref[0]


def adaln_table(conds, mod_w, mod_b):
    n6 = 6 * D_MODEL
    tn = 1536
    return pl.pallas_call(
        _adaln_kernel,
        grid=(DEPTH, n6 // tn),
        in_specs=[pl.BlockSpec((8, D_MODEL), lambda l, j: (0, 0)),
                  pl.BlockSpec((1, D_MODEL, tn), lambda l, j: (l, 0, j)),
                  pl.BlockSpec((1, 1, tn), lambda l, j: (l, 0, j))],
        out_specs=pl.BlockSpec((1, 8, tn), lambda l, j: (l, 0, j)),
        out_shape=jax.ShapeDtypeStruct((DEPTH, 8, n6), F32),
        compiler_params=_cp(("parallel", "parallel")),
        name="adaln_table",
    )(conds, mod_w, mod_b.reshape(DEPTH, 1, n6))


def _mod_spec(cfg, layer, part):
    tiles_per_group = cfg.gs // ROW_TILE
    return pl.BlockSpec((None, None, None, 1, D_MODEL),
                        lambda i: (layer, i // tiles_per_group, part, 0, 0))


def _rms(x, w):
    return x * lax.rsqrt(jnp.mean(x * x, axis=-1, keepdims=True) + EPS) * w


def _nmm_kernel(x_ref, nw_ref, sh_ref, sc_ref, w_ref, o_ref):
    h = _rms(x_ref[...], nw_ref[...]) * (1.0 + sc_ref[...]) + sh_ref[...]
    o_ref[...] = _dot(h.astype(BF16), w_ref[...])


def norm_mod_matmul(cfg, x, nw, mod5, layer, w_bf16):
    n_out = w_bf16.shape[1]
    return pl.pallas_call(
        _nmm_kernel,
        grid=(cfg.t // ROW_TILE,),
        in_specs=[pl.BlockSpec((ROW_TILE, D_MODEL), lambda i: (i, 0)),
                  pl.BlockSpec((1, D_MODEL), lambda i: (0, 0)),
                  _mod_spec(cfg, layer, 0), _mod_spec(cfg, layer, 1),
                  pl.BlockSpec((D_MODEL, n_out), lambda i: (0, 0))],
        out_specs=pl.BlockSpec((ROW_TILE, n_out), lambda i: (i, 0)),
        out_shape=jax.ShapeDtypeStruct((cfg.t, n_out), F32),
        compiler_params=_cp(("parallel",)),
        name="norm_mod_matmul",
    )(x, nw.reshape(1, D_MODEL), mod5, mod5, w_bf16)


def _s5_tables(lam_re, lam_im, log_dt, b_re, b_im, c_re, c_im):
    L = S5_CHUNK
    lr, li = lam_re.astype(F32), lam_im.astype(F32)
    dt = jnp.exp(log_dt.astype(F32))[..., None]
    mag = jnp.exp(lr * dt)
    ar, ai = mag * jnp.cos(li * dt), mag * jnp.sin(li * dt)
    den = lr * lr + li * li
    cr = ((ar - 1.0) * lr + ai * li) / den
    ci = (ai * lr - (ar - 1.0) * li) / den
    br, bi = b_re.astype(F32), b_im.astype(F32)
    bbr = cr[..., None] * br - ci[..., None] * bi
    bbi = cr[..., None] * bi + ci[..., None] * br

    def pw_step(carry, _):
        pr, pi = carry
        return (pr * ar - pi * ai, pr * ai + pi * ar), (pr, pi)

    (_, _), (pr, pi) = lax.scan(pw_step, (jnp.ones_like(ar), jnp.zeros_like(ar)), None, length=L + 1)
    er = pr[:L, ..., None] * bbr[None] - pi[:L, ..., None] * bbi[None]
    ei = pr[:L, ..., None] * bbi[None] + pi[:L, ..., None] * bbr[None]
    ccr, cci = c_re.astype(F32), c_im.astype(F32)
    kt = (jnp.einsum('dgcp,tdgpe->tdgce', ccr, er, precision=_HI)
          - jnp.einsum('dgcp,tdgpe->tdgce', cci, ei, precision=_HI))
    eye2 = jnp.eye(2, dtype=F32)
    wide = 2 * L * S5_GROUP

    def pairs(a, axis):
        return a.reshape(a.shape[:axis] + (S5_PAIRS, 2) + a.shape[axis + 1:])

    def lag_rows(k):
        k = pairs(k, 1).transpose(1, 2, 4, 0, 3)
        k = k[:, :, :, :, None, :] * eye2[None, :, None, None, :, None]
        return k.reshape(S5_PAIRS, 2 * S5_GROUP, wide)

    kc_f, kc_b = lag_rows(kt[:, 0]), lag_rows(kt[::-1, 1])
    wq = jnp.stack([er[::-1, 0], ei[::-1, 0], er[:, 1], ei[:, 1]], axis=0)
    wq = pairs(wq, 2).transpose(2, 1, 3, 5, 0, 4)
    wcat = (wq[:, :, :, :, :, None, :] * eye2[None, None, :, None, None, :, None]).reshape(S5_PAIRS, wide, wide)

    def g_coef(d, p_r, p_i):
        g_r = ccr[d][None] * p_r[:, :, None, :] - cci[d][None] * p_i[:, :, None, :]
        g_i = -ccr[d][None] * p_i[:, :, None, :] - cci[d][None] * p_r[:, :, None, :]
        return g_r, g_i

    gq = jnp.stack(g_coef(0, pr[1:L + 1, 0], pi[1:L + 1, 0]) + g_coef(1, pr[L:0:-1, 1], pi[L:0:-1, 1]), axis=0)
    gq = pairs(gq, 2).transpose(2, 0, 3, 5, 1, 4)
    gmat = (gq[:, :, :, :, :, None, :] * eye2[None, None, :, None, None, :, None]).reshape(S5_PAIRS, wide, wide)

    def pair_lanes(a):
        return a.reshape(S5_PAIRS, 2 * S5_STATE)

    a16 = jnp.stack([pair_lanes(pr[L, 0]), pair_lanes(pi[L, 0]),
                     pair_lanes(pr[L, 1]), pair_lanes(pi[L, 1])], axis=1)
    return wcat.astype(BF16), kc_f, kc_b, gmat.astype(BF16), a16


def _s5_pair(cfg, u, w_mat, m_mat, g_mat, a, h0_ref, pp, fin_ref, xs_ref):
    rc = cfg.tc // S5_CHUNK
    lc = cfg.ctx_n // S5_CHUNK
    ll = cfg.lat_n // S5_CHUNK
    r_all = cfg.t // S5_CHUNK
    w = _dot(u, w_mat)
    row =lax.broadcasted_iota(I32, (r_all, LANES), 0)
    is_ctx = row < rc
    pos = jnp.where(is_ctx, row % lc, (row - rc) % ll)
    seg = jnp.where(is_ctx, lc, ll)
    prev_parts = []
    for d in range(2):
        xr = w[:, (2 * d) * LANES:(2 * d + 1) * LANES]
        xi = w[:, (2 * d + 1) * LANES:(2 * d + 2) * LANES]
        ar, ai = a[2 * d:2 * d + 1], a[2 * d + 1:2 * d + 2]
        first = 0 if d == 0 else seg - 1
        h0r = jnp.zeros((r_all, LANES), F32)
        h0i = jnp.zeros((r_all, LANES), F32)
        for b in range(cfg.lat_b):
            sel = (row >= rc + b * ll) & (row < rc + (b + 1) * ll)
            h0r = jnp.where(sel, h0_ref[pp, b, 2 * d:2 * d + 1, :], h0r)
            h0i = jnp.where(sel, h0_ref[pp, b, 2 * d + 1:2 * d + 2, :], h0i)
        at_first = pos == first
        xr = xr + jnp.where(at_first, ar * h0r - ai * h0i, 0.0)
        xi = xi + jnp.where(at_first, ar * h0i + ai * h0r, 0.0)
        pr, pi = ar, ai
        step = 1
        while step < max(lc, ll):
            if d == 0:
                sr, si = pltpu.roll(xr, step, 0), pltpu.roll(xi, step, 0)
                ok = (pos >= step)
            else:
                sr, si = pltpu.roll(xr, r_all - step, 0), pltpu.roll(xi, r_all - step, 0)
                ok = (pos < seg - step)
            xr, xi = (xr + jnp.where(ok, pr * sr - pi * si, 0.0),
                      xi + jnp.where(ok, pr * si + pi * sr, 0.0))
            pr, pi = pr * pr - pi * pi, 2.0 * pr * pi
            step *= 2
        if d == 0:
            nr, ni = pltpu.roll(xr, 1, 0), pltpu.roll(xi, 1, 0)
        else:
            nr, ni = pltpu.roll(xr, r_all - 1, 0), pltpu.roll(xi, r_all - 1, 0)
        prev_parts += [jnp.where(at_first, h0r, nr), jnp.where(at_first, h0i, ni)]
        xs_ref[2 * d] = xr
        xs_ref[2 * d + 1] = xi
    xin = jnp.concatenate(prev_parts, axis=1).astype(BF16)
    for d in range(2):
        start = lc - 1 if d == 0 else 0
        for k in range(2):
            fin_ref[pp, 2 * d + k] = xs_ref[2 * d + k, pl.ds(start, cfg.ctx_b, stride=lc), :]
    return _dot(u, m_mat) + _dot(xin, g_mat)


def _s5_toeplitz(kf, kb):
    sw = 2 * S5_GROUP
    wide = S5_CHUNK * sw
    lane = lax.broadcasted_iota(I32, (sw, wide), 1)
    blocks = []
    for s in range(S5_CHUNK):
        f = kf if s == 0 else pltpu.roll(kf, sw * s, 1)
        back = (wide - sw * (S5_CHUNK - 1 - s)) % wide
        b = kb if back == 0 else pltpu.roll(kb, back, 1)
        blocks.append(jnp.where(lane >= sw * s, f, 0.0) + jnp.where(lane < sw * (s + 1), b, 0.0))
    return jnp.concatenate(blocks, axis=0).astype(BF16)


def _s5_kernel(cfg, u_ref, w_ref, kf_ref, kb_ref, g_ref, a_ref, h0_ref, y_ref, fin_ref,
               uflat_ref, ystage_ref, xs_ref):
    r_all = cfg.t // S5_CHUNK
    sw = 2 * S5_GROUP
    for s in range(S5_CHUNK):
        blk = u_ref[pl.ds(s, r_all, stride=S5_CHUNK), :]
        for pp in range(S5_QUAD):
            uflat_ref[pp, :, s * sw:(s + 1) * sw] = blk[:, pp * sw:(pp + 1) * sw]
    for pp in range(S5_QUAD):
        y = _s5_pair(cfg, uflat_ref[pp].astype(BF16), w_ref[pp], _s5_toeplitz(kf_ref[pp], kb_ref[pp]),
                     g_ref[pp], a_ref[pp], h0_ref, pp, fin_ref, xs_ref)
        for s in range(S5_CHUNK):
            ystage_ref[s, :, pp * sw:(pp + 1) * sw] = y[:, s * sw:(s + 1) * sw]
    for s in range(S5_CHUNK):
        y_ref[pl.ds(s, r_all, stride=S5_CHUNK), :] = ystage_ref[s]


def s5_mix(cfg, proj, wcat, kc_f, kc_b, gmat, a16, h0):
    r_all = cfg.t // S5_CHUNK
    wide = 2 * S5_CHUNK * S5_GROUP
    mat = pl.BlockSpec((S5_QUAD, wide, wide), lambda q: (q, 0, 0))
    lag = pl.BlockSpec((S5_QUAD, 2 * S5_GROUP, wide), lambda q: (q, 0, 0))
    return pl.pallas_call(
        functools.partial(_s5_kernel, cfg),
        grid=(S5_PAIRS // S5_QUAD,),
        in_specs=[pl.BlockSpec((cfg.t, LANES), lambda q: (0, q)), mat, lag, lag, mat,
                  pl.BlockSpec((S5_QUAD, 4, LANES), lambda q: (q, 0, 0)),
                  pl.BlockSpec((S5_QUAD, cfg.lat_b, 4, LANES), lambda q: (q, 0, 0, 0))],
        out_specs=[pl.BlockSpec((cfg.t, LANES), lambda q: (0, q)),
                   pl.BlockSpec((S5_QUAD, 4, cfg.ctx_b, LANES), lambda q: (q, 0, 0, 0))],
        out_shape=[jax.ShapeDtypeStruct((cfg.t, S5_WIDTH), F32),
                   jax.ShapeDtypeStruct((S5_PAIRS, 4, cfg.ctx_b, LANES), F32)],
        scratch_shapes=[pltpu.VMEM((S5_QUAD, r_all, wide), F32),
                        pltpu.VMEM((S5_CHUNK, r_all, LANES), F32),
                        pltpu.VMEM((4, r_all, LANES), F32)],
        compiler_params=_cp(("parallel",)),
        name="s5_mix",
    )(proj, wcat, kc_f, kc_b, gmat, a16, h0)


def _rope(x, cos, sin_signed, half):
    w = x.shape[-1]
    lane = lax.broadcasted_iota(I32, x.shape, x.ndim - 1)
    swapped = jnp.where((lane % (2 * half)) < half,
                        pltpu.roll(x, w - half, x.ndim - 1), pltpu.roll(x, half, x.ndim - 1))
    return x * cos + swapped * sin_signed


def _attn_kernel(n_parts, latent, n_q_blocks, sink_ref, q_ref, *refs):
    hd = ATT_HEAD_DIM
    scale = hd ** -0.5
    if latent:
        (kp_ref, kc_ref, kn_ref, vp_ref, vc_ref, vn_ref, ck_ref, cv_ref,
         cq_ref, sq_ref, ckp_ref, skp_ref, ckc_ref, skc_ref, ckn_ref, skn_ref, o_ref) = refs
        i = pl.program_id(1)
        q = _rope(q_ref[...], cq_ref[...], sq_ref[...], hd // 2)
        kparts = [_rope(kp_ref[...], ckp_ref[...], skp_ref[...], hd // 2),
                  _rope(kc_ref[...], ckc_ref[...], skc_ref[...], hd // 2),
                  _rope(kn_ref[...], ckn_ref[...], skn_ref[...], hd // 2)]
        kband = jnp.concatenate(kparts, axis=0)
        vband = jnp.concatenate([vp_ref[...], vc_ref[...], vn_ref[...]], axis=0)
        keys = [kband, ck_ref[...]]
        vals = [vband, cv_ref[...]]
        nrow = ATT_GROUP * LANES
        qpos = lax.broadcasted_iota(I32, (nrow, 3 * LANES), 0) % LANES
        kpos = lax.broadcasted_iota(I32, (nrow, 3 * LANES), 1) - LANES
        kabs = kpos + i * LANES
        band_ok = (jnp.abs(kpos - qpos) <= WINDOW) & (kabs >= 0) & (kabs < n_q_blocks * LANES)
        masks = [band_ok, None]
    else:
        k_ref, v_ref, o_ref = refs
        q = q_ref[...]
        keys = [k_ref[...]]
        vals = [v_ref[...]]
        masks = [None]
    for kv in range(ATT_KV_HEADS):
        qs = jnp.concatenate([q[:, (kv * ATT_GROUP + g) * hd:(kv * ATT_GROUP + g + 1) * hd]
                              for g in range(ATT_GROUP)], axis=0).astype(BF16)
        sink_col = jnp.concatenate([jnp.full((LANES, 1), sink_ref[kv * ATT_GROUP + g], F32)
                                    for g in range(ATT_GROUP)], axis=0)
        logits = []
        mx = sink_col
        for kk, msk in zip(keys, masks):
            s = _nt(qs, kk[:, kv * hd:(kv + 1) * hd].astype(BF16)) * scale
            if msk is not None:
                s = jnp.where(msk, s, -1e30)
            logits.append(s)
            mx = jnp.maximum(mx, jnp.max(s, axis=-1, keepdims=True))
        den = jnp.exp(sink_col - mx)
        acc = None
        for s, vv in zip(logits, vals):
            p = jnp.exp(s - mx)
            den = den + jnp.sum(p, axis=-1, keepdims=True)
            o = _dot(p.astype(BF16), vv[:, kv * hd:(kv + 1) * hd].astype(BF16))
            acc = o if acc is None else acc + o
        out = acc / den
        for g in range(ATT_GROUP):
            h = kv * ATT_GROUP + g
            o_ref[:, h * hd:(h + 1) * hd] = out[g * LANES:(g + 1) * LANES, :]


def attention(cfg, proj, sink, cache_k, cache_v, rope_cs):
    qcol = S5_WIDTH // 512
    kcol = (S5_WIDTH + 512) // LANES
    vcol = kcol + 1
    smem = pl.BlockSpec(memory_space=pltpu.SMEM)
    nqc = cfg.ctx_n // LANES
    y_ctx = pl.pallas_call(
        functools.partial(_attn_kernel, 1, False, nqc),
        grid=(cfg.ctx_b, nqc),
        in_specs=[smem,
                  pl.BlockSpec((LANES, 512), lambda b, i: (b * nqc + i, qcol)),
                  pl.BlockSpec((cfg.ctx_n, LANES), lambda b, i: (b, kcol)),
                  pl.BlockSpec((cfg.ctx_n, LANES), lambda b, i: (b, vcol))],
        out_specs=pl.BlockSpec((LANES, 512), lambda b, i: (b * nqc + i, 0)),
        out_shape=jax.ShapeDtypeStruct((cfg.tc, 512), F32),
        compiler_params=_cp(("parallel", "parallel")),
        name="attn_context",
    )(sink, proj, proj, proj)
    nql = cfg.lat_n // LANES
    off = cfg.tc // LANES
    cos_t, sin_t = rope_cs

    def rb(b, i):
        return off + b * nql + i

    def prev(i):
        return jnp.maximum(i - 1, 0)

    def nxt(i):
        return jnp.minimum(i + 1, nql - 1)

    kspec = lambda f, col: pl.BlockSpec((LANES, LANES), lambda b, i: (rb(b, f(i)), col))
    tspec = lambda f: pl.BlockSpec((LANES, LANES), lambda b, i: (f(i), 0))
    same = lambda i: i
    y_lat = pl.pallas_call(
        functools.partial(_attn_kernel, 2, True, nql),
        grid=(cfg.lat_b, nql),
        in_specs=[smem,
                  pl.BlockSpec((LANES, 512), lambda b, i: (rb(b, i), qcol)),
                  kspec(prev, kcol), kspec(same, kcol), kspec(nxt, kcol),
                  kspec(prev, vcol), kspec(same, vcol), kspec(nxt, vcol),
                  pl.BlockSpec((None, cfg.past, LANES), lambda b, i: (b, 0, 0)),
                  pl.BlockSpec((None, cfg.past, LANES), lambda b, i: (b, 0, 0)),
                  pl.BlockSpec((LANES, 512), lambda b, i: (i, 0)),
                  pl.BlockSpec((LANES, 512), lambda b, i: (i, 0)),
                  tspec(prev), tspec(prev), tspec(same), tspec(same), tspec(nxt), tspec(nxt)],
        out_specs=pl.BlockSpec((LANES, 512), lambda b, i: (b * nql + i, 0)),
        out_shape=jax.ShapeDtypeStruct((cfg.lat_b * cfg.lat_n, 512), F32),
        compiler_params=_cp(("parallel", "parallel")),
        name="attn_latent",
    )(sink, proj, proj, proj, proj, proj, proj, proj, cache_k, cache_v,
      cos_t, sin_t, cos_t, sin_t, cos_t, sin_t, cos_t, sin_t)
    return jnp.concatenate([y_ctx, y_lat], axis=0)


def _rope_tables(n, head, width):
    rows = jnp.repeat(jnp.arange(n // GRID_W, dtype=F32), GRID_W)
    cols = (jnp.arange(n) % GRID_W).astype(F32)
    nf = head // 4
    inv = ROPE_BASE ** (-jnp.arange(nf, dtype=F32) / nf)
    ang = jnp.concatenate([rows[:, None] * inv, cols[:, None] * inv], axis=-1)
    c, s = jnp.cos(ang), jnp.sin(ang)
    cos_h = jnp.concatenate([c, c], axis=-1)
    sin_h = jnp.concatenate([-s, s], axis=-1)
    reps = width // head
    return jnp.tile(cos_h, (1, reps)), jnp.tile(sin_h, (1, reps))


def _even_out_kernel(x_ref, u_ref, yd_ref, ya_ref, d_ref, glu_ref, wo_ref, g_ref, o_ref):
    y = jax.nn.gelu(u_ref[...] * d_ref[...] + yd_ref[...])
    z = y * jax.nn.sigmoid(_dot(y.astype(BF16), glu_ref[...]))
    cat = jnp.concatenate([z, ya_ref[...]], axis=-1).astype(BF16)
    o_ref[...] = x_ref[...] + g_ref[...] * _dot(cat, wo_ref[...])


def even_out(cfg, x, proj, yd, yatt, s5_d, glu_bf16, wo_bf16, mod5, layer):
    tile = lambda w: pl.BlockSpec((ROW_TILE, w), lambda i: (i, 0))
    return pl.pallas_call(
        _even_out_kernel,
        grid=(cfg.t // ROW_TILE,),
        in_specs=[tile(D_MODEL), tile(S5_WIDTH), tile(S5_WIDTH), tile(512),
                  pl.BlockSpec((1, S5_WIDTH), lambda i: (0, 0)),
                  pl.BlockSpec((S5_WIDTH, S5_WIDTH), lambda i: (0, 0)),
                  pl.BlockSpec((D_MODEL, D_MODEL), lambda i: (0, 0)),
                  _mod_spec(cfg, layer, 2)],
        out_specs=tile(D_MODEL),
        out_shape=jax.ShapeDtypeStruct((cfg.t, D_MODEL), F32),
        compiler_params=_cp(("parallel",)),
        name="even_out",
    )(x, proj, yd, yatt, s5_d.reshape(1, S5_WIDTH), glu_bf16, wo_bf16, mod5)


REC_CHUNK = 128
REC_LEVELS = 7


def _rec_tables():
    c = REC_CHUNK
    t = np.arange(c)[:, None]
    u = np.arange(c)[None, :]
    out = np.zeros((2, (REC_LEVELS + 2) * c, c), np.float32)
    for d in range(2):
        for lv in range(REC_LEVELS):
            h = 1 << lv
            mid = (t // (2 * h)) * (2 * h) + h
            second = (t % (2 * h)) >= h
            if d == 0:
                a = np.where(second, (u >= mid) & (u <= t), (u > t) & (u < mid))
            else:
                a = np.where(second, (u >= mid) & (u < t), (u >= t) & (u < mid))
            out[d, lv * c:(lv + 1) * c] = a
        if d == 0:
            out[d, REC_LEVELS * c:(REC_LEVELS + 1) * c] = u <= t
            out[d, (REC_LEVELS + 1) * c:] = u > t
        else:
            out[d, REC_LEVELS * c:(REC_LEVELS + 1) * c] = u >= t
            out[d, (REC_LEVELS + 1) * c:] = u < t
    return out


def _rec_kernel(cfg, layer_o, *refs):
    fwd_in, bwd_in = refs[0:6], refs[6:12]
    amat_ref, lbl_ref, rdec_ref = refs[12:15]
    rope_f, rope_b = refs[15:17], refs[17:19]
    s0_f, s0_b, o_f, o_b, st_f, st_b = refs[19:25]
    _rec_direction(cfg, layer_o, 0, *fwd_in, amat_ref, lbl_ref, rdec_ref, *rope_f, s0_f, o_f, st_f)
    _rec_direction(cfg, layer_o, 1, *bwd_in, amat_ref, lbl_ref, rdec_ref, *rope_b, s0_b, o_b, st_b)


def _rec_direction(cfg, layer_o, d, q_ref, z_ref, v_ref, rq_ref, rk_ref, rv_ref,
                   amat_ref, lbl_ref, rdec_ref, cos_ref, sin_ref, s0_ref, o_ref, st_ref):
    c = REC_CHUNK
    step = pl.program_id(0)
    n_chunks = cfg.t // c
    chunk = step if d == 0 else n_chunks - 1 - step
    cpc = cfg.ctx_n // c
    cpl = cfg.lat_n // c
    n_ctx_chunks = cfg.tc // c
    is_lat = chunk >= n_ctx_chunks
    pos = jnp.where(is_lat, (chunk - n_ctx_chunks) % cpl, chunk % cpc)
    last = jnp.where(is_lat, cpl - 1, cpc - 1)
    first_processed = (pos == 0) if d == 0 else (pos == last)

    @pl.when(first_processed & jnp.logical_not(is_lat))
    def _():
        st_ref[...] = jnp.zeros_like(st_ref)

    @pl.when(first_processed & is_lat)
    def _():
        st_ref[...] = s0_ref[...]

    row = lax.broadcasted_iota(I32, (c, c), 0)
    col = lax.broadcasted_iota(I32, (c, c), 1)
    fwd = d == 0
    amat = amat_ref[d]

    def mix_head(q, k, v, scores, e_q, e_k, hidx):
        o = _dot(scores.astype(BF16), v.astype(BF16))
        st = st_ref[hidx]
        o = o + _nt((q * e_q).astype(BF16), st.astype(BF16))
        kd = (k * e_k).astype(BF16)
        total = jnp.where(fwd, e_q[c - 1:c], e_q[0:1])
        st_ref[hidx] = st * total + _dot(v.T.astype(BF16), kd)
        return o

    def hier_scores(q, k, e):
        scores = jnp.where(row == col, _nt(q.astype(BF16), k.astype(BF16)), 0.0)
        for lv in range(REC_LEVELS):
            h = 1 << lv
            el = e[lv * c:(lv + 1) * c]
            second = (row % (2 * h)) >= h
            is_q = second == fwd
            qt = jnp.where(is_q, q * el, 0.0).astype(BF16)
            kt = jnp.where(is_q, 0.0, k * el).astype(BF16)
            same = (row // (2 * h)) == (col // (2 * h))
            scores = scores + jnp.where(same, _nt(qt, kt), 0.0)
        return scores

    lbl = lbl_ref[...]
    mx = jnp.max(lbl, axis=0, keepdims=True)
    ex = jnp.exp(lbl - mx)
    sm = ex / jnp.sum(ex, axis=0, keepdims=True)
    lb_all = jnp.zeros_like(sm[0])
    for i in range(1, layer_o + 1):
        lb_all = lb_all + sm[i]
    lb = jnp.where(fwd, lb_all[0:1], lb_all[1:2])
    z = z_ref[...]
    log_sig = jnp.minimum(z, 0.0) - jnp.log1p(jnp.exp(-jnp.abs(z)))
    a_ = jnp.log1p(-lb) + log_sig
    b_ = jnp.log(lb)
    big = jnp.maximum(a_, b_)
    logf = big + jnp.log1p(jnp.exp(-jnp.abs(a_ - b_)))
    kh = (1.0 - lb) * jax.nn.sigmoid(-z)
    l1 = logf.astype(BF16)
    r1 = logf - l1.astype(F32)
    l2 = r1.astype(BF16)
    l3 = (r1 - l2.astype(F32)).astype(BF16)
    wide = REC_HEADS * HEAD_DIM
    dsum_all = _dot(amat, jnp.concatenate([l1, l2, l3], axis=1))
    dsum_all = dsum_all[:, :wide] + dsum_all[:, wide:2 * wide] + dsum_all[:, 2 * wide:]
    qh, vh = q_ref[...], v_ref[...]
    outs = []
    for h in range(REC_HEADS):
        sl = slice(h * HEAD_DIM, (h + 1) * HEAD_DIM)
        e = jnp.exp(dsum_all[:, sl])
        q, k = qh[:, sl], kh[:, sl]
        outs.append(mix_head(q, k, vh[:, sl], hier_scores(q, k, e),
                             e[REC_LEVELS * c:(REC_LEVELS + 1) * c], e[(REC_LEVELS + 1) * c:], h))
    rq, rk = rq_ref[...], rk_ref[...]
    rq_rot = _rope(rq, cos_ref[...], sin_ref[...], HEAD_DIM // 2)
    rk_rot = _rope(rk, cos_ref[...], sin_ref[...], HEAD_DIM // 2)
    rq = jnp.where(is_lat, rq_rot, rq)
    rk = jnp.where(is_lat, rk_rot, rk) * (HEAD_DIM ** -0.5)
    rv = rv_ref[...]
    gam = jnp.exp(jnp.where(fwd, rdec_ref[0:1], rdec_ref[1:2]))
    lag = jnp.where(fwd, row - col, col - row)
    lag_f = jnp.maximum(lag, 0).astype(F32)
    steps_q = jnp.where(fwd, row + 1, c - row).astype(F32)
    steps_k = jnp.where(fwd, c - 1 - row, row).astype(F32)
    for h in range(REC_HEADS):
        sl = slice(h * HEAD_DIM, (h + 1) * HEAD_DIM)
        g = gam[:, sl]
        q, k = rq[:, sl], rk[:, sl]
        scores = jnp.where(lag >= 0, _nt(q.astype(BF16), k.astype(BF16)) * jnp.exp(-g * lag_f), 0.0)
        outs.append(mix_head(q, k, rv[:, sl], scores, jnp.exp(-g * steps_q), jnp.exp(-g * steps_k),
                             REC_HEADS + h))
    o_ref[...] = jnp.concatenate(outs, axis=1)


def recurrences(cfg, layer_o, proj, amat, lb_logits, ret_decay_lanes, rope_cs, s0):
    c = REC_CHUNK
    n_chunks = cfg.t // c
    n_ctx_chunks = cfg.tc // c
    cpc, cpl = cfg.ctx_n // c, cfg.lat_n // c
    n_seq = cfg.ctx_b + cfg.lat_b

    def chunk_of(d, s):
        return s if d == 0 else n_chunks - 1 - s

    def seq_of(d, s):
        ch = chunk_of(d, s)
        return jnp.where(ch >= n_ctx_chunks, cfg.ctx_b + (ch - n_ctx_chunks) // cpl, ch // cpc)

    def lat_of(d, s):
        return jnp.maximum(seq_of(d, s) - cfg.ctx_b, 0)

    def lat_pos(d, s):
        ch = chunk_of(d, s)
        return jnp.where(ch >= n_ctx_chunks, (ch - n_ctx_chunks) % cpl, 0)

    def col(d, k):
        return pl.BlockSpec((c, 512), lambda s: (chunk_of(d, s), k))

    def cols(d):
        return [col(d, 0), col(d, 1 + d), col(d, 3), col(d, 5), col(d, 6), col(d, 7)]

    def rope(d):
        return [pl.BlockSpec((c, 512), lambda s: (lat_pos(d, s), 0))] * 2

    def state_in(d):
        return pl.BlockSpec((None, None, 2 * REC_HEADS, HEAD_DIM, HEAD_DIM), lambda s: (d, lat_of(d, s), 0, 0, 0))

    def state_out(d):
        return pl.BlockSpec((None, 2 * REC_HEADS, HEAD_DIM, HEAD_DIM), lambda s: (seq_of(d, s), 0, 0, 0))

    nrow = (REC_LEVELS + 2) * c
    cos_t, sin_t = rope_cs
    o_shape = jax.ShapeDtypeStruct((cfg.t, D_MODEL), F32)
    st_shape = jax.ShapeDtypeStruct((n_seq, 2 * REC_HEADS, HEAD_DIM, HEAD_DIM), F32)
    return pl.pallas_call(
        functools.partial(_rec_kernel, cfg, layer_o),
        grid=(n_chunks,),
        in_specs=cols(0) + cols(1) + [
            pl.BlockSpec((2, nrow, c), lambda s: (0, 0, 0)),
            pl.BlockSpec((N_ODD, 2, 512), lambda s: (0, 0, 0)),
            pl.BlockSpec((None, 2, 512), lambda s: (0, 0, 0))] + rope(0) + rope(1) + [state_in(0), state_in(1)],
        out_specs=[pl.BlockSpec((c, D_MODEL), lambda s: (chunk_of(0, s), 0)),
                   pl.BlockSpec((c, D_MODEL), lambda s: (chunk_of(1, s), 0)),
                   state_out(0), state_out(1)],
        out_shape=[o_shape, o_shape, st_shape, st_shape],
        compiler_params=_cp(("arbitrary",)),
        name="recurrences",
    )(*([proj] * 12), amat, lb_logits, ret_decay_lanes, cos_t, sin_t, cos_t, sin_t, s0, s0)


def _odd_out_kernel(x_ref, of_ref, ob_ref, cg_ref, rg_ref, hw_ref, rw_ref, wo_ref, g_ref, o_ref):
    o = of_ref[...] + ob_ref[...]
    hw, rw = hw_ref[...], rw_ref[...]
    parts = []
    for h in range(REC_HEADS):
        sl = slice(h * HEAD_DIM, (h + 1) * HEAD_DIM)
        oh = o[:, sl]
        parts.append(oh * lax.rsqrt(jnp.mean(oh * oh, -1, keepdims=True) + EPS) * hw[:, sl])
    oc = jnp.concatenate(parts, axis=1) * jax.nn.sigmoid(cg_ref[...])
    parts = []
    for h in range(REC_HEADS):
        sl = slice(h * HEAD_DIM, (h + 1) * HEAD_DIM)
        oh = o[:, 512 + h * HEAD_DIM:512 + (h + 1) * HEAD_DIM]
        oh = oh - jnp.mean(oh, -1, keepdims=True)
        parts.append(oh * lax.rsqrt(jnp.mean(oh * oh, -1, keepdims=True) + EPS) * rw[:, sl])
    rg = rg_ref[...]
    orr = jnp.concatenate(parts, axis=1) * (rg * jax.nn.sigmoid(rg))
    cat = jnp.concatenate([oc, orr], axis=1).astype(BF16)
    o_ref[...] = x_ref[...] + g_ref[...] * _dot(cat, wo_ref[...])


def odd_out(cfg, x, o_f, o_b, proj, hgrn_norm_w, ret_norm_w, wo_bf16, mod5, layer):
    tile = lambda w, k=0: pl.BlockSpec((ROW_TILE, w), lambda i: (i, k))
    return pl.pallas_call(
        _odd_out_kernel,
        grid=(cfg.t // ROW_TILE,),
        in_specs=[tile(D_MODEL), tile(D_MODEL), tile(D_MODEL),
                  tile(512, 4), tile(512, 8),
                  pl.BlockSpec((1, 512), lambda i: (0, 0)), pl.BlockSpec((1, 512), lambda i: (0, 0)),
                  pl.BlockSpec((D_MODEL, D_MODEL), lambda i: (0, 0)),
                  _mod_spec(cfg, layer, 2)],
        out_specs=tile(D_MODEL),
        out_shape=jax.ShapeDtypeStruct((cfg.t, D_MODEL), F32),
        compiler_params=_cp(("parallel",)),
        name="odd_out",
    )(x, o_f, o_b, proj, proj, hgrn_norm_w.reshape(1, 512), ret_norm_w.reshape(1, 512), wo_bf16, mod5)


def _norm_logits_kernel(x_ref, nw_ref, sh_ref, sc_ref, wr_ref, h_ref, lg_ref):
    h = (_rms(x_ref[...], nw_ref[...]) * (1.0 + sc_ref[...]) + sh_ref[...]).astype(BF16)
    h_ref[...] = h
    lg_ref[...] = _nt(wr_ref[...], h)


def moe_norm_logits(cfg, x, nw, mod5, layer, wr_t_bf16):
    return pl.pallas_call(
        _norm_logits_kernel,
        grid=(cfg.t // ROW_TILE,),
        in_specs=[pl.BlockSpec((ROW_TILE, D_MODEL), lambda i: (i, 0)),
                  pl.BlockSpec((1, D_MODEL), lambda i: (0, 0)),
                  _mod_spec(cfg, layer, 3), _mod_spec(cfg, layer, 4),
                  pl.BlockSpec((N_EXPERTS, D_MODEL), lambda i: (0, 0))],
        out_specs=[pl.BlockSpec((ROW_TILE, D_MODEL), lambda i: (i, 0)),
                   pl.BlockSpec((N_EXPERTS, ROW_TILE), lambda i: (0, i))],
        out_shape=[jax.ShapeDtypeStruct((cfg.t, D_MODEL), BF16),
                   jax.ShapeDtypeStruct((N_EXPERTS, cfg.t), F32)],
        compiler_params=_cp(("parallel",)),
        name="moe_norm_logits",
    )(x, nw.reshape(1, D_MODEL), mod5, mod5, wr_t_bf16)


def _route_kernel(cfg, lg_ref, pos_ref, aff_ref, cj_ref):
    gs = cfg.gs
    g = pl.program_id(0)
    logits = lg_ref[...]
    mx = jnp.max(logits, axis=0, keepdims=True)
    ex = jnp.exp(logits - mx)
    aff = ex / jnp.sum(ex, axis=0, keepdims=True)
    aff_ref[...] = aff

    def seg_sum(x, seg):
        parts = []
        for r in range(gs // seg):
            s = jnp.sum(x[:, r * seg:(r + 1) * seg], axis=1, keepdims=True)
            parts.append(jnp.broadcast_to(s, (N_EXPERTS, seg)))
        return parts[0] if len(parts) == 1 else jnp.concatenate(parts, axis=1)

    tri = (lax.broadcasted_iota(I32, (LANES, LANES), 0) < lax.broadcasted_iota(I32, (LANES, LANES), 1)).astype(BF16)
    lane = lax.broadcasted_iota(I32, (N_EXPERTS, LANES), 1)

    def prefix(mask, seg):
        parts = []
        carry = jnp.zeros((N_EXPERTS, 1), F32)
        starts = jnp.zeros((N_EXPERTS, LANES), F32)
        for j in range(gs // LANES):
            if (j * LANES) % seg == 0:
                carry = jnp.zeros((N_EXPERTS, 1), F32)
            m = mask[:, j * LANES:(j + 1) * LANES]
            starts = starts + jnp.where(lane == j, carry, 0.0)
            parts.append(_dot(m.astype(BF16), tri) + carry)
            carry = carry + jnp.sum(m, axis=1, keepdims=True)
        starts = starts + jnp.where(lane == gs // LANES, carry, 0.0)
        return jnp.concatenate(parts, axis=1), starts

    def select(seg, cap):
        def at_least_cap(mid):
            return seg_sum((aff >= mid).astype(F32), seg) >= cap

        def halve_bits(_, lohi):
            lo, hi = lohi
            mid = lo + ((hi - lo) >> 1)
            ok = at_least_cap(pltpu.bitcast(mid, F32))
            return jnp.where(ok, mid, lo), jnp.where(ok, hi, mid)

        def halve_value(_, lohi):
            lo, hi = lohi
            mid = 0.5 * (lo + hi)
            ok = at_least_cap(mid)
            return jnp.where(ok, mid, lo), jnp.where(ok, hi, mid)

        lo0 = jnp.zeros((N_EXPERTS, gs), I32)
        hi0 = jnp.full((N_EXPERTS, gs), F32_INF_BITS, I32)
        lo_b, hi_b = lax.fori_loop(0, 31, halve_bits, (lo0, hi0))
        lo_f = jnp.where(lo_b < F32_MIN_NORMAL_BITS, 0.0, pltpu.bitcast(lo_b, F32))
        lo_f, hi_f = lax.fori_loop(0, 30, halve_value, (lo_f, pltpu.bitcast(hi_b, F32)))
        above = aff >= hi_f
        tied = jnp.logical_and(aff >= lo_f, aff < hi_f)
        need = cap - seg_sum(above.astype(F32), seg)
        before, _ = prefix(tied.astype(F32), seg)
        return jnp.logical_or(above, jnp.logical_and(tied, before < need))

    def finish(sel):
        self_f = sel.astype(F32)
        slot, starts = prefix(self_f, gs)
        pos_ref[...] = jnp.where(sel, slot.astype(I32), -1)
        cj_ref[...] = starts.astype(I32)

    @pl.when(g == 0)
    def _():
        finish(select(cfg.ctx_n, CAPACITY_FACTOR * cfg.ctx_n // N_EXPERTS))

    @pl.when(g > 0)
    def _():
        finish(select(cfg.lat_n, CAPACITY_FACTOR * cfg.lat_n // N_EXPERTS))


def moe_route(cfg, logits):
    gs = cfg.gs
    return pl.pallas_call(
        functools.partial(_route_kernel, cfg),
        grid=(cfg.groups,),
        in_specs=[pl.BlockSpec((N_EXPERTS, gs), lambda g: (0, g))],
        out_specs=[pl.BlockSpec((None, N_EXPERTS, gs), lambda g: (g, 0, 0)),
                   pl.BlockSpec((None, N_EXPERTS, gs), lambda g: (g, 0, 0)),
                   pl.BlockSpec((None, N_EXPERTS, LANES), lambda g: (g, 0, 0))],
        out_shape=[jax.ShapeDtypeStruct((cfg.groups, N_EXPERTS, gs), I32),
                   jax.ShapeDtypeStruct((cfg.groups, N_EXPERTS, gs), F32),
                   jax.ShapeDtypeStruct((cfg.groups, N_EXPERTS, LANES), I32)],
        compiler_params=_cp(("parallel",)),
        name="moe_route",
    )(logits)


def _slot_block(cfg):
    return min(LANES, cfg.slots)


GATHER_TOKENS = 2 * LANES


def _gather_kernel(cfg, cj_ref, h_ref, pos_ref, aff_ref, x_ref, gate_ref, acc_ref, gacc_ref):
    g, e = pl.program_id(0), pl.program_id(1)
    sb = _slot_block(cfg)
    tw = GATHER_TOKENS
    acc_ref[...] = jnp.zeros_like(acc_ref)
    gacc_ref[...] = jnp.zeros_like(gacc_ref)
    base = (g * N_EXPERTS + e) * LANES
    sub = lax.broadcasted_iota(I32, (sb, tw), 0)
    for tb in range(cfg.gs // tw):
        s0, s1 = cj_ref[base + tb * (tw // LANES)], cj_ref[base + (tb + 1) * (tw // LANES)]
        for jb in range(cfg.slots // sb):
            @pl.when((s1 > s0) & (s0 < (jb + 1) * sb) & (s1 > jb * sb))
            def _():
                p = pos_ref[:, tb * tw:(tb + 1) * tw]
                hit = jnp.broadcast_to(p, (sb, tw)) == (sub + jb * sb)
                acc_ref[jb * sb:(jb + 1) * sb, :] += _dot(hit.astype(BF16), h_ref[tb * tw:(tb + 1) * tw, :])
                a = aff_ref[:, tb * tw:(tb + 1) * tw]
                gacc_ref[jb * sb:(jb + 1) * sb, :] += jnp.sum(jnp.where(hit, a, 0.0), axis=1, keepdims=True)
    x_ref[...] = acc_ref[...].astype(BF16)
    gate_ref[...] = gacc_ref[...]


def moe_gather(cfg, cj, h, pos, aff):
    gs, slots = cfg.gs, cfg.slots
    pos4 = pos.reshape(cfg.groups, N_EXPERTS, 1, gs)
    aff4 = aff.reshape(cfg.groups, N_EXPERTS, 1, gs)
    grid_spec = pltpu.PrefetchScalarGridSpec(
        num_scalar_prefetch=1,
        grid=(cfg.groups, N_EXPERTS),
        in_specs=[pl.BlockSpec((gs, D_MODEL), lambda g, e, cj: (g, 0)),
                  pl.BlockSpec((None, None, 1, gs), lambda g, e, cj: (g, e, 0, 0)),
                  pl.BlockSpec((None, None, 1, gs), lambda g, e, cj: (g, e, 0, 0))],
        out_specs=[pl.BlockSpec((None, slots, D_MODEL), lambda g, e, cj: (e, g, 0)),
                   pl.BlockSpec((None, slots, 1), lambda g, e, cj: (e, g, 0))],
        scratch_shapes=[pltpu.VMEM((slots, D_MODEL), F32), pltpu.VMEM((slots, 1), F32)])
    return pl.pallas_call(
        functools.partial(_gather_kernel, cfg),
        grid_spec=grid_spec,
        out_shape=[jax.ShapeDtypeStruct((N_EXPERTS, cfg.groups * slots, D_MODEL), BF16),
                   jax.ShapeDtypeStruct((N_EXPERTS, cfg.groups * slots, 1), F32)],
        compiler_params=_cp(("parallel", "arbitrary")),
        name="moe_gather",
    )(cj.reshape(-1), h, pos4, aff4)


def _ffn_kernel(rows, x_ref, gate_ref, w1_ref, w3_ref, w2_ref, y_ref, acc_ref, w1b, w3b, w2b):
    c = pl.program_id(1)
    w1b[...] = w1_ref[...].astype(BF16)
    w3b[...] = w3_ref[...].astype(BF16)
    w2b[...] = w2_ref[...].astype(BF16)
    rt = min(FFN_ROWS, rows)

    def body(i, _):
        r = pl.multiple_of(i * rt, rt)
        x = x_ref[pl.ds(r, rt), :]
        a = _dot(x, w1b[...])
        b = _dot(x, w3b[...])
        hid = (a * jax.nn.sigmoid(a) * b).astype(BF16)
        y = _dot(hid, w2b[...])

        @pl.when(c == 0)
        def _():
            acc_ref[pl.ds(r, rt), :] = y

        @pl.when(c > 0)
        def _():
            acc_ref[pl.ds(r, rt), :] += y

        return 0

    lax.fori_loop(0, rows // rt, body, 0)

    @pl.when(c == pl.num_programs(1) - 1)
    def _():
        y_ref[...] = (acc_ref[...] * gate_ref[...]).astype(BF16)


def moe_ffn(cfg, layer, xg, gate, w1, w3, w2):
    rows = cfg.groups * cfg.slots
    nff = EXPERT_FF // FF_TILE
    return pl.pallas_call(
        functools.partial(_ffn_kernel, rows),
        grid=(N_EXPERTS, nff),
        in_specs=[pl.BlockSpec((None, rows, D_MODEL), lambda e, c: (e, 0, 0)),
                  pl.BlockSpec((None, rows, 1), lambda e, c: (e, 0, 0)),
                  pl.BlockSpec((None, None, D_MODEL, FF_TILE), lambda e, c: (layer, e, 0, c)),
                  pl.BlockSpec((None, None, D_MODEL, FF_TILE), lambda e, c: (layer, e, 0, c)),
                  pl.BlockSpec((None, None, FF_TILE, D_MODEL), lambda e, c: (layer, e, c, 0))],
        out_specs=pl.BlockSpec((None, rows, D_MODEL), lambda e, c: (e, 0, 0)),
        out_shape=jax.ShapeDtypeStruct((N_EXPERTS, rows, D_MODEL), BF16),
        scratch_shapes=[pltpu.VMEM((rows, D_MODEL), F32),
                        pltpu.VMEM((D_MODEL, FF_TILE), BF16), pltpu.VMEM((D_MODEL, FF_TILE), BF16),
                        pltpu.VMEM((FF_TILE, D_MODEL), BF16)],
        compiler_params=_cp(("parallel", "arbitrary")),
        name="moe_ffn",
    )(xg, gate, w1, w3, w2)


def _scatter_kernel(cfg, final, cj_ref, x_ref, pos_ref, y_ref, g_ref, fw_ref, o_ref, acc_ref):
    g, tb = pl.program_id(0), pl.program_id(1)
    sb = _slot_block(cfg)
    acc_ref[...] = jnp.zeros_like(acc_ref)
    sub = lax.broadcasted_iota(I32, (sb, LANES), 0)
    for e in range(N_EXPERTS):
        base = (g * N_EXPERTS + e) * LANES
        s0, s1 = cj_ref[base + tb], cj_ref[base + tb + 1]
        for jb in range(cfg.slots // sb):
            @pl.when((s1 > s0) & (s0 < (jb + 1) * sb) & (s1 > jb * sb))
            def _():
                hit = jnp.broadcast_to(pos_ref[e:e + 1, :], (sb, LANES)) == (sub + jb * sb)
                hit_t = jnp.where(hit, 1.0, 0.0).T.astype(BF16)
                acc_ref[...] += _dot(hit_t, y_ref[e, jb * sb:(jb + 1) * sb, :])
    out = x_ref[...] + g_ref[...] * acc_ref[...]
    if final:
        out = _rms(out, fw_ref[...])
    o_ref[...] = out


def moe_scatter(cfg, layer, cj, x, pos, y, mod5, final_w, final):
    gs, slots = cfg.gs, cfg.slots
    nt = gs // LANES
    grid_spec = pltpu.PrefetchScalarGridSpec(
        num_scalar_prefetch=1,
        grid=(cfg.groups, nt),
        in_specs=[pl.BlockSpec((LANES, D_MODEL), lambda g, tb, cj: (g * nt + tb, 0)),
                  pl.BlockSpec((None, N_EXPERTS, LANES), lambda g, tb, cj: (g, 0, tb)),
                  pl.BlockSpec((N_EXPERTS, slots, D_MODEL), lambda g, tb, cj: (0, g, 0)),
                  pl.BlockSpec((None, None, None, 1, D_MODEL), lambda g, tb, cj: (layer, g, 5, 0, 0)),
                  pl.BlockSpec((1, D_MODEL), lambda g, tb, cj: (0, 0))],
        out_specs=pl.BlockSpec((LANES, D_MODEL), lambda g, tb, cj: (g * nt + tb, 0)),
        scratch_shapes=[pltpu.VMEM((LANES, D_MODEL), F32)])
    return pl.pallas_call(
        functools.partial(_scatter_kernel, cfg, final),
        grid_spec=grid_spec,
        out_shape=jax.ShapeDtypeStruct((cfg.t, D_MODEL), F32),
        compiler_params=_cp(("parallel", "arbitrary")),
        name="moe_scatter",
    )(cj.reshape(-1), x, pos, y, mod5, final_w.reshape(1, D_MODEL))


def _forward(cfg, x_prompt, x_sample, state_s5, cache_k, cache_v, state_hgrn, state_ret, c, c_ctx, p):
    assert cfg.tc == cfg.lat_n and cfg.ctx_n % ROW_TILE == 0 and cfg.lat_n % (GRID_W * 2) == 0
    x = jnp.concatenate([x_prompt.reshape(cfg.tc, D_MODEL), x_sample.reshape(-1, D_MODEL)], axis=0)
    conds = jnp.zeros((8, D_MODEL), F32).at[0].set(c_ctx).at[1:1 + cfg.lat_b].set(c)
    mod = adaln_table(conds, p['mod_w'], p['mod_b'])
    mod5 = mod[:, :cfg.groups].reshape(DEPTH, cfg.groups, 6, 1, D_MODEL)
    rope_att = _rope_tables(cfg.lat_n, ATT_HEAD_DIM, 512)
    rope_ret = _rope_tables(cfg.lat_n, HEAD_DIM, 512)
    amat = jnp.asarray(_rec_tables(), BF16)
    s5_fin, k_out, v_out, hgrn_fin, ret_fin = [], [], [], [], []
    for layer in range(DEPTH):
        if layer % 2 == 0:
            e = layer // 2
            proj = norm_mod_matmul(cfg, x, p['norm1_w'][layer], mod5, layer, p['ab_w_in'][e].astype(BF16))
            s5_tabs = _s5_tables(p['s5_lambda_re'][e], p['s5_lambda_im'][e], p['s5_log_dt'][e],
                                 p['s5_b_re'][e], p['s5_b_im'][e], p['s5_c_re'][e], p['s5_c_im'][e])
            h0 = state_s5[:, e].reshape(cfg.lat_b, 4, S5_PAIRS, 2 * S5_STATE).transpose(2, 0, 1, 3)
            yd, fin = s5_mix(cfg, proj, *s5_tabs, h0)
            s5_fin.append(fin.transpose(2, 1, 0, 3).reshape(cfg.ctx_b, 2, 2, S5_GROUPS, S5_STATE))
            ck = cache_k[:, e].reshape(cfg.lat_b, cfg.past, LANES)
            cv = cache_v[:, e].reshape(cfg.lat_b, cfg.past, LANES)
            yatt = attention(cfg, proj, p['attn_sink'][e], ck, cv, rope_att)
            k_out.append(proj[:cfg.tc, 1024:1152].reshape(cfg.ctx_b, cfg.ctx_n, ATT_KV_HEADS, ATT_HEAD_DIM))
            v_out.append(proj[:cfg.tc, 1152:1280].reshape(cfg.ctx_b, cfg.ctx_n, ATT_KV_HEADS, ATT_HEAD_DIM))
            x = even_out(cfg, x, proj, yd, yatt, p['s5_d'][e], p['s5_glu_w'][e].astype(BF16),
                         p['ab_w_out'][e].astype(BF16), mod5, layer)
        else:
            o = layer // 2
            proj = norm_mod_matmul(cfg, x, p['norm1_w'][layer], mod5, layer, p['cd_w_in'][o].astype(BF16))
            s0 = jnp.concatenate([state_hgrn[:, o], state_ret[:, o]], axis=2).transpose(1, 0, 2, 4, 3)
            rdec = jnp.repeat(p['ret_decay'][o], HEAD_DIM, axis=-1)
            o_f, o_b, st_f, st_b = recurrences(cfg, o, proj, amat, p['hgrn_lb_logits'], rdec[None], rope_ret, s0)
            st = jnp.stack([st_f[:cfg.ctx_b], st_b[:cfg.ctx_b]], axis=1).transpose(0, 1, 2, 4, 3)
            hgrn_fin.append(st[:, :, :REC_HEADS])
            ret_fin.append(st[:, :, REC_HEADS:])
            x = odd_out(cfg, x, o_f, o_b, proj, p['hgrn_norm_w'][o], p['ret_norm_w'][o],
                        p['cd_w_out'][o].astype(BF16), mod5, layer)
        h2, logits = moe_norm_logits(cfg, x, p['norm2_w'][layer], mod5, layer,
                                     p['router_w'][layer].T.astype(BF16))
        pos, aff, cj = moe_route(cfg, logits)
        xg, gate = moe_gather(cfg, cj, h2, pos, aff)
        yg = moe_ffn(cfg, layer, xg, gate, p['moe_w1'], p['moe_w3'], p['moe_w2'])
        x = moe_scatter(cfg, layer, cj, x, pos, yg, mod5, p['final_norm_w'], layer == DEPTH - 1)
    y_prompt = x[:cfg.tc].reshape(cfg.ctx_b, cfg.ctx_n, D_MODEL)
    y_sample = x[cfg.tc:].reshape(cfg.lat_b, cfg.lat_n, D_MODEL)
    return (y_prompt, y_sample, jnp.stack(s5_fin, axis=1), jnp.stack(k_out, axis=1), jnp.stack(v_out, axis=1),
            jnp.stack(hgrn_fin, axis=1), jnp.stack(ret_fin, axis=1))


def kernel(x_prompt, x_sample, state_s5, cache_k, cache_v, state_hgrn, state_ret, c, c_ctx, mod_w, mod_b, norm1_w, norm2_w, final_norm_w, ab_w_in, ab_w_out, s5_lambda_re, s5_lambda_im, s5_log_dt, s5_b_re, s5_b_im, s5_c_re, s5_c_im, s5_d, s5_glu_w, attn_sink, cd_w_in, cd_w_out, hgrn_lb_logits, hgrn_norm_w, ret_decay, ret_norm_w, router_w, moe_w1, moe_w3, moe_w2):
    cfg = Cfg(ctx_b=x_prompt.shape[0], ctx_n=x_prompt.shape[1], lat_b=x_sample.shape[0],
              lat_n=x_sample.shape[1], past=cache_k.shape[2])
    p = dict(mod_w=mod_w, mod_b=mod_b, norm1_w=norm1_w, norm2_w=norm2_w, final_norm_w=final_norm_w,
             ab_w_in=ab_w_in, ab_w_out=ab_w_out, s5_lambda_re=s5_lambda_re, s5_lambda_im=s5_lambda_im,
             s5_log_dt=s5_log_dt, s5_b_re=s5_b_re, s5_b_im=s5_b_im, s5_c_re=s5_c_re, s5_c_im=s5_c_im,
             s5_d=s5_d, s5_glu_w=s5_glu_w, attn_sink=attn_sink, cd_w_in=cd_w_in, cd_w_out=cd_w_out,
             hgrn_lb_logits=hgrn_lb_logits, hgrn_norm_w=hgrn_norm_w, ret_decay=ret_decay,
             ret_norm_w=ret_norm_w, router_w=router_w, moe_w1=moe_w1, moe_w3=moe_w3, moe_w2=moe_w2)
    return _forward(cfg, x_prompt, x_sample, state_s5, cache_k, cache_v, state_hgrn, state_ret, c, c_ctx, p)
```

```python
import functools
import math
from typing import NamedTuple

import numpy as np
import jax
import jax.numpy as jnp
from jax import lax
from jax.experimental import pallas as pl
from jax.experimental.pallas import tpu as pltpu

F32 = jnp.float32
BF16 = jnp.bfloat16
I32 = jnp.int32

D_MODEL = 1024
DEPTH = 4
N_EVEN = 2
N_ODD = 2
EPS = 1e-6
GRID_W = 64
S5_WIDTH = 512
S5_GROUP = 16
S5_GROUPS = 32
S5_STATE = 64
S5_CHUNK = 16
S5_PAIRS = S5_GROUPS // 2
S5_QUAD = 4
ATT_HEAD_DIM = 64
ATT_HEADS = 8
ATT_KV_HEADS = 2
ATT_GROUP = 4
WINDOW = 128
ROPE_BASE = 10000.0
AB_IN = 1280
HEAD_DIM = 128
REC_HEADS = 4
CD_IN = 4608
N_EXPERTS = 16
EXPERT_FF = 1536
CAPACITY_FACTOR = 2

LANES = 128
ROW_TILE = 256
FFN_ROWS = 512
FF_TILE = 512
F32_INF_BITS = 0x7F800000
F32_MIN_NORMAL_BITS = 0x00800000
VMEM_LIMIT = 56 * 1024 * 1024


class Cfg(NamedTuple):
    ctx_b: int
    ctx_n: int
    lat_b: int
    lat_n: int
    past: int

    @property
    def tc(self):
        return self.ctx_b * self.ctx_n

    @property
    def t(self):
        return self.tc + self.lat_b * self.lat_n

    @property
    def gs(self):
        return self.lat_n

    @property
    def groups(self):
        return 1 + self.lat_b

    @property
    def slots(self):
        return CAPACITY_FACTOR * self.gs // N_EXPERTS


def _cp(sem, vmem=VMEM_LIMIT):
    return pltpu.CompilerParams(dimension_semantics=sem, vmem_limit_bytes=vmem)


def _nt(a, b):
    return lax.dot_general(a, b, (((1,), (1,)), ((), ())), preferred_element_type=F32)


def _dot(a, b, precision=None):
    return jnp.dot(a, b, preferred_element_type=F32, precision=precision)


_HI = lax.Precision.HIGHEST


def _adaln_kernel(c_ref, w_ref, b_ref, o_ref):
    c = c_ref[...]
    s = c * jax.nn.sigmoid(c)
    o_ref[0] = _dot(s.astype(BF16), w_ref[0].astype(BF16)) + b_ref[0]


def adaln_table(conds, mod_w, mod_b):
    n6 = 6 * D_MODEL
    tn = 1536
    return pl.pallas_call(
        _adaln_kernel,
        grid=(DEPTH, n6 // tn),
        in_specs=[pl.BlockSpec((8, D_MODEL), lambda l, j: (0, 0)),
                  pl.BlockSpec((1, D_MODEL, tn), lambda l, j: (l, 0, j)),
                  pl.BlockSpec((1, 1, tn), lambda l, j: (l, 0, j))],
        out_specs=pl.BlockSpec((1, 8, tn), lambda l, j: (l, 0, j)),
        out_shape=jax.ShapeDtypeStruct((DEPTH, 8, n6), F32),
        compiler_params=_cp(("parallel", "parallel")),
        name="adaln_table",
    )(conds, mod_w, mod_b.reshape(DEPTH, 1, n6))


def _mod_spec(cfg, layer, part):
    tiles_per_group = cfg.gs // ROW_TILE
    return pl.BlockSpec((None, None, None, 1, D_MODEL),
                        lambda i: (layer, i // tiles_per_group, part, 0, 0))


def _rms(x, w):
    return x * lax.rsqrt(jnp.mean(x * x, axis=-1, keepdims=True) + EPS) * w


def _nmm_kernel(x_ref, nw_ref, sh_ref, sc_ref, w_ref, o_ref):
    h = _rms(x_ref[...], nw_ref[...]) * (1.0 + sc_ref[...]) + sh_ref[...]
    o_ref[...] = _dot(h.astype(BF16), w_ref[...])


def norm_mod_matmul(cfg, x, nw, mod5, layer, w_bf16):
    n_out = w_bf16.shape[1]
    return pl.pallas_call(
        _nmm_kernel,
        grid=(cfg.t // ROW_TILE,),
        in_specs=[pl.BlockSpec((ROW_TILE, D_MODEL), lambda i: (i, 0)),
                  pl.BlockSpec((1, D_MODEL), lambda i: (0, 0)),
                  _mod_spec(cfg, layer, 0), _mod_spec(cfg, layer, 1),
                  pl.BlockSpec((D_MODEL, n_out), lambda i: (0, 0))],
        out_specs=pl.BlockSpec((ROW_TILE, n_out), lambda i: (i, 0)),
        out_shape=jax.ShapeDtypeStruct((cfg.t, n_out), F32),
        compiler_params=_cp(("parallel",)),
        name="norm_mod_matmul",
    )(x, nw.reshape(1, D_MODEL), mod5, mod5, w_bf16)


def _s5_tables(lam_re, lam_im, log_dt, b_re, b_im, c_re, c_im):
    L = S5_CHUNK
    lr, li = lam_re.astype(F32), lam_im.astype(F32)
    dt = jnp.exp(log_dt.astype(F32))[..., None]
    mag = jnp.exp(lr * dt)
    ar, ai = mag * jnp.cos(li * dt), mag * jnp.sin(li * dt)
    den = lr * lr + li * li
    cr = ((ar - 1.0) * lr + ai * li) / den
    ci = (ai * lr - (ar - 1.0) * li) / den
    br, bi = b_re.astype(F32), b_im.astype(F32)
    bbr = cr[..., None] * br - ci[..., None] * bi
    bbi = cr[..., None] * bi + ci[..., None] * br

    def pw_step(carry, _):
        pr, pi = carry
        return (pr * ar - pi * ai, pr * ai + pi * ar), (pr, pi)

    (_, _), (pr, pi) = lax.scan(pw_step, (jnp.ones_like(ar), jnp.zeros_like(ar)), None, length=L + 1)
    er = pr[:L, ..., None] * bbr[None] - pi[:L, ..., None] * bbi[None]
    ei = pr[:L, ..., None] * bbi[None] + pi[:L, ..., None] * bbr[None]
    ccr, cci = c_re.astype(F32), c_im.astype(F32)
    kt = (jnp.einsum('dgcp,tdgpe->tdgce', ccr, er, precision=_HI)
          - jnp.einsum('dgcp,tdgpe->tdgce', cci, ei, precision=_HI))
    eye2 = jnp.eye(2, dtype=F32)
    wide = 2 * L * S5_GROUP

    def pairs(a, axis):
        return a.reshape(a.shape[:axis] + (S5_PAIRS, 2) + a.shape[axis + 1:])

    def lag_rows(k):
        k = pairs(k, 1).transpose(1, 2, 4, 0, 3)
        k = k[:, :, :, :, None, :] * eye2[None, :, None, None, :, None]
        return k.reshape(S5_PAIRS, 2 * S5_GROUP, wide)

    kc_f, kc_b = lag_rows(kt[:, 0]), lag_rows(kt[::-1, 1])
    wq = jnp.stack([er[::-1, 0], ei[::-1, 0], er[:, 1], ei[:, 1]], axis=0)
    wq = pairs(wq, 2).transpose(2, 1, 3, 5, 0, 4)
    wcat = (wq[:, :, :, :, :, None, :] * eye2[None, None, :, None, None, :, None]).reshape(S5_PAIRS, wide, wide)

    def g_coef(d, p_r, p_i):
        g_r = ccr[d][None] * p_r[:, :, None, :] - cci[d][None] * p_i[:, :, None, :]
        g_i = -ccr[d][None] * p_i[:, :, None, :] - cci[d][None] * p_r[:, :, None, :]
        return g_r, g_i

    gq = jnp.stack(g_coef(0, pr[1:L + 1, 0], pi[1:L + 1, 0]) + g_coef(1, pr[L:0:-1, 1], pi[L:0:-1, 1]), axis=0)
    gq = pairs(gq, 2).transpose(2, 0, 3, 5, 1, 4)
    gmat = (gq[:, :, :, :, :, None, :] * eye2[None, None, :, None, None, :, None]).reshape(S5_PAIRS, wide, wide)

    def pair_lanes(a):
        return a.reshape(S5_PAIRS, 2 * S5_STATE)

    a16 = jnp.stack([pair_lanes(pr[L, 0]), pair_lanes(pi[L, 0]),
                     pair_lanes(pr[L, 1]), pair_lanes(pi[L, 1])], axis=1)
    return wcat.astype(BF16), kc_f, kc_b, gmat.astype(BF16), a16


def _s5_pair(cfg, u, w_mat, m_mat, g_mat, a, h0_ref, pp, fin_ref, xs_ref):
    rc = cfg.tc // S5_CHUNK
    lc = cfg.ctx_n // S5_CHUNK
    ll = cfg.lat_n // S5_CHUNK
    r_all = cfg.t // S5_CHUNK
    w = _dot(u, w_mat)
    row =lax.broadcasted_iota(I32, (r_all, LANES), 0)
    is_ctx = row < rc
    pos = jnp.where(is_ctx, row % lc, (row - rc) % ll)
    seg = jnp.where(is_ctx, lc, ll)
    prev_parts = []
    for d in range(2):
        xr = w[:, (2 * d) * LANES:(2 * d + 1) * LANES]
        xi = w[:, (2 * d + 1) * LANES:(2 * d + 2) * LANES]
        ar, ai = a[2 * d:2 * d + 1], a[2 * d + 1:2 * d + 2]
        first = 0 if d == 0 else seg - 1
        h0r = jnp.zeros((r_all, LANES), F32)
        h0i = jnp.zeros((r_all, LANES), F32)
        for b in range(cfg.lat_b):
            sel = (row >= rc + b * ll) & (row < rc + (b + 1) * ll)
            h0r = jnp.where(sel, h0_ref[pp, b, 2 * d:2 * d + 1, :], h0r)
            h0i = jnp.where(sel, h0_ref[pp, b, 2 * d + 1:2 * d + 2, :], h0i)
        at_first = pos == first
        xr = xr + jnp.where(at_first, ar * h0r - ai * h0i, 0.0)
        xi = xi + jnp.where(at_first, ar * h0i + ai * h0r, 0.0)
        pr, pi = ar, ai
        step = 1
        while step < max(lc, ll):
            if d == 0:
                sr, si = pltpu.roll(xr, step, 0), pltpu.roll(xi, step, 0)
                ok = (pos >= step)
            else:
                sr, si = pltpu.roll(xr, r_all - step, 0), pltpu.roll(xi, r_all - step, 0)
                ok = (pos < seg - step)
            xr, xi = (xr + jnp.where(ok, pr * sr - pi * si, 0.0),
                      xi + jnp.where(ok, pr * si + pi * sr, 0.0))
            pr, pi = pr * pr - pi * pi, 2.0 * pr * pi
            step *= 2
        if d == 0:
            nr, ni = pltpu.roll(xr, 1, 0), pltpu.roll(xi, 1, 0)
        else:
            nr, ni = pltpu.roll(xr, r_all - 1, 0), pltpu.roll(xi, r_all - 1, 0)
        prev_parts += [jnp.where(at_first, h0r, nr), jnp.where(at_first, h0i, ni)]
        xs_ref[2 * d] = xr
        xs_ref[2 * d + 1] = xi
    xin = jnp.concatenate(prev_parts, axis=1).astype(BF16)
    for d in range(2):
        start = lc - 1 if d == 0 else 0
        for k in range(2):
            fin_ref[pp, 2 * d + k] = xs_ref[2 * d + k, pl.ds(start, cfg.ctx_b, stride=lc), :]
    return _dot(u, m_mat) + _dot(xin, g_mat)


def _s5_toeplitz(kf, kb):
    sw = 2 * S5_GROUP
    wide = S5_CHUNK * sw
    lane = lax.broadcasted_iota(I32, (sw, wide), 1)
    blocks = []
    for s in range(S5_CHUNK):
        f = kf if s == 0 else pltpu.roll(kf, sw * s, 1)
        back = (wide - sw * (S5_CHUNK - 1 - s)) % wide
        b = kb if back == 0 else pltpu.roll(kb, back, 1)
        blocks.append(jnp.where(lane >= sw * s, f, 0.0) + jnp.where(lane < sw * (s + 1), b, 0.0))
    return jnp.concatenate(blocks, axis=0).astype(BF16)


def _s5_kernel(cfg, u_ref, w_ref, kf_ref, kb_ref, g_ref, a_ref, h0_ref, y_ref, fin_ref,
               uflat_ref, ystage_ref, xs_ref):
    r_all = cfg.t // S5_CHUNK
    sw = 2 * S5_GROUP
    for s in range(S5_CHUNK):
        blk = u_ref[pl.ds(s, r_all, stride=S5_CHUNK), :]
        for pp in range(S5_QUAD):
            uflat_ref[pp, :, s * sw:(s + 1) * sw] = blk[:, pp * sw:(pp + 1) * sw]
    for pp in range(S5_QUAD):
        y = _s5_pair(cfg, uflat_ref[pp].astype(BF16), w_ref[pp], _s5_toeplitz(kf_ref[pp], kb_ref[pp]),
                     g_ref[pp], a_ref[pp], h0_ref, pp, fin_ref, xs_ref)
        for s in range(S5_CHUNK):
            ystage_ref[s, :, pp * sw:(pp + 1) * sw] = y[:, s * sw:(s + 1) * sw]
    for s in range(S5_CHUNK):
        y_ref[pl.ds(s, r_all, stride=S5_CHUNK), :] = ystage_ref[s]


def s5_mix(cfg, proj, wcat, kc_f, kc_b, gmat, a16, h0):
    r_all = cfg.t // S5_CHUNK
    wide = 2 * S5_CHUNK * S5_GROUP
    mat = pl.BlockSpec((S5_QUAD, wide, wide), lambda q: (q, 0, 0))
    lag = pl.BlockSpec((S5_QUAD, 2 * S5_GROUP, wide), lambda q: (q, 0, 0))
    return pl.pallas_call(
        functools.partial(_s5_kernel, cfg),
        grid=(S5_PAIRS // S5_QUAD,),
        in_specs=[pl.BlockSpec((cfg.t, LANES), lambda q: (0, q)), mat, lag, lag, mat,
                  pl.BlockSpec((S5_QUAD, 4, LANES), lambda q: (q, 0, 0)),
                  pl.BlockSpec((S5_QUAD, cfg.lat_b, 4, LANES), lambda q: (q, 0, 0, 0))],
        out_specs=[pl.BlockSpec((cfg.t, LANES), lambda q: (0, q)),
                   pl.BlockSpec((S5_QUAD, 4, cfg.ctx_b, LANES), lambda q: (q, 0, 0, 0))],
        out_shape=[jax.ShapeDtypeStruct((cfg.t, S5_WIDTH), F32),
                   jax.ShapeDtypeStruct((S5_PAIRS, 4, cfg.ctx_b, LANES), F32)],
        scratch_shapes=[pltpu.VMEM((S5_QUAD, r_all, wide), F32),
                        pltpu.VMEM((S5_CHUNK, r_all, LANES), F32),
                        pltpu.VMEM((4, r_all, LANES), F32)],
        compiler_params=_cp(("parallel",)),
        name="s5_mix",
    )(proj, wcat, kc_f, kc_b, gmat, a16, h0)


def _rope(x, cos, sin_signed, half):
    w = x.shape[-1]
    lane = lax.broadcasted_iota(I32, x.shape, x.ndim - 1)
    swapped = jnp.where((lane % (2 * half)) < half,
                        pltpu.roll(x, w - half, x.ndim - 1), pltpu.roll(x, half, x.ndim - 1))
    return x * cos + swapped * sin_signed


def _attn_kernel(n_parts, latent, n_q_blocks, sink_ref, q_ref, *refs):
    hd = ATT_HEAD_DIM
    scale = hd ** -0.5
    if latent:
        (kp_ref, kc_ref, kn_ref, vp_ref, vc_ref, vn_ref, ck_ref, cv_ref,
         cq_ref, sq_ref, ckp_ref, skp_ref, ckc_ref, skc_ref, ckn_ref, skn_ref, o_ref) = refs
        i = pl.program_id(1)
        q = _rope(q_ref[...], cq_ref[...], sq_ref[...], hd // 2)
        kparts = [_rope(kp_ref[...], ckp_ref[...], skp_ref[...], hd // 2),
                  _rope(kc_ref[...], ckc_ref[...], skc_ref[...], hd // 2),
                  _rope(kn_ref[...], ckn_ref[...], skn_ref[...], hd // 2)]
        kband = jnp.concatenate(kparts, axis=0)
        vband = jnp.concatenate([vp_ref[...], vc_ref[...], vn_ref[...]], axis=0)
        keys = [kband, ck_ref[...]]
        vals = [vband, cv_ref[...]]
        nrow = ATT_GROUP * LANES
        qpos = lax.broadcasted_iota(I32, (nrow, 3 * LANES), 0) % LANES
        kpos = lax.broadcasted_iota(I32, (nrow, 3 * LANES), 1) - LANES
        kabs = kpos + i * LANES
        band_ok = (jnp.abs(kpos - qpos) <= WINDOW) & (kabs >= 0) & (kabs < n_q_blocks * LANES)
        masks = [band_ok, None]
    else:
        k_ref, v_ref, o_ref = refs
        q = q_ref[...]
        keys = [k_ref[...]]
        vals = [v_ref[...]]
        masks = [None]
    for kv in range(ATT_KV_HEADS):
        qs = jnp.concatenate([q[:, (kv * ATT_GROUP + g) * hd:(kv * ATT_GROUP + g + 1) * hd]
                              for g in range(ATT_GROUP)], axis=0).astype(BF16)
        sink_col = jnp.concatenate([jnp.full((LANES, 1), sink_ref[kv * ATT_GROUP + g], F32)
                                    for g in range(ATT_GROUP)], axis=0)
        logits = []
        mx = sink_col
        for kk, msk in zip(keys, masks):
            s = _nt(qs, kk[:, kv * hd:(kv + 1) * hd].astype(BF16)) * scale
            if msk is not None:
                s = jnp.where(msk, s, -1e30)
            logits.append(s)
            mx = jnp.maximum(mx, jnp.max(s, axis=-1, keepdims=True))
        den = jnp.exp(sink_col - mx)
        acc = None
        for s, vv in zip(logits, vals):
            p = jnp.exp(s - mx)
            den = den + jnp.sum(p, axis=-1, keepdims=True)
            o = _dot(p.astype(BF16), vv[:, kv * hd:(kv + 1) * hd].astype(BF16))
            acc = o if acc is None else acc + o
        out = acc / den
        for g in range(ATT_GROUP):
            h = kv * ATT_GROUP + g
            o_ref[:, h * hd:(h + 1) * hd] = out[g * LANES:(g + 1) * LANES, :]


def attention(cfg, proj, sink, cache_k, cache_v, rope_cs):
    qcol = S5_WIDTH // 512
    kcol = (S5_WIDTH + 512) // LANES
    vcol = kcol + 1
    smem = pl.BlockSpec(memory_space=pltpu.SMEM)
    nqc = cfg.ctx_n // LANES
    y_ctx = pl.pallas_call(
        functools.partial(_attn_kernel, 1, False, nqc),
        grid=(cfg.ctx_b, nqc),
        in_specs=[smem,
                  pl.BlockSpec((LANES, 512), lambda b, i: (b * nqc + i, qcol)),
                  pl.BlockSpec((cfg.ctx_n, LANES), lambda b, i: (b, kcol)),
                  pl.BlockSpec((cfg.ctx_n, LANES), lambda b, i: (b, vcol))],
        out_specs=pl.BlockSpec((LANES, 512), lambda b, i: (b * nqc + i, 0)),
        out_shape=jax.ShapeDtypeStruct((cfg.tc, 512), F32),
        compiler_params=_cp(("parallel", "parallel")),
        name="attn_context",
    )(sink, proj, proj, proj)
    nql = cfg.lat_n // LANES
    off = cfg.tc // LANES
    cos_t, sin_t = rope_cs

    def rb(b, i):
        return off + b * nql + i

    def prev(i):
        return jnp.maximum(i - 1, 0)

    def nxt(i):
        return jnp.minimum(i + 1, nql - 1)

    kspec = lambda f, col: pl.BlockSpec((LANES, LANES), lambda b, i: (rb(b, f(i)), col))
    tspec = lambda f: pl.BlockSpec((LANES, LANES), lambda b, i: (f(i), 0))
    same = lambda i: i
    y_lat = pl.pallas_call(
        functools.partial(_attn_kernel, 2, True, nql),
        grid=(cfg.lat_b, nql),
        in_specs=[smem,
                  pl.BlockSpec((LANES, 512), lambda b, i: (rb(b, i), qcol)),
                  kspec(prev, kcol), kspec(same, kcol), kspec(nxt, kcol),
                  kspec(prev, vcol), kspec(same, vcol), kspec(nxt, vcol),
                  pl.BlockSpec((None, cfg.past, LANES), lambda b, i: (b, 0, 0)),
                  pl.BlockSpec((None, cfg.past, LANES), lambda b, i: (b, 0, 0)),
                  pl.BlockSpec((LANES, 512), lambda b, i: (i, 0)),
                  pl.BlockSpec((LANES, 512), lambda b, i: (i, 0)),
                  tspec(prev), tspec(prev), tspec(same), tspec(same), tspec(nxt), tspec(nxt)],
        out_specs=pl.BlockSpec((LANES, 512), lambda b, i: (b * nql + i, 0)),
        out_shape=jax.ShapeDtypeStruct((cfg.lat_b * cfg.lat_n, 512), F32),
        compiler_params=_cp(("parallel", "parallel")),
        name="attn_latent",
    )(sink, proj, proj, proj, proj, proj, proj, proj, cache_k, cache_v,
      cos_t, sin_t, cos_t, sin_t, cos_t, sin_t, cos_t, sin_t)
    return jnp.concatenate([y_ctx, y_lat], axis=0)


def _rope_tables(n, head, width):
    rows = jnp.repeat(jnp.arange(n // GRID_W, dtype=F32), GRID_W)
    cols = (jnp.arange(n) % GRID_W).astype(F32)
    nf = head // 4
    inv = ROPE_BASE ** (-jnp.arange(nf, dtype=F32) / nf)
    ang = jnp.concatenate([rows[:, None] * inv, cols[:, None] * inv], axis=-1)
    c, s = jnp.cos(ang), jnp.sin(ang)
    cos_h = jnp.concatenate([c, c], axis=-1)
    sin_h = jnp.concatenate([-s, s], axis=-1)
    reps = width // head
    return jnp.tile(cos_h, (1, reps)), jnp.tile(sin_h, (1, reps))


def _even_out_kernel(x_ref, u_ref, yd_ref, ya_ref, d_ref, glu_ref, wo_ref, g_ref, o_ref):
    y = jax.nn.gelu(u_ref[...] * d_ref[...] + yd_ref[...])
    z = y * jax.nn.sigmoid(_dot(y.astype(BF16), glu_ref[...]))
    cat = jnp.concatenate([z, ya_ref[...]], axis=-1).astype(BF16)
    o_ref[...] = x_ref[...] + g_ref[...] * _dot(cat, wo_ref[...])


def even_out(cfg, x, proj, yd, yatt, s5_d, glu_bf16, wo_bf16, mod5, layer):
    tile = lambda w: pl.BlockSpec((ROW_TILE, w), lambda i: (i, 0))
    return pl.pallas_call(
        _even_out_kernel,
        grid=(cfg.t // ROW_TILE,),
        in_specs=[tile(D_MODEL), tile(S5_WIDTH), tile(S5_WIDTH), tile(512),
                  pl.BlockSpec((1, S5_WIDTH), lambda i: (0, 0)),
                  pl.BlockSpec((S5_WIDTH, S5_WIDTH), lambda i: (0, 0)),
                  pl.BlockSpec((D_MODEL, D_MODEL), lambda i: (0, 0)),
                  _mod_spec(cfg, layer, 2)],
        out_specs=tile(D_MODEL),
        out_shape=jax.ShapeDtypeStruct((cfg.t, D_MODEL), F32),
        compiler_params=_cp(("parallel",)),
        name="even_out",
    )(x, proj, yd, yatt, s5_d.reshape(1, S5_WIDTH), glu_bf16, wo_bf16, mod5)


REC_CHUNK = 128
REC_LEVELS = 7


def _rec_tables():
    c = REC_CHUNK
    t = np.arange(c)[:, None]
    u = np.arange(c)[None, :]
    out = np.zeros((2, (REC_LEVELS + 2) * c, c), np.float32)
    for d in range(2):
        for lv in range(REC_LEVELS):
            h = 1 << lv
            mid = (t // (2 * h)) * (2 * h) + h
            second = (t % (2 * h)) >= h
            if d == 0:
                a = np.where(second, (u >= mid) & (u <= t), (u > t) & (u < mid))
            else:
                a = np.where(second, (u >= mid) & (u < t), (u >= t) & (u < mid))
            out[d, lv * c:(lv + 1) * c] = a
        if d == 0:
            out[d, REC_LEVELS * c:(REC_LEVELS + 1) * c] = u <= t
            out[d, (REC_LEVELS + 1) * c:] = u > t
        else:
            out[d, REC_LEVELS * c:(REC_LEVELS + 1) * c] = u >= t
            out[d, (REC_LEVELS + 1) * c:] = u < t
    return out


def _rec_kernel(cfg, layer_o, *refs):
    fwd_in, bwd_in = refs[0:6], refs[6:12]
    amat_ref, lbl_ref, rdec_ref = refs[12:15]
    rope_f, rope_b = refs[15:17], refs[17:19]
    s0_f, s0_b, o_f, o_b, st_f, st_b = refs[19:25]
    _rec_direction(cfg, layer_o, 0, *fwd_in, amat_ref, lbl_ref, rdec_ref, *rope_f, s0_f, o_f, st_f)
    _rec_direction(cfg, layer_o, 1, *bwd_in, amat_ref, lbl_ref, rdec_ref, *rope_b, s0_b, o_b, st_b)


def _rec_direction(cfg, layer_o, d, q_ref, z_ref, v_ref, rq_ref, rk_ref, rv_ref,
                   amat_ref, lbl_ref, rdec_ref, cos_ref, sin_ref, s0_ref, o_ref, st_ref):
    c = REC_CHUNK
    step = pl.program_id(0)
    n_chunks = cfg.t // c
    chunk = step if d == 0 else n_chunks - 1 - step
    cpc = cfg.ctx_n // c
    cpl = cfg.lat_n // c
    n_ctx_chunks = cfg.tc // c
    is_lat = chunk >= n_ctx_chunks
    pos = jnp.where(is_lat, (chunk - n_ctx_chunks) % cpl, chunk % cpc)
    last = jnp.where(is_lat, cpl - 1, cpc - 1)
    first_processed = (pos == 0) if d == 0 else (pos == last)

    @pl.when(first_processed & jnp.logical_not(is_lat))
    def _():
        st_ref[...] = jnp.zeros_like(st_ref)

    @pl.when(first_processed & is_lat)
    def _():
        st_ref[...] = s0_ref[...]

    row = lax.broadcasted_iota(I32, (c, c), 0)
    col = lax.broadcasted_iota(I32, (c, c), 1)
    fwd = d == 0
    amat = amat_ref[d]

    def mix_head(q, k, v, scores, e_q, e_k, hidx):
        o = _dot(scores.astype(BF16), v.astype(BF16))
        st = st_ref[hidx]
        o = o + _nt((q * e_q).astype(BF16), st.astype(BF16))
        kd = (k * e_k).astype(BF16)
        total = jnp.where(fwd, e_q[c - 1:c], e_q[0:1])
        st_ref[hidx] = st * total + _dot(v.T.astype(BF16), kd)
        return o

    def hier_scores(q, k, e):
        scores = jnp.where(row == col, _nt(q.astype(BF16), k.astype(BF16)), 0.0)
        for lv in range(REC_LEVELS):
            h = 1 << lv
            el = e[lv * c:(lv + 1) * c]
            second = (row % (2 * h)) >= h
            is_q = second == fwd
            qt = jnp.where(is_q, q * el, 0.0).astype(BF16)
            kt = jnp.where(is_q, 0.0, k * el).astype(BF16)
            same = (row // (2 * h)) == (col // (2 * h))
            scores = scores + jnp.where(same, _nt(qt, kt), 0.0)
        return scores

    lbl = lbl_ref[...]
    mx = jnp.max(lbl, axis=0, keepdims=True)
    ex = jnp.exp(lbl - mx)
    sm = ex / jnp.sum(ex, axis=0, keepdims=True)
    lb_all = jnp.zeros_like(sm[0])
    for i in range(1, layer_o + 1):
        lb_all = lb_all + sm[i]
    lb = jnp.where(fwd, lb_all[0:1], lb_all[1:2])
    z = z_ref[...]
    log_sig = jnp.minimum(z, 0.0) - jnp.log1p(jnp.exp(-jnp.abs(z)))
    a_ = jnp.log1p(-lb) + log_sig
    b_ = jnp.log(lb)
    big = jnp.maximum(a_, b_)
    logf = big + jnp.log1p(jnp.exp(-jnp.abs(a_ - b_)))
    kh = (1.0 - lb) * jax.nn.sigmoid(-z)
    l1 = logf.astype(BF16)
    r1 = logf - l1.astype(F32)
    l2 = r1.astype(BF16)
    l3 = (r1 - l2.astype(F32)).astype(BF16)
    wide = REC_HEADS * HEAD_DIM
    dsum_all = _dot(amat, jnp.concatenate([l1, l2, l3], axis=1))
    dsum_all = dsum_all[:, :wide] + dsum_all[:, wide:2 * wide] + dsum_all[:, 2 * wide:]
    qh, vh = q_ref[...], v_ref[...]
    outs = []
    for h in range(REC_HEADS):
        sl = slice(h * HEAD_DIM, (h + 1) * HEAD_DIM)
        e = jnp.exp(dsum_all[:, sl])
        q, k = qh[:, sl], kh[:, sl]
        outs.append(mix_head(q, k, vh[:, sl], hier_scores(q, k, e),
                             e[REC_LEVELS * c:(REC_LEVELS + 1) * c], e[(REC_LEVELS + 1) * c:], h))
    rq, rk = rq_ref[...], rk_ref[...]
    rq_rot = _rope(rq, cos_ref[...], sin_ref[...], HEAD_DIM // 2)
    rk_rot = _rope(rk, cos_ref[...], sin_ref[...], HEAD_DIM // 2)
    rq = jnp.where(is_lat, rq_rot, rq)
    rk = jnp.where(is_lat, rk_rot, rk) * (HEAD_DIM ** -0.5)
    rv = rv_ref[...]
    gam = jnp.exp(jnp.where(fwd, rdec_ref[0:1], rdec_ref[1:2]))
    lag = jnp.where(fwd, row - col, col - row)
    lag_f = jnp.maximum(lag, 0).astype(F32)
    steps_q = jnp.where(fwd, row + 1, c - row).astype(F32)
    steps_k = jnp.where(fwd, c - 1 - row, row).astype(F32)
    for h in range(REC_HEADS):
        sl = slice(h * HEAD_DIM, (h + 1) * HEAD_DIM)
        g = gam[:, sl]
        q, k = rq[:, sl], rk[:, sl]
        scores = jnp.where(lag >= 0, _nt(q.astype(BF16), k.astype(BF16)) * jnp.exp(-g * lag_f), 0.0)
        outs.append(mix_head(q, k, rv[:, sl], scores, jnp.exp(-g * steps_q), jnp.exp(-g * steps_k),
                             REC_HEADS + h))
    o_ref[...] = jnp.concatenate(outs, axis=1)


def recurrences(cfg, layer_o, proj, amat, lb_logits, ret_decay_lanes, rope_cs, s0):
    c = REC_CHUNK
    n_chunks = cfg.t // c
    n_ctx_chunks = cfg.tc // c
    cpc, cpl = cfg.ctx_n // c, cfg.lat_n // c
    n_seq = cfg.ctx_b + cfg.lat_b

    def chunk_of(d, s):
        return s if d == 0 else n_chunks - 1 - s

    def seq_of(d, s):
        ch = chunk_of(d, s)
        return jnp.where(ch >= n_ctx_chunks, cfg.ctx_b + (ch - n_ctx_chunks) // cpl, ch // cpc)

    def lat_of(d, s):
        return jnp.maximum(seq_of(d, s) - cfg.ctx_b, 0)

    def lat_pos(d, s):
        ch = chunk_of(d, s)
        return jnp.where(ch >= n_ctx_chunks, (ch - n_ctx_chunks) % cpl, 0)

    def col(d, k):
        return pl.BlockSpec((c, 512), lambda s: (chunk_of(d, s), k))

    def cols(d):
        return [col(d, 0), col(d, 1 + d), col(d, 3), col(d, 5), col(d, 6), col(d, 7)]

    def rope(d):
        return [pl.BlockSpec((c, 512), lambda s: (lat_pos(d, s), 0))] * 2

    def state_in(d):
        return pl.BlockSpec((None, None, 2 * REC_HEADS, HEAD_DIM, HEAD_DIM), lambda s: (d, lat_of(d, s), 0, 0, 0))

    def state_out(d):
        return pl.BlockSpec((None, 2 * REC_HEADS, HEAD_DIM, HEAD_DIM), lambda s: (seq_of(d, s), 0, 0, 0))

    nrow = (REC_LEVELS + 2) * c
    cos_t, sin_t = rope_cs
    o_shape = jax.ShapeDtypeStruct((cfg.t, D_MODEL), F32)
    st_shape = jax.ShapeDtypeStruct((n_seq, 2 * REC_HEADS, HEAD_DIM, HEAD_DIM), F32)
    return pl.pallas_call(
        functools.partial(_rec_kernel, cfg, layer_o),
        grid=(n_chunks,),
        in_specs=cols(0) + cols(1) + [
            pl.BlockSpec((2, nrow, c), lambda s: (0, 0, 0)),
            pl.BlockSpec((N_ODD, 2, 512), lambda s: (0, 0, 0)),
            pl.BlockSpec((None, 2, 512), lambda s: (0, 0, 0))] + rope(0) + rope(1) + [state_in(0), state_in(1)],
        out_specs=[pl.BlockSpec((c, D_MODEL), lambda s: (chunk_of(0, s), 0)),
                   pl.BlockSpec((c, D_MODEL), lambda s: (chunk_of(1, s), 0)),
                   state_out(0), state_out(1)],
        out_shape=[o_shape, o_shape, st_shape, st_shape],
        compiler_params=_cp(("arbitrary",)),
        name="recurrences",
    )(*([proj] * 12), amat, lb_logits, ret_decay_lanes, cos_t, sin_t, cos_t, sin_t, s0, s0)


def _odd_out_kernel(x_ref, of_ref, ob_ref, cg_ref, rg_ref, hw_ref, rw_ref, wo_ref, g_ref, o_ref):
    o = of_ref[...] + ob_ref[...]
    hw, rw = hw_ref[...], rw_ref[...]
    parts = []
    for h in range(REC_HEADS):
        sl = slice(h * HEAD_DIM, (h + 1) * HEAD_DIM)
        oh = o[:, sl]
        parts.append(oh * lax.rsqrt(jnp.mean(oh * oh, -1, keepdims=True) + EPS) * hw[:, sl])
    oc = jnp.concatenate(parts, axis=1) * jax.nn.sigmoid(cg_ref[...])
    parts = []
    for h in range(REC_HEADS):
        sl = slice(h * HEAD_DIM, (h + 1) * HEAD_DIM)
        oh = o[:, 512 + h * HEAD_DIM:512 + (h + 1) * HEAD_DIM]
        oh = oh - jnp.mean(oh, -1, keepdims=True)
        parts.append(oh * lax.rsqrt(jnp.mean(oh * oh, -1, keepdims=True) + EPS) * rw[:, sl])
    rg = rg_ref[...]
    orr = jnp.concatenate(parts, axis=1) * (rg * jax.nn.sigmoid(rg))
    cat = jnp.concatenate([oc, orr], axis=1).astype(BF16)
    o_ref[...] = x_ref[...] + g_ref[...] * _dot(cat, wo_ref[...])


def odd_out(cfg, x, o_f, o_b, proj, hgrn_norm_w, ret_norm_w, wo_bf16, mod5, layer):
    tile = lambda w, k=0: pl.BlockSpec((ROW_TILE, w), lambda i: (i, k))
    return pl.pallas_call(
        _odd_out_kernel,
        grid=(cfg.t // ROW_TILE,),
        in_specs=[tile(D_MODEL), tile(D_MODEL), tile(D_MODEL),
                  tile(512, 4), tile(512, 8),
                  pl.BlockSpec((1, 512), lambda i: (0, 0)), pl.BlockSpec((1, 512), lambda i: (0, 0)),
                  pl.BlockSpec((D_MODEL, D_MODEL), lambda i: (0, 0)),
                  _mod_spec(cfg, layer, 2)],
        out_specs=tile(D_MODEL),
        out_shape=jax.ShapeDtypeStruct((cfg.t, D_MODEL), F32),
        compiler_params=_cp(("parallel",)),
        name="odd_out",
    )(x, o_f, o_b, proj, proj, hgrn_norm_w.reshape(1, 512), ret_norm_w.reshape(1, 512), wo_bf16, mod5)


def _norm_logits_kernel(x_ref, nw_ref, sh_ref, sc_ref, wr_ref, h_ref, lg_ref):
    h = (_rms(x_ref[...], nw_ref[...]) * (1.0 + sc_ref[...]) + sh_ref[...]).astype(BF16)
    h_ref[...] = h
    lg_ref[...] = _nt(wr_ref[...], h)


def moe_norm_logits(cfg, x, nw, mod5, layer, wr_t_bf16):
    return pl.pallas_call(
        _norm_logits_kernel,
        grid=(cfg.t // ROW_TILE,),
        in_specs=[pl.BlockSpec((ROW_TILE, D_MODEL), lambda i: (i, 0)),
                  pl.BlockSpec((1, D_MODEL), lambda i: (0, 0)),
                  _mod_spec(cfg, layer, 3), _mod_spec(cfg, layer, 4),
                  pl.BlockSpec((N_EXPERTS, D_MODEL), lambda i: (0, 0))],
        out_specs=[pl.BlockSpec((ROW_TILE, D_MODEL), lambda i: (i, 0)),
                   pl.BlockSpec((N_EXPERTS, ROW_TILE), lambda i: (0, i))],
        out_shape=[jax.ShapeDtypeStruct((cfg.t, D_MODEL), BF16),
                   jax.ShapeDtypeStruct((N_EXPERTS, cfg.t), F32)],
        compiler_params=_cp(("parallel",)),
        name="moe_norm_logits",
    )(x, nw.reshape(1, D_MODEL), mod5, mod5, wr_t_bf16)


def _route_kernel(cfg, lg_ref, pos_ref, aff_ref, cj_ref):
    gs = cfg.gs
    g = pl.program_id(0)
    logits = lg_ref[...]
    mx = jnp.max(logits, axis=0, keepdims=True)
    ex = jnp.exp(logits - mx)
    aff = ex / jnp.sum(ex, axis=0, keepdims=True)
    aff_ref[...] = aff

    def seg_sum(x, seg):
        parts = []
        for r in range(gs // seg):
            s = jnp.sum(x[:, r * seg:(r + 1) * seg], axis=1, keepdims=True)
            parts.append(jnp.broadcast_to(s, (N_EXPERTS, seg)))
        return parts[0] if len(parts) == 1 else jnp.concatenate(parts, axis=1)

    tri = (lax.broadcasted_iota(I32, (LANES, LANES), 0) < lax.broadcasted_iota(I32, (LANES, LANES), 1)).astype(BF16)
    lane = lax.broadcasted_iota(I32, (N_EXPERTS, LANES), 1)

    def prefix(mask, seg):
        parts = []
        carry = jnp.zeros((N_EXPERTS, 1), F32)
        starts = jnp.zeros((N_EXPERTS, LANES), F32)
        for j in range(gs // LANES):
            if (j * LANES) % seg == 0:
                carry = jnp.zeros((N_EXPERTS, 1), F32)
            m = mask[:, j * LANES:(j + 1) * LANES]
            starts = starts + jnp.where(lane == j, carry, 0.0)
            parts.append(_dot(m.astype(BF16), tri) + carry)
            carry = carry + jnp.sum(m, axis=1, keepdims=True)
        starts = starts + jnp.where(lane == gs // LANES, carry, 0.0)
        return jnp.concatenate(parts, axis=1), starts

    def select(seg, cap):
        def at_least_cap(mid):
            return seg_sum((aff >= mid).astype(F32), seg) >= cap

        def halve_bits(_, lohi):
            lo, hi = lohi
            mid = lo + ((hi - lo) >> 1)
            ok = at_least_cap(pltpu.bitcast(mid, F32))
            return jnp.where(ok, mid, lo), jnp.where(ok, hi, mid)

        def halve_value(_, lohi):
            lo, hi = lohi
            mid = 0.5 * (lo + hi)
            ok = at_least_cap(mid)
            return jnp.where(ok, mid, lo), jnp.where(ok, hi, mid)

        lo0 = jnp.zeros((N_EXPERTS, gs), I32)
        hi0 = jnp.full((N_EXPERTS, gs), F32_INF_BITS, I32)
        lo_b, hi_b = lax.fori_loop(0, 31, halve_bits, (lo0, hi0))
        lo_f = jnp.where(lo_b < F32_MIN_NORMAL_BITS, 0.0, pltpu.bitcast(lo_b, F32))
        lo_f, hi_f = lax.fori_loop(0, 30, halve_value, (lo_f, pltpu.bitcast(hi_b, F32)))
        above = aff >= hi_f
        tied = jnp.logical_and(aff >= lo_f, aff < hi_f)
        need = cap - seg_sum(above.astype(F32), seg)
        before, _ = prefix(tied.astype(F32), seg)
        return jnp.logical_or(above, jnp.logical_and(tied, before < need))

    def finish(sel):
        self_f = sel.astype(F32)
        slot, starts = prefix(self_f, gs)
        pos_ref[...] = jnp.where(sel, slot.astype(I32), -1)
        cj_ref[...] = starts.astype(I32)

    @pl.when(g == 0)
    def _():
        finish(select(cfg.ctx_n, CAPACITY_FACTOR * cfg.ctx_n // N_EXPERTS))

    @pl.when(g > 0)
    def _():
        finish(select(cfg.lat_n, CAPACITY_FACTOR * cfg.lat_n // N_EXPERTS))


def moe_route(cfg, logits):
    gs = cfg.gs
    return pl.pallas_call(
        functools.partial(_route_kernel, cfg),
        grid=(cfg.groups,),
        in_specs=[pl.BlockSpec((N_EXPERTS, gs), lambda g: (0, g))],
        out_specs=[pl.BlockSpec((None, N_EXPERTS, gs), lambda g: (g, 0, 0)),
                   pl.BlockSpec((None, N_EXPERTS, gs), lambda g: (g, 0, 0)),
                   pl.BlockSpec((None, N_EXPERTS, LANES), lambda g: (g, 0, 0))],
        out_shape=[jax.ShapeDtypeStruct((cfg.groups, N_EXPERTS, gs), I32),
                   jax.ShapeDtypeStruct((cfg.groups, N_EXPERTS, gs), F32),
                   jax.ShapeDtypeStruct((cfg.groups, N_EXPERTS, LANES), I32)],
        compiler_params=_cp(("parallel",)),
        name="moe_route",
    )(logits)


def _slot_block(cfg):
    return min(LANES, cfg.slots)


GATHER_TOKENS = 2 * LANES


def _gather_kernel(cfg, cj_ref, h_ref, pos_ref, aff_ref, x_ref, gate_ref, acc_ref, gacc_ref):
    g, e = pl.program_id(0), pl.program_id(1)
    sb = _slot_block(cfg)
    tw = GATHER_TOKENS
    acc_ref[...] = jnp.zeros_like(acc_ref)
    gacc_ref[...] = jnp.zeros_like(gacc_ref)
    base = (g * N_EXPERTS + e) * LANES
    sub = lax.broadcasted_iota(I32, (sb, tw), 0)
    for tb in range(cfg.gs // tw):
        s0, s1 = cj_ref[base + tb * (tw // LANES)], cj_ref[base + (tb + 1) * (tw // LANES)]
        for jb in range(cfg.slots // sb):
            @pl.when((s1 > s0) & (s0 < (jb + 1) * sb) & (s1 > jb * sb))
            def _():
                p = pos_ref[pl.ds(e, 1), tb * tw:(tb + 1) * tw]
                hit = jnp.broadcast_to(p, (sb, tw)) == (sub + jb * sb)
                acc_ref[jb * sb:(jb + 1) * sb, :] += _dot(hit.astype(BF16), h_ref[tb * tw:(tb + 1) * tw, :])
                a = aff_ref[pl.ds(e, 1), tb * tw:(tb + 1) * tw]
                gacc_ref[jb * sb:(jb + 1) * sb, :] += jnp.sum(jnp.where(hit, a, 0.0), axis=1, keepdims=True)
    x_ref[...] = acc_ref[...].astype(BF16)
    gate_ref[...] = gacc_ref[...]


def moe_gather(cfg, cj, h, pos, aff):
    gs, slots = cfg.gs, cfg.slots
    grid_spec = pltpu.PrefetchScalarGridSpec(
        num_scalar_prefetch=1,
        grid=(cfg.groups, N_EXPERTS),
        in_specs=[pl.BlockSpec((gs, D_MODEL), lambda g, e, cj: (g, 0)),
                  pl.BlockSpec((None, N_EXPERTS, gs), lambda g, e, cj: (g, 0, 0)),
                  pl.BlockSpec((None, N_EXPERTS, gs), lambda g, e, cj: (g, 0, 0))],
        out_specs=[pl.BlockSpec((None, slots, D_MODEL), lambda g, e, cj: (e, g, 0)),
                   pl.BlockSpec((None, slots, 1), lambda g, e, cj: (e, g, 0))],
        scratch_shapes=[pltpu.VMEM((slots, D_MODEL), F32), pltpu.VMEM((slots, 1), F32)])
    return pl.pallas_call(
        functools.partial(_gather_kernel, cfg),
        grid_spec=grid_spec,
        out_shape=[jax.ShapeDtypeStruct((N_EXPERTS, cfg.groups * slots, D_MODEL), BF16),
                   jax.ShapeDtypeStruct((N_EXPERTS, cfg.groups * slots, 1), F32)],
        compiler_params=_cp(("parallel", "arbitrary")),
        name="moe_gather",
    )(cj.reshape(-1), h, pos, aff)


def _ffn_kernel(rows, x_ref, gate_ref, w1_ref, w3_ref, w2_ref, y_ref, acc_ref, w1b, w3b, w2b):
    c = pl.program_id(1)
    w1b[...] = w1_ref[...].astype(BF16)
    w3b[...] = w3_ref[...].astype(BF16)
    w2b[...] = w2_ref[...].astype(BF16)
    rt = min(FFN_ROWS, rows)

    def body(i, _):
        r = pl.multiple_of(i * rt, rt)
        x = x_ref[pl.ds(r, rt), :]
        a = _dot(x, w1b[...])
        b = _dot(x, w3b[...])
        hid = (a * jax.nn.sigmoid(a) * b).astype(BF16)
        y = _dot(hid, w2b[...])

        @pl.when(c == 0)
        def _():
            acc_ref[pl.ds(r, rt), :] = y

        @pl.when(c > 0)
        def _():
            acc_ref[pl.ds(r, rt), :] += y

        return 0

    lax.fori_loop(0, rows // rt, body, 0)

    @pl.when(c == pl.num_programs(1) - 1)
    def _():
        y_ref[...] = (acc_ref[...] * gate_ref[...]).astype(BF16)


def moe_ffn(cfg, layer, xg, gate, w1, w3, w2):
    rows = cfg.groups * cfg.slots
    nff = EXPERT_FF // FF_TILE
    return pl.pallas_call(
        functools.partial(_ffn_kernel, rows),
        grid=(N_EXPERTS, nff),
        in_specs=[pl.BlockSpec((None, rows, D_MODEL), lambda e, c: (e, 0, 0)),
                  pl.BlockSpec((None, rows, 1), lambda e, c: (e, 0, 0)),
                  pl.BlockSpec((None, None, D_MODEL, FF_TILE), lambda e, c: (layer, e, 0, c)),
                  pl.BlockSpec((None, None, D_MODEL, FF_TILE), lambda e, c: (layer, e, 0, c)),
                  pl.BlockSpec((None, None, FF_TILE, D_MODEL), lambda e, c: (layer, e, c, 0))],
        out_specs=pl.BlockSpec((None, rows, D_MODEL), lambda e, c: (e, 0, 0)),
        out_shape=jax.ShapeDtypeStruct((N_EXPERTS, rows, D_MODEL), BF16),
        scratch_shapes=[pltpu.VMEM((rows, D_MODEL), F32),
                        pltpu.VMEM((D_MODEL, FF_TILE), BF16), pltpu.VMEM((D_MODEL, FF_TILE), BF16),
                        pltpu.VMEM((FF_TILE, D_MODEL), BF16)],
        compiler_params=_cp(("parallel", "arbitrary")),
        name="moe_ffn",
    )(xg, gate, w1, w3, w2)


def _scatter_kernel(cfg, final, cj_ref, x_ref, pos_ref, y_ref, g_ref, fw_ref, o_ref):
    g, tb = pl.program_id(0), pl.program_id(1)
    sb = _slot_block(cfg)
    win = min(2 * sb, cfg.slots)
    sub = lax.broadcasted_iota(I32, (win, LANES), 0)
    acc = jnp.zeros((LANES, D_MODEL), F32)
    for e in range(N_EXPERTS):
        s0 = cj_ref[(g * N_EXPERTS + e) * LANES + tb]
        ws = pl.multiple_of(jnp.minimum((s0 // sb) * sb, cfg.slots - win), sb)
        hit = jnp.broadcast_to(pos_ref[e:e + 1, :], (win, LANES)) == (sub + ws)
        hit_t = jnp.where(hit, 1.0, 0.0).T.astype(BF16)
        acc = acc + _dot(hit_t, y_ref[e, pl.ds(ws, win), :])
    out = x_ref[...] + g_ref[...] * acc
    if final:
        out = _rms(out, fw_ref[...])
    o_ref[...] = out


def moe_scatter(cfg, layer, cj, x, pos, y, mod5, final_w, final):
    gs, slots = cfg.gs, cfg.slots
    nt = gs // LANES
    grid_spec = pltpu.PrefetchScalarGridSpec(
        num_scalar_prefetch=1,
        grid=(cfg.groups, nt),
        in_specs=[pl.BlockSpec((LANES, D_MODEL), lambda g, tb, cj: (g * nt + tb, 0)),
                  pl.BlockSpec((None, N_EXPERTS, LANES), lambda g, tb, cj: (g, 0, tb)),
                  pl.BlockSpec((N_EXPERTS, slots, D_MODEL), lambda g, tb, cj: (0, g, 0)),
                  pl.BlockSpec((None, None, None, 1, D_MODEL), lambda g, tb, cj: (layer, g, 5, 0, 0)),
                  pl.BlockSpec((1, D_MODEL), lambda g, tb, cj: (0, 0))],
        out_specs=pl.BlockSpec((LANES, D_MODEL), lambda g, tb, cj: (g * nt + tb, 0)))
    return pl.pallas_call(
        functools.partial(_scatter_kernel, cfg, final),
        grid_spec=grid_spec,
        out_shape=jax.ShapeDtypeStruct((cfg.t, D_MODEL), F32),
        compiler_params=_cp(("parallel", "arbitrary")),
        name="moe_scatter",
    )(cj.reshape(-1), x, pos, y, mod5, final_w.reshape(1, D_MODEL))


def _forward(cfg, x_prompt, x_sample, state_s5, cache_k, cache_v, state_hgrn, state_ret, c, c_ctx, p):
    assert cfg.tc == cfg.lat_n and cfg.ctx_n % ROW_TILE == 0 and cfg.lat_n % (GRID_W * 2) == 0
    x = jnp.concatenate([x_prompt.reshape(cfg.tc, D_MODEL), x_sample.reshape(-1, D_MODEL)], axis=0)
    conds = jnp.zeros((8, D_MODEL), F32).at[0].set(c_ctx).at[1:1 + cfg.lat_b].set(c)
    mod = adaln_table(conds, p['mod_w'], p['mod_b'])
    mod5 = mod[:, :cfg.groups].reshape(DEPTH, cfg.groups, 6, 1, D_MODEL)
    rope_att = _rope_tables(cfg.lat_n, ATT_HEAD_DIM, 512)
    rope_ret = _rope_tables(cfg.lat_n, HEAD_DIM, 512)
    amat = jnp.asarray(_rec_tables(), BF16)
    s5_fin, k_out, v_out, hgrn_fin, ret_fin = [], [], [], [], []
    for layer in range(DEPTH):
        if layer % 2 == 0:
            e = layer // 2
            proj = norm_mod_matmul(cfg, x, p['norm1_w'][layer], mod5, layer, p['ab_w_in'][e].astype(BF16))
            s5_tabs = _s5_tables(p['s5_lambda_re'][e], p['s5_lambda_im'][e], p['s5_log_dt'][e],
                                 p['s5_b_re'][e], p['s5_b_im'][e], p['s5_c_re'][e], p['s5_c_im'][e])
            h0 = state_s5[:, e].reshape(cfg.lat_b, 4, S5_PAIRS, 2 * S5_STATE).transpose(2, 0, 1, 3)
            yd, fin = s5_mix(cfg, proj, *s5_tabs, h0)
            s5_fin.append(fin.transpose(2, 1, 0, 3).reshape(cfg.ctx_b, 2, 2, S5_GROUPS, S5_STATE))
            ck = cache_k[:, e].reshape(cfg.lat_b, cfg.past, LANES)
            cv = cache_v[:, e].reshape(cfg.lat_b, cfg.past, LANES)
            yatt = attention(cfg, proj, p['attn_sink'][e], ck, cv, rope_att)
            k_out.append(proj[:cfg.tc, 1024:1152].reshape(cfg.ctx_b, cfg.ctx_n, ATT_KV_HEADS, ATT_HEAD_DIM))
            v_out.append(proj[:cfg.tc, 1152:1280].reshape(cfg.ctx_b, cfg.ctx_n, ATT_KV_HEADS, ATT_HEAD_DIM))
            x = even_out(cfg, x, proj, yd, yatt, p['s5_d'][e], p['s5_glu_w'][e].astype(BF16),
                         p['ab_w_out'][e].astype(BF16), mod5, layer)
        else:
            o = layer // 2
            proj = norm_mod_matmul(cfg, x, p['norm1_w'][layer], mod5, layer, p['cd_w_in'][o].astype(BF16))
            s0 = jnp.concatenate([state_hgrn[:, o], state_ret[:, o]], axis=2).transpose(1, 0, 2, 4, 3)
            rdec = jnp.repeat(p['ret_decay'][o], HEAD_DIM, axis=-1)
            o_f, o_b, st_f, st_b = recurrences(cfg, o, proj, amat, p['hgrn_lb_logits'], rdec[None], rope_ret, s0)
            st = jnp.stack([st_f[:cfg.ctx_b], st_b[:cfg.ctx_b]], axis=1).transpose(0, 1, 2, 4, 3)
            hgrn_fin.append(st[:, :, :REC_HEADS])
            ret_fin.append(st[:, :, REC_HEADS:])
            x = odd_out(cfg, x, o_f, o_b, proj, p['hgrn_norm_w'][o], p['ret_norm_w'][o],
                        p['cd_w_out'][o].astype(BF16), mod5, layer)
        h2, logits = moe_norm_logits(cfg, x, p['norm2_w'][layer], mod5, layer,
                                     p['router_w'][layer].T.astype(BF16))
        pos, aff, cj = moe_route(cfg, logits)
        xg, gate = moe_gather(cfg, cj, h2, pos, aff)
        yg = moe_ffn(cfg, layer, xg, gate, p['moe_w1'], p['moe_w3'], p['moe_w2'])
        x = moe_scatter(cfg, layer, cj, x, pos, yg, mod5, p['final_norm_w'], layer == DEPTH - 1)
    y_prompt = x[:cfg.tc].reshape(cfg.ctx_b, cfg.ctx_n, D_MODEL)
    y_sample = x[cfg.tc:].reshape(cfg.lat_b, cfg.lat_n, D_MODEL)
    return (y_prompt, y_sample, jnp.stack(s5_fin, axis=1), jnp.stack(k_out, axis=1), jnp.stack(v_out, axis=1),
            jnp.stack(hgrn_fin, axis=1), jnp.stack(ret_fin, axis=1))


def kernel(x_prompt, x_sample, state_s5, cache_k, cache_v, state_hgrn, state_ret, c, c_ctx, mod_w, mod_b, norm1_w, norm2_w, final_norm_w, ab_w_in, ab_w_out, s5_lambda_re, s5_lambda_im, s5_log_dt, s5_b_re, s5_b_im, s5_c_re, s5_c_im, s5_d, s5_glu_w, attn_sink, cd_w_in, cd_w_out, hgrn_lb_logits, hgrn_norm_w, ret_decay, ret_norm_w, router_w, moe_w1, moe_w3, moe_w2):
    cfg = Cfg(ctx_b=x_prompt.shape[0], ctx_n=x_prompt.shape[1], lat_b=x_sample.shape[0],
              lat_n=x_sample.shape[1], past=cache_k.shape[2])
    p = dict(mod_w=mod_w, mod_b=mod_b, norm1_w=norm1_w, norm2_w=norm2_w, final_norm_w=final_norm_w,
             ab_w_in=ab_w_in, ab_w_out=ab_w_out, s5_lambda_re=s5_lambda_re, s5_lambda_im=s5_lambda_im,
             s5_log_dt=s5_log_dt, s5_b_re=s5_b_re, s5_b_im=s5_b_im, s5_c_re=s5_c_re, s5_c_im=s5_c_im,
             s5_d=s5_d, s5_glu_w=s5_glu_w, attn_sink=attn_sink, cd_w_in=cd_w_in, cd_w_out=cd_w_out,
             hgrn_lb_logits=hgrn_lb_logits, hgrn_norm_w=hgrn_norm_w, ret_decay=ret_decay,
             ret_norm_w=ret_norm_w, router_w=router_w, moe_w1=moe_w1, moe_w3=moe_w3, moe_w2=moe_w2)
    return _forward(cfg, x_prompt, x_sample, state_s5, cache_k, cache_v, state_hgrn, state_ret, c, c_ctx, p)
```

```python
import functools
import math
from typing import NamedTuple

import numpy as np
import jax
import jax.numpy as jnp
from jax import lax
from jax.experimental import pallas as pl
from jax.experimental.pallas import tpu as pltpu

F32 = jnp.float32
BF16 = jnp.bfloat16
I32 = jnp.int32

D_MODEL = 1024
DEPTH = 4
N_EVEN = 2
N_ODD = 2
EPS = 1e-6
GRID_W = 64
S5_WIDTH = 512
S5_GROUP = 16
S5_GROUPS = 32
S5_STATE = 64
S5_CHUNK = 16
S5_PAIRS = S5_GROUPS // 2
S5_QUAD = 4
ATT_HEAD_DIM = 64
ATT_HEADS = 8
ATT_KV_HEADS = 2
ATT_GROUP = 4
WINDOW = 128
ROPE_BASE = 10000.0
AB_IN = 1280
HEAD_DIM = 128
REC_HEADS = 4
CD_IN = 4608
N_EXPERTS = 16
EXPERT_FF = 1536
CAPACITY_FACTOR = 2

LANES = 128
SUBLANES = 8
ROW_TILE = 256
FFN_ROWS = 512
FF_TILE = 512
F32_INF_BITS = 0x7F800000
F32_MIN_NORMAL_BITS = 0x00800000
VMEM_LIMIT = 56 * 1024 * 1024


class Cfg(NamedTuple):
    ctx_b: int
    ctx_n: int
    lat_b: int
    lat_n: int
    past: int

    @property
    def tc(self):
        return self.ctx_b * self.ctx_n

    @property
    def t(self):
        return self.tc + self.lat_b * self.lat_n

    @property
    def gs(self):
        return self.lat_n

    @property
    def groups(self):
        return 1 + self.lat_b

    @property
    def slots(self):
        return CAPACITY_FACTOR * self.gs // N_EXPERTS


def _cp(sem, vmem=VMEM_LIMIT):
    return pltpu.CompilerParams(dimension_semantics=sem, vmem_limit_bytes=vmem)


def _nt(a, b):
    return lax.dot_general(a, b, (((1,), (1,)), ((), ())), preferred_element_type=F32)


def _dot(a, b, precision=None):
    return jnp.dot(a, b, preferred_element_type=F32, precision=precision)


_HI = lax.Precision.HIGHEST


def _adaln_kernel(c_ref, w_ref, b_ref, o_ref):
    c = c_ref[...]
    s = c * jax.nn.sigmoid(c)
    o_ref[0] = _dot(s.astype(BF16), w_ref[0].astype(BF16)) + b_ref[0]


def adaln_table(conds, mod_w, mod_b):
    n6 = 6 * D_MODEL
    tn = 1536
    return pl.pallas_call(
        _adaln_kernel,
        grid=(DEPTH, n6 // tn),
        in_specs=[pl.BlockSpec((8, D_MODEL), lambda l, j: (0, 0)),
                  pl.BlockSpec((1, D_MODEL, tn), lambda l, j: (l, 0, j)),
                  pl.BlockSpec((1, 1, tn), lambda l, j: (l, 0, j))],
        out_specs=pl.BlockSpec((1, 8, tn), lambda l, j: (l, 0, j)),
        out_shape=jax.ShapeDtypeStruct((DEPTH, 8, n6), F32),
        compiler_params=_cp(("parallel", "parallel")),
        name="adaln_table",
    )(conds, mod_w, mod_b.reshape(DEPTH, 1, n6))


def _mod_spec(cfg, layer, part):
    tiles_per_group = cfg.gs // ROW_TILE
    return pl.BlockSpec((None, None, None, 1, D_MODEL),
                        lambda i: (layer, i // tiles_per_group, part, 0, 0))


def _rms(x, w):
    return x * lax.rsqrt(jnp.mean(x * x, axis=-1, keepdims=True) + EPS) * w


def _nmm_kernel(x_ref, nw_ref, sh_ref, sc_ref, w_ref, o_ref):
    h = _rms(x_ref[...], nw_ref[...]) * (1.0 + sc_ref[...]) + sh_ref[...]
    o_ref[...] = _dot(h.astype(BF16), w_ref[...])


def norm_mod_matmul(cfg, x, nw, mod5, layer, w_bf16):
    n_out = w_bf16.shape[1]
    return pl.pallas_call(
        _nmm_kernel,
        grid=(cfg.t // ROW_TILE,),
        in_specs=[pl.BlockSpec((ROW_TILE, D_MODEL), lambda i: (i, 0)),
                  pl.BlockSpec((1, D_MODEL), lambda i: (0, 0)),
                  _mod_spec(cfg, layer, 0), _mod_spec(cfg, layer, 1),
                  pl.BlockSpec((D_MODEL, n_out), lambda i: (0, 0))],
        out_specs=pl.BlockSpec((ROW_TILE, n_out), lambda i: (i, 0)),
        out_shape=jax.ShapeDtypeStruct((cfg.t, n_out), F32),
        compiler_params=_cp(("parallel",)),
        name="norm_mod_matmul",
    )(x, nw.reshape(1, D_MODEL), mod5, mod5, w_bf16)


def _s5_tables(lam_re, lam_im, log_dt, b_re, b_im, c_re, c_im):
    L = S5_CHUNK
    lr, li = lam_re.astype(F32), lam_im.astype(F32)
    dt = jnp.exp(log_dt.astype(F32))[..., None]
    mag = jnp.exp(lr * dt)
    ar, ai = mag * jnp.cos(li * dt), mag * jnp.sin(li * dt)
    den = lr * lr + li * li
    cr = ((ar - 1.0) * lr + ai * li) / den
    ci = (ai * lr - (ar - 1.0) * li) / den
    br, bi = b_re.astype(F32), b_im.astype(F32)
    bbr = cr[..., None] * br - ci[..., None] * bi
    bbi = cr[..., None] * bi + ci[..., None] * br

    def pw_step(carry, _):
        pr, pi = carry
        return (pr * ar - pi * ai, pr * ai + pi * ar), (pr, pi)

    (_, _), (pr, pi) = lax.scan(pw_step, (jnp.ones_like(ar), jnp.zeros_like(ar)), None, length=L + 1)
    er = pr[:L, ..., None] * bbr[None] - pi[:L, ..., None] * bbi[None]
    ei = pr[:L, ..., None] * bbi[None] + pi[:L, ..., None] * bbr[None]
    ccr, cci = c_re.astype(F32), c_im.astype(F32)
    kt = (jnp.einsum('dgcp,tdgpe->tdgce', ccr, er, precision=_HI)
          - jnp.einsum('dgcp,tdgpe->tdgce', cci, ei, precision=_HI))
    eye2 = jnp.eye(2, dtype=F32)
    wide = 2 * L * S5_GROUP

    def pairs(a, axis):
        return a.reshape(a.shape[:axis] + (S5_PAIRS, 2) + a.shape[axis + 1:])

    def lag_rows(k):
        k = pairs(k, 1).transpose(1, 2, 4, 0, 3)
        k = k[:, :, :, :, None, :] * eye2[None, :, None, None, :, None]
        return k.reshape(S5_PAIRS, 2 * S5_GROUP, wide)

    kc_f, kc_b = lag_rows(kt[:, 0]), lag_rows(kt[::-1, 1])
    wq = jnp.stack([er[::-1, 0], ei[::-1, 0], er[:, 1], ei[:, 1]], axis=0)
    wq = pairs(wq, 2).transpose(2, 1, 3, 5, 0, 4).astype(BF16)
    eye2b = eye2.astype(BF16)
    wcat = (wq[:, :, :, :, :, None, :] * eye2b[None, None, :, None, None, :, None]).reshape(S5_PAIRS, wide, wide)

    def g_coef(d, p_r, p_i):
        g_r = ccr[d][None] * p_r[:, :, None, :] - cci[d][None] * p_i[:, :, None, :]
        g_i = -ccr[d][None] * p_i[:, :, None, :] - cci[d][None] * p_r[:, :, None, :]
        return g_r, g_i

    gq = jnp.stack(g_coef(0, pr[1:L + 1, 0], pi[1:L + 1, 0]) + g_coef(1, pr[L:0:-1, 1], pi[L:0:-1, 1]), axis=0)
    gq = pairs(gq, 2).transpose(2, 0, 3, 5, 1, 4).astype(BF16)
    gmat = (gq[:, :, :, :, :, None, :] * eye2b[None, None, :, None, None, :, None]).reshape(S5_PAIRS, wide, wide)

    def pair_lanes(a):
        return a.reshape(S5_PAIRS, 2 * S5_STATE)

    a16 = jnp.stack([pair_lanes(pr[L, 0]), pair_lanes(pi[L, 0]),
                     pair_lanes(pr[L, 1]), pair_lanes(pi[L, 1])], axis=1)
    return wcat, kc_f, kc_b, gmat, a16


def _s5_pair(cfg, u, w_mat, m_mat, g_mat, a, h0_ref, pp, fin_ref, xs_ref):
    rc = cfg.tc // S5_CHUNK
    lc = cfg.ctx_n // S5_CHUNK
    ll = cfg.lat_n // S5_CHUNK
    r_all = cfg.t // S5_CHUNK
    w = _dot(u, w_mat)
    row =lax.broadcasted_iota(I32, (r_all, LANES), 0)
    is_ctx = row < rc
    pos = jnp.where(is_ctx, row % lc, (row - rc) % ll)
    seg = jnp.where(is_ctx, lc, ll)
    prev_parts = []
    for d in range(2):
        xr = w[:, (2 * d) * LANES:(2 * d + 1) * LANES]
        xi = w[:, (2 * d + 1) * LANES:(2 * d + 2) * LANES]
        ar, ai = a[2 * d:2 * d + 1], a[2 * d + 1:2 * d + 2]
        first = 0 if d == 0 else seg - 1
        h0r = jnp.zeros((r_all, LANES), F32)
        h0i = jnp.zeros((r_all, LANES), F32)
        for b in range(cfg.lat_b):
            sel = (row >= rc + b * ll) & (row < rc + (b + 1) * ll)
            h0r = jnp.where(sel, h0_ref[pp, b, 2 * d:2 * d + 1, :], h0r)
            h0i = jnp.where(sel, h0_ref[pp, b, 2 * d + 1:2 * d + 2, :], h0i)
        at_first = pos == first
        xr = xr + jnp.where(at_first, ar * h0r - ai * h0i, 0.0)
        xi = xi + jnp.where(at_first, ar * h0i + ai * h0r, 0.0)
        pr, pi = ar, ai
        step = 1
        while step < max(lc, ll):
            if d == 0:
                sr, si = pltpu.roll(xr, step, 0), pltpu.roll(xi, step, 0)
                ok = (pos >= step)
            else:
                sr, si = pltpu.roll(xr, r_all - step, 0), pltpu.roll(xi, r_all - step, 0)
                ok = (pos < seg - step)
            xr, xi = (xr + jnp.where(ok, pr * sr - pi * si, 0.0),
                      xi + jnp.where(ok, pr * si + pi * sr, 0.0))
            pr, pi = pr * pr - pi * pi, 2.0 * pr * pi
            step *= 2
        if d == 0:
            nr, ni = pltpu.roll(xr, 1, 0), pltpu.roll(xi, 1, 0)
        else:
            nr, ni = pltpu.roll(xr, r_all - 1, 0), pltpu.roll(xi, r_all - 1, 0)
        prev_parts += [jnp.where(at_first, h0r, nr), jnp.where(at_first, h0i, ni)]
        xs_ref[2 * d] = xr
        xs_ref[2 * d + 1] = xi
    xin = jnp.concatenate(prev_parts, axis=1).astype(BF16)
    for d in range(2):
        start = lc - 1 if d == 0 else 0
        for k in range(2):
            fin_ref[pp, 2 * d + k] = xs_ref[2 * d + k, pl.ds(start, cfg.ctx_b, stride=lc), :]
    return _dot(u, m_mat) + _dot(xin, g_mat)


def _s5_toeplitz(kf, kb):
    sw = 2 * S5_GROUP
    wide = S5_CHUNK * sw
    lane = lax.broadcasted_iota(I32, (sw, wide), 1)
    blocks = []
    for s in range(S5_CHUNK):
        f = kf if s == 0 else pltpu.roll(kf, sw * s, 1)
        back = (wide - sw * (S5_CHUNK - 1 - s)) % wide
        b = kb if back == 0 else pltpu.roll(kb, back, 1)
        blocks.append(jnp.where(lane >= sw * s, f, 0.0) + jnp.where(lane < sw * (s + 1), b, 0.0))
    return jnp.concatenate(blocks, axis=0).astype(BF16)


def _s5_kernel(cfg, u_ref, w_ref, kf_ref, kb_ref, g_ref, a_ref, h0_ref, y_ref, fin_ref,
               uflat_ref, ystage_ref, xs_ref):
    r_all = cfg.t // S5_CHUNK
    sw = 2 * S5_GROUP
    for s in range(S5_CHUNK):
        blk = u_ref[pl.ds(s, r_all, stride=S5_CHUNK), :]
        for pp in range(S5_QUAD):
            uflat_ref[pp, :, s * sw:(s + 1) * sw] = blk[:, pp * sw:(pp + 1) * sw]
    for pp in range(S5_QUAD):
        y = _s5_pair(cfg, uflat_ref[pp].astype(BF16), w_ref[pp], _s5_toeplitz(kf_ref[pp], kb_ref[pp]),
                     g_ref[pp], a_ref[pp], h0_ref, pp, fin_ref, xs_ref)
        for s in range(S5_CHUNK):
            ystage_ref[s, :, pp * sw:(pp + 1) * sw] = y[:, s * sw:(s + 1) * sw]
    for s in range(S5_CHUNK):
        y_ref[pl.ds(s, r_all, stride=S5_CHUNK), :] = ystage_ref[s]


def s5_mix(cfg, proj, wcat, kc_f, kc_b, gmat, a16, h0):
    r_all = cfg.t // S5_CHUNK
    wide = 2 * S5_CHUNK * S5_GROUP
    mat = pl.BlockSpec((S5_QUAD, wide, wide), lambda q: (q, 0, 0))
    lag = pl.BlockSpec((S5_QUAD, 2 * S5_GROUP, wide), lambda q: (q, 0, 0))
    return pl.pallas_call(
        functools.partial(_s5_kernel, cfg),
        grid=(S5_PAIRS // S5_QUAD,),
        in_specs=[pl.BlockSpec((cfg.t, LANES), lambda q: (0, q)), mat, lag, lag, mat,
                  pl.BlockSpec((S5_QUAD, 4, LANES), lambda q: (q, 0, 0)),
                  pl.BlockSpec((S5_QUAD, cfg.lat_b, 4, LANES), lambda q: (q, 0, 0, 0))],
        out_specs=[pl.BlockSpec((cfg.t, LANES), lambda q: (0, q)),
                   pl.BlockSpec((S5_QUAD, 4, cfg.ctx_b, LANES), lambda q: (q, 0, 0, 0))],
        out_shape=[jax.ShapeDtypeStruct((cfg.t, S5_WIDTH), F32),
                   jax.ShapeDtypeStruct((S5_PAIRS, 4, cfg.ctx_b, LANES), F32)],
        scratch_shapes=[pltpu.VMEM((S5_QUAD, r_all, wide), F32),
                        pltpu.VMEM((S5_CHUNK, r_all, LANES), F32),
                        pltpu.VMEM((4, r_all, LANES), F32)],
        compiler_params=_cp(("parallel",)),
        name="s5_mix",
    )(proj, wcat, kc_f, kc_b, gmat, a16, h0)


def _rope(x, cos, sin_signed, half):
    w = x.shape[-1]
    lane = lax.broadcasted_iota(I32, x.shape, x.ndim - 1)
    swapped = jnp.where((lane % (2 * half)) < half,
                        pltpu.roll(x, w - half, x.ndim - 1), pltpu.roll(x, half, x.ndim - 1))
    return x * cos + swapped * sin_signed


def _attn_kernel(n_parts, latent, n_q_blocks, sink_ref, q_ref, *refs):
    hd = ATT_HEAD_DIM
    scale = hd ** -0.5
    if latent:
        (kp_ref, kc_ref, kn_ref, vp_ref, vc_ref, vn_ref, ck_ref, cv_ref,
         cq_ref, sq_ref, ckp_ref, skp_ref, ckc_ref, skc_ref, ckn_ref, skn_ref, o_ref) = refs
        i = pl.program_id(1)
        q = _rope(q_ref[...], cq_ref[...], sq_ref[...], hd // 2)
        kparts = [_rope(kp_ref[...], ckp_ref[...], skp_ref[...], hd // 2),
                  _rope(kc_ref[...], ckc_ref[...], skc_ref[...], hd // 2),
                  _rope(kn_ref[...], ckn_ref[...], skn_ref[...], hd // 2)]
        kband = jnp.concatenate(kparts, axis=0)
        vband = jnp.concatenate([vp_ref[...], vc_ref[...], vn_ref[...]], axis=0)
        keys = [kband, ck_ref[...]]
        vals = [vband, cv_ref[...]]
        nrow = ATT_GROUP * LANES
        qpos = lax.broadcasted_iota(I32, (nrow, 3 * LANES), 0) % LANES
        kpos = lax.broadcasted_iota(I32, (nrow, 3 * LANES), 1) - LANES
        kabs = kpos + i * LANES
        band_ok = (jnp.abs(kpos - qpos) <= WINDOW) & (kabs >= 0) & (kabs < n_q_blocks * LANES)
        masks = [band_ok, None]
    else:
        k_ref, v_ref, o_ref = refs
        q = q_ref[...]
        keys = [k_ref[...]]
        vals = [v_ref[...]]
        masks = [None]
    for kv in range(ATT_KV_HEADS):
        qs = jnp.concatenate([q[:, (kv * ATT_GROUP + g) * hd:(kv * ATT_GROUP + g + 1) * hd]
                              for g in range(ATT_GROUP)], axis=0).astype(BF16)
        sink_col = jnp.concatenate([jnp.full((LANES, 1), sink_ref[kv * ATT_GROUP + g], F32)
                                    for g in range(ATT_GROUP)], axis=0)
        logits = []
        mx = sink_col
        for kk, msk in zip(keys, masks):
            s = _nt(qs, kk[:, kv * hd:(kv + 1) * hd].astype(BF16)) * scale
            if msk is not None:
                s = jnp.where(msk, s, -1e30)
            logits.append(s)
            mx = jnp.maximum(mx, jnp.max(s, axis=-1, keepdims=True))
        den = jnp.exp(sink_col - mx)
        acc = None
        for s, vv in zip(logits, vals):
            p = jnp.exp(s - mx)
            den = den + jnp.sum(p, axis=-1, keepdims=True)
            o = _dot(p.astype(BF16), vv[:, kv * hd:(kv + 1) * hd].astype(BF16))
            acc = o if acc is None else acc + o
        out = acc / den
        for g in range(ATT_GROUP):
            h = kv * ATT_GROUP + g
            o_ref[:, h * hd:(h + 1) * hd] = out[g * LANES:(g + 1) * LANES, :]


def attention(cfg, proj, sink, cache_k, cache_v, rope_cs):
    qcol = S5_WIDTH // 512
    kcol = (S5_WIDTH + 512) // LANES
    vcol = kcol + 1
    smem = pl.BlockSpec(memory_space=pltpu.SMEM)
    nqc = cfg.ctx_n // LANES
    y_ctx = pl.pallas_call(
        functools.partial(_attn_kernel, 1, False, nqc),
        grid=(cfg.ctx_b, nqc),
        in_specs=[smem,
                  pl.BlockSpec((LANES, 512), lambda b, i: (b * nqc + i, qcol)),
                  pl.BlockSpec((cfg.ctx_n, LANES), lambda b, i: (b, kcol)),
                  pl.BlockSpec((cfg.ctx_n, LANES), lambda b, i: (b, vcol))],
        out_specs=pl.BlockSpec((LANES, 512), lambda b, i: (b * nqc + i, 0)),
        out_shape=jax.ShapeDtypeStruct((cfg.tc, 512), F32),
        compiler_params=_cp(("parallel", "parallel")),
        name="attn_context",
    )(sink, proj, proj, proj)
    nql = cfg.lat_n // LANES
    off = cfg.tc // LANES
    cos_t, sin_t = rope_cs

    def rb(b, i):
        return off + b * nql + i

    def prev(i):
        return jnp.maximum(i - 1, 0)

    def nxt(i):
        return jnp.minimum(i + 1, nql - 1)

    kspec = lambda f, col: pl.BlockSpec((LANES, LANES), lambda b, i: (rb(b, f(i)), col))
    tspec = lambda f: pl.BlockSpec((LANES, LANES), lambda b, i: (f(i), 0))
    same = lambda i: i
    y_lat = pl.pallas_call(
        functools.partial(_attn_kernel, 2, True, nql),
        grid=(cfg.lat_b, nql),
        in_specs=[smem,
                  pl.BlockSpec((LANES, 512), lambda b, i: (rb(b, i), qcol)),
                  kspec(prev, kcol), kspec(same, kcol), kspec(nxt, kcol),
                  kspec(prev, vcol), kspec(same, vcol), kspec(nxt, vcol),
                  pl.BlockSpec((None, cfg.past, LANES), lambda b, i: (b, 0, 0)),
                  pl.BlockSpec((None, cfg.past, LANES), lambda b, i: (b, 0, 0)),
                  pl.BlockSpec((LANES, 512), lambda b, i: (i, 0)),
                  pl.BlockSpec((LANES, 512), lambda b, i: (i, 0)),
                  tspec(prev), tspec(prev), tspec(same), tspec(same), tspec(nxt), tspec(nxt)],
        out_specs=pl.BlockSpec((LANES, 512), lambda b, i: (b * nql + i, 0)),
        out_shape=jax.ShapeDtypeStruct((cfg.lat_b * cfg.lat_n, 512), F32),
        compiler_params=_cp(("parallel", "parallel")),
        name="attn_latent",
    )(sink, proj, proj, proj, proj, proj, proj, proj, cache_k, cache_v,
      cos_t, sin_t, cos_t, sin_t, cos_t, sin_t, cos_t, sin_t)
    return jnp.concatenate([y_ctx, y_lat], axis=0)


def _rope_tables(n, head, width):
    rows = jnp.repeat(jnp.arange(n // GRID_W, dtype=F32), GRID_W)
    cols = (jnp.arange(n) % GRID_W).astype(F32)
    nf = head // 4
    inv = ROPE_BASE ** (-jnp.arange(nf, dtype=F32) / nf)
    ang = jnp.concatenate([rows[:, None] * inv, cols[:, None] * inv], axis=-1)
    c, s = jnp.cos(ang), jnp.sin(ang)
    cos_h = jnp.concatenate([c, c], axis=-1)
    sin_h = jnp.concatenate([-s, s], axis=-1)
    reps = width // head
    return jnp.tile(cos_h, (1, reps)), jnp.tile(sin_h, (1, reps))


def _even_out_kernel(x_ref, u_ref, yd_ref, ya_ref, d_ref, glu_ref, wo_ref, g_ref, o_ref):
    y = jax.nn.gelu(u_ref[...] * d_ref[...] + yd_ref[...])
    z = y * jax.nn.sigmoid(_dot(y.astype(BF16), glu_ref[...]))
    cat = jnp.concatenate([z, ya_ref[...]], axis=-1).astype(BF16)
    o_ref[...] = x_ref[...] + g_ref[...] * _dot(cat, wo_ref[...])


def even_out(cfg, x, proj, yd, yatt, s5_d, glu_bf16, wo_bf16, mod5, layer):
    tile = lambda w: pl.BlockSpec((ROW_TILE, w), lambda i: (i, 0))
    return pl.pallas_call(
        _even_out_kernel,
        grid=(cfg.t // ROW_TILE,),
        in_specs=[tile(D_MODEL), tile(S5_WIDTH), tile(S5_WIDTH), tile(512),
                  pl.BlockSpec((1, S5_WIDTH), lambda i: (0, 0)),
                  pl.BlockSpec((S5_WIDTH, S5_WIDTH), lambda i: (0, 0)),
                  pl.BlockSpec((D_MODEL, D_MODEL), lambda i: (0, 0)),
                  _mod_spec(cfg, layer, 2)],
        out_specs=tile(D_MODEL),
        out_shape=jax.ShapeDtypeStruct((cfg.t, D_MODEL), F32),
        compiler_params=_cp(("parallel",)),
        name="even_out",
    )(x, proj, yd, yatt, s5_d.reshape(1, S5_WIDTH), glu_bf16, wo_bf16, mod5)


REC_CHUNK = 128
REC_LEVELS = 7


def _rec_tables():
    c = REC_CHUNK
    t = np.arange(c)[:, None]
    u = np.arange(c)[None, :]
    out = np.zeros((2, (REC_LEVELS + 2) * c, c), np.float32)
    for d in range(2):
        for lv in range(REC_LEVELS):
            h = 1 << lv
            mid = (t // (2 * h)) * (2 * h) + h
            second = (t % (2 * h)) >= h
            if d == 0:
                a = np.where(second, (u >= mid) & (u <= t), (u > t) & (u < mid))
            else:
                a = np.where(second, (u >= mid) & (u < t), (u >= t) & (u < mid))
            out[d, lv * c:(lv + 1) * c] = a
        if d == 0:
            out[d, REC_LEVELS * c:(REC_LEVELS + 1) * c] = u <= t
            out[d, (REC_LEVELS + 1) * c:] = u > t
        else:
            out[d, REC_LEVELS * c:(REC_LEVELS + 1) * c] = u >= t
            out[d, (REC_LEVELS + 1) * c:] = u < t
    return out


def _rec_kernel(cfg, layer_o, *refs):
    fwd_in, bwd_in = refs[0:6], refs[6:12]
    amat_ref, lbl_ref, rdec_ref = refs[12:15]
    rope_f, rope_b = refs[15:17], refs[17:19]
    s0_f, s0_b, o_f, o_b, st_f, st_b = refs[19:25]
    _rec_direction(cfg, layer_o, 0, *fwd_in, amat_ref, lbl_ref, rdec_ref, *rope_f, s0_f, o_f, st_f)
    _rec_direction(cfg, layer_o, 1, *bwd_in, amat_ref, lbl_ref, rdec_ref, *rope_b, s0_b, o_b, st_b)


def _rec_direction(cfg, layer_o, d, q_ref, z_ref, v_ref, rq_ref, rk_ref, rv_ref,
                   amat_ref, lbl_ref, rdec_ref, cos_ref, sin_ref, s0_ref, o_ref, st_ref):
    c = REC_CHUNK
    step = pl.program_id(0)
    n_chunks = cfg.t // c
    chunk = step if d == 0 else n_chunks - 1 - step
    cpc = cfg.ctx_n // c
    cpl = cfg.lat_n // c
    n_ctx_chunks = cfg.tc // c
    is_lat = chunk >= n_ctx_chunks
    pos = jnp.where(is_lat, (chunk - n_ctx_chunks) % cpl, chunk % cpc)
    last = jnp.where(is_lat, cpl - 1, cpc - 1)
    first_processed = (pos == 0) if d == 0 else (pos == last)

    @pl.when(first_processed & jnp.logical_not(is_lat))
    def _():
        st_ref[...] = jnp.zeros_like(st_ref)

    @pl.when(first_processed & is_lat)
    def _():
        st_ref[...] = s0_ref[...]

    row = lax.broadcasted_iota(I32, (c, c), 0)
    col = lax.broadcasted_iota(I32, (c, c), 1)
    fwd = d == 0
    amat = amat_ref[d]

    def mix_head(q, k, v, scores, e_q, e_k, hidx):
        o = _dot(scores.astype(BF16), v.astype(BF16))
        st = st_ref[hidx]
        o = o + _nt((q * e_q).astype(BF16), st.astype(BF16))
        kd = (k * e_k).astype(BF16)
        total = jnp.where(fwd, e_q[c - 1:c], e_q[0:1])
        st_ref[hidx] = st * total + _dot(v.T.astype(BF16), kd)
        return o

    def hier_scores(q, k, e):
        scores = jnp.where(row == col, _nt(q.astype(BF16), k.astype(BF16)), 0.0)
        for lv in range(REC_LEVELS):
            h = 1 << lv
            el = e[lv * c:(lv + 1) * c]
            second = (row % (2 * h)) >= h
            is_q = second == fwd
            qt = jnp.where(is_q, q * el, 0.0).astype(BF16)
            kt = jnp.where(is_q, 0.0, k * el).astype(BF16)
            same = (row // (2 * h)) == (col // (2 * h))
            scores = scores + jnp.where(same, _nt(qt, kt), 0.0)
        return scores

    lbl = lbl_ref[...]
    mx = jnp.max(lbl, axis=0, keepdims=True)
    ex = jnp.exp(lbl - mx)
    sm = ex / jnp.sum(ex, axis=0, keepdims=True)
    lb_all = jnp.zeros_like(sm[0])
    for i in range(1, layer_o + 1):
        lb_all = lb_all + sm[i]
    lb = jnp.where(fwd, lb_all[0:1], lb_all[1:2])
    z = z_ref[...]
    log_sig = jnp.minimum(z, 0.0) - jnp.log1p(jnp.exp(-jnp.abs(z)))
    a_ = jnp.log1p(-lb) + log_sig
    b_ = jnp.log(lb)
    big = jnp.maximum(a_, b_)
    logf = big + jnp.log1p(jnp.exp(-jnp.abs(a_ - b_)))
    kh = (1.0 - lb) * jax.nn.sigmoid(-z)
    l1 = logf.astype(BF16)
    r1 = logf - l1.astype(F32)
    l2 = r1.astype(BF16)
    l3 = (r1 - l2.astype(F32)).astype(BF16)
    wide = REC_HEADS * HEAD_DIM
    dsum_all = _dot(amat, jnp.concatenate([l1, l2, l3], axis=1))
    dsum_all = dsum_all[:, :wide] + dsum_all[:, wide:2 * wide] + dsum_all[:, 2 * wide:]
    qh, vh = q_ref[...], v_ref[...]
    outs = []
    for h in range(REC_HEADS):
        sl = slice(h * HEAD_DIM, (h + 1) * HEAD_DIM)
        e = jnp.exp(dsum_all[:, sl])
        q, k = qh[:, sl], kh[:, sl]
        outs.append(mix_head(q, k, vh[:, sl], hier_scores(q, k, e),
                             e[REC_LEVELS * c:(REC_LEVELS + 1) * c], e[(REC_LEVELS + 1) * c:], h))
    rq, rk = rq_ref[...], rk_ref[...]
    rq_rot = _rope(rq, cos_ref[...], sin_ref[...], HEAD_DIM // 2)
    rk_rot = _rope(rk, cos_ref[...], sin_ref[...], HEAD_DIM // 2)
    rq = jnp.where(is_lat, rq_rot, rq)
    rk = jnp.where(is_lat, rk_rot, rk) * (HEAD_DIM ** -0.5)
    rv = rv_ref[...]
    gam = jnp.exp(jnp.where(fwd, rdec_ref[0:1], rdec_ref[1:2]))
    lag = jnp.where(fwd, row - col, col - row)
    lag_f = jnp.maximum(lag, 0).astype(F32)
    steps_q = jnp.where(fwd, row + 1, c - row).astype(F32)
    steps_k = jnp.where(fwd, c - 1 - row, row).astype(F32)
    for h in range(REC_HEADS):
        sl = slice(h * HEAD_DIM, (h + 1) * HEAD_DIM)
        g = gam[:, sl]
        q, k = rq[:, sl], rk[:, sl]
        scores = jnp.where(lag >= 0, _nt(q.astype(BF16), k.astype(BF16)) * jnp.exp(-g * lag_f), 0.0)
        outs.append(mix_head(q, k, rv[:, sl], scores, jnp.exp(-g * steps_q), jnp.exp(-g * steps_k),
                             REC_HEADS + h))
    o_ref[...] = jnp.concatenate(outs, axis=1)


def recurrences(cfg, layer_o, proj, amat, lb_logits, ret_decay_lanes, rope_cs, s0):
    c = REC_CHUNK
    n_chunks = cfg.t // c
    n_ctx_chunks = cfg.tc // c
    cpc, cpl = cfg.ctx_n // c, cfg.lat_n // c
    n_seq = cfg.ctx_b + cfg.lat_b

    def chunk_of(d, s):
        return s if d == 0 else n_chunks - 1 - s

    def seq_of(d, s):
        ch = chunk_of(d, s)
        return jnp.where(ch >= n_ctx_chunks, cfg.ctx_b + (ch - n_ctx_chunks) // cpl, ch // cpc)

    def lat_of(d, s):
        return jnp.maximum(seq_of(d, s) - cfg.ctx_b, 0)

    def lat_pos(d, s):
        ch = chunk_of(d, s)
        return jnp.where(ch >= n_ctx_chunks, (ch - n_ctx_chunks) % cpl, 0)

    def col(d, k):
        return pl.BlockSpec((c, 512), lambda s: (chunk_of(d, s), k))

    def cols(d):
        return [col(d, 0), col(d, 1 + d), col(d, 3), col(d, 5), col(d, 6), col(d, 7)]

    def rope(d):
        return [pl.BlockSpec((c, 512), lambda s: (lat_pos(d, s), 0))] * 2

    def state_in(d):
        return pl.BlockSpec((None, None, 2 * REC_HEADS, HEAD_DIM, HEAD_DIM), lambda s: (d, lat_of(d, s), 0, 0, 0))

    def state_out(d):
        return pl.BlockSpec((None, 2 * REC_HEADS, HEAD_DIM, HEAD_DIM), lambda s: (seq_of(d, s), 0, 0, 0))

    nrow = (REC_LEVELS + 2) * c
    cos_t, sin_t = rope_cs
    o_shape = jax.ShapeDtypeStruct((cfg.t, D_MODEL), F32)
    st_shape = jax.ShapeDtypeStruct((n_seq, 2 * REC_HEADS, HEAD_DIM, HEAD_DIM), F32)
    return pl.pallas_call(
        functools.partial(_rec_kernel, cfg, layer_o),
        grid=(n_chunks,),
        in_specs=cols(0) + cols(1) + [
            pl.BlockSpec((2, nrow, c), lambda s: (0, 0, 0)),
            pl.BlockSpec((N_ODD, 2, 512), lambda s: (0, 0, 0)),
            pl.BlockSpec((None, 2, 512), lambda s: (0, 0, 0))] + rope(0) + rope(1) + [state_in(0), state_in(1)],
        out_specs=[pl.BlockSpec((c, D_MODEL), lambda s: (chunk_of(0, s), 0)),
                   pl.BlockSpec((c, D_MODEL), lambda s: (chunk_of(1, s), 0)),
                   state_out(0), state_out(1)],
        out_shape=[o_shape, o_shape, st_shape, st_shape],
        compiler_params=_cp(("arbitrary",)),
        name="recurrences",
    )(*([proj] * 12), amat, lb_logits, ret_decay_lanes, cos_t, sin_t, cos_t, sin_t, s0, s0)


def _odd_out_kernel(x_ref, of_ref, ob_ref, cg_ref, rg_ref, hw_ref, rw_ref, wo_ref, g_ref, o_ref):
    o = of_ref[...] + ob_ref[...]
    hw, rw = hw_ref[...], rw_ref[...]
    parts = []
    for h in range(REC_HEADS):
        sl = slice(h * HEAD_DIM, (h + 1) * HEAD_DIM)
        oh = o[:, sl]
        parts.append(oh * lax.rsqrt(jnp.mean(oh * oh, -1, keepdims=True) + EPS) * hw[:, sl])
    oc = jnp.concatenate(parts, axis=1) * jax.nn.sigmoid(cg_ref[...])
    parts = []
    for h in range(REC_HEADS):
        sl = slice(h * HEAD_DIM, (h + 1) * HEAD_DIM)
        oh = o[:, 512 + h * HEAD_DIM:512 + (h + 1) * HEAD_DIM]
        oh = oh - jnp.mean(oh, -1, keepdims=True)
        parts.append(oh * lax.rsqrt(jnp.mean(oh * oh, -1, keepdims=True) + EPS) * rw[:, sl])
    rg = rg_ref[...]
    orr = jnp.concatenate(parts, axis=1) * (rg * jax.nn.sigmoid(rg))
    cat = jnp.concatenate([oc, orr], axis=1).astype(BF16)
    o_ref[...] = x_ref[...] + g_ref[...] * _dot(cat, wo_ref[...])


def odd_out(cfg, x, o_f, o_b, proj, hgrn_norm_w, ret_norm_w, wo_bf16, mod5, layer):
    tile = lambda w, k=0: pl.BlockSpec((ROW_TILE, w), lambda i: (i, k))
    return pl.pallas_call(
        _odd_out_kernel,
        grid=(cfg.t // ROW_TILE,),
        in_specs=[tile(D_MODEL), tile(D_MODEL), tile(D_MODEL),
                  tile(512, 4), tile(512, 8),
                  pl.BlockSpec((1, 512), lambda i: (0, 0)), pl.BlockSpec((1, 512), lambda i: (0, 0)),
                  pl.BlockSpec((D_MODEL, D_MODEL), lambda i: (0, 0)),
                  _mod_spec(cfg, layer, 2)],
        out_specs=tile(D_MODEL),
        out_shape=jax.ShapeDtypeStruct((cfg.t, D_MODEL), F32),
        compiler_params=_cp(("parallel",)),
        name="odd_out",
    )(x, o_f, o_b, proj, proj, hgrn_norm_w.reshape(1, 512), ret_norm_w.reshape(1, 512), wo_bf16, mod5)


def _norm_logits_kernel(x_ref, nw_ref, sh_ref, sc_ref, wr_ref, h_ref, lg_ref):
    h = (_rms(x_ref[...], nw_ref[...]) * (1.0 + sc_ref[...]) + sh_ref[...]).astype(BF16)
    h_ref[...] = h
    lg_ref[...] = _nt(wr_ref[...], h)


def moe_norm_logits(cfg, x, nw, mod5, layer, wr_t_bf16):
    return pl.pallas_call(
        _norm_logits_kernel,
        grid=(cfg.t // ROW_TILE,),
        in_specs=[pl.BlockSpec((ROW_TILE, D_MODEL), lambda i: (i, 0)),
                  pl.BlockSpec((1, D_MODEL), lambda i: (0, 0)),
                  _mod_spec(cfg, layer, 3), _mod_spec(cfg, layer, 4),
                  pl.BlockSpec((N_EXPERTS, D_MODEL), lambda i: (0, 0))],
        out_specs=[pl.BlockSpec((ROW_TILE, D_MODEL), lambda i: (i, 0)),
                   pl.BlockSpec((N_EXPERTS, ROW_TILE), lambda i: (0, i))],
        out_shape=[jax.ShapeDtypeStruct((cfg.t, D_MODEL), BF16),
                   jax.ShapeDtypeStruct((N_EXPERTS, cfg.t), F32)],
        compiler_params=_cp(("parallel",)),
        name="moe_norm_logits",
    )(x, nw.reshape(1, D_MODEL), mod5, mod5, wr_t_bf16)


def _route_kernel(cfg, lg_ref, pos_ref, aff_ref, cj_ref):
    gs = cfg.gs
    g = pl.program_id(0)
    logits = lg_ref[...]
    mx = jnp.max(logits, axis=0, keepdims=True)
    ex = jnp.exp(logits - mx)
    aff = ex / jnp.sum(ex, axis=0, keepdims=True)
    aff_ref[...] = aff

    def seg_sum(x, seg):
        parts = []
        for r in range(gs // seg):
            s = jnp.sum(x[:, r * seg:(r + 1) * seg], axis=1, keepdims=True)
            parts.append(jnp.broadcast_to(s, (N_EXPERTS, seg)))
        return parts[0] if len(parts) == 1 else jnp.concatenate(parts, axis=1)

    tri = (lax.broadcasted_iota(I32, (LANES, LANES), 0) < lax.broadcasted_iota(I32, (LANES, LANES), 1)).astype(BF16)
    lane = lax.broadcasted_iota(I32, (N_EXPERTS, LANES), 1)

    def prefix(mask, seg):
        parts = []
        carry = jnp.zeros((N_EXPERTS, 1), F32)
        starts = jnp.zeros((N_EXPERTS, LANES), F32)
        for j in range(gs // LANES):
            if (j * LANES) % seg == 0:
                carry = jnp.zeros((N_EXPERTS, 1), F32)
            m = mask[:, j * LANES:(j + 1) * LANES]
            starts = starts + jnp.where(lane == j, carry, 0.0)
            parts.append(_dot(m.astype(BF16), tri) + carry)
            carry = carry + jnp.sum(m, axis=1, keepdims=True)
        starts = starts + jnp.where(lane == gs // LANES, carry, 0.0)
        return jnp.concatenate(parts, axis=1), starts

    def select(seg, cap):
        def at_least_cap(mid):
            return seg_sum((aff >= mid).astype(F32), seg) >= cap

        def halve_bits(_, lohi):
            lo, hi = lohi
            mid = lo + ((hi - lo) >> 1)
            ok = at_least_cap(pltpu.bitcast(mid, F32))
            return jnp.where(ok, mid, lo), jnp.where(ok, hi, mid)

        def halve_value(_, lohi):
            lo, hi = lohi
            mid = 0.5 * (lo + hi)
            ok = at_least_cap(mid)
            return jnp.where(ok, mid, lo), jnp.where(ok, hi, mid)

        lo0 = jnp.zeros((N_EXPERTS, gs), I32)
        hi0 = jnp.full((N_EXPERTS, gs), F32_INF_BITS, I32)
        lo_b, hi_b = lax.fori_loop(0, 31, halve_bits, (lo0, hi0))
        lo_f = jnp.where(lo_b < F32_MIN_NORMAL_BITS, 0.0, pltpu.bitcast(lo_b, F32))
        lo_f, hi_f = lax.fori_loop(0, 30, halve_value, (lo_f, pltpu.bitcast(hi_b, F32)))
        above = aff >= hi_f
        tied = jnp.logical_and(aff >= lo_f, aff < hi_f)
        need = cap - seg_sum(above.astype(F32), seg)
        before, _ = prefix(tied.astype(F32), seg)
        return jnp.logical_or(above, jnp.logical_and(tied, before < need))

    def finish(sel):
        self_f = sel.astype(F32)
        slot, starts = prefix(self_f, gs)
        pos_ref[...] = jnp.where(sel, slot.astype(I32), -1)
        cj_ref[...] = starts.astype(I32)

    @pl.when(g == 0)
    def _():
        finish(select(cfg.ctx_n, CAPACITY_FACTOR * cfg.ctx_n // N_EXPERTS))

    @pl.when(g > 0)
    def _():
        finish(select(cfg.lat_n, CAPACITY_FACTOR * cfg.lat_n // N_EXPERTS))


def moe_route(cfg, logits):
    gs = cfg.gs
    return pl.pallas_call(
        functools.partial(_route_kernel, cfg),
        grid=(cfg.groups,),
        in_specs=[pl.BlockSpec((N_EXPERTS, gs), lambda g: (0, g))],
        out_specs=[pl.BlockSpec((None, N_EXPERTS, gs), lambda g: (g, 0, 0)),
                   pl.BlockSpec((None, N_EXPERTS, gs), lambda g: (g, 0, 0)),
                   pl.BlockSpec((None, N_EXPERTS, LANES), lambda g: (g, 0, 0))],
        out_shape=[jax.ShapeDtypeStruct((cfg.groups, N_EXPERTS, gs), I32),
                   jax.ShapeDtypeStruct((cfg.groups, N_EXPERTS, gs), F32),
                   jax.ShapeDtypeStruct((cfg.groups, N_EXPERTS, LANES), I32)],
        compiler_params=_cp(("parallel",)),
        name="moe_route",
    )(logits)


def _slot_block(cfg):
    return min(LANES, cfg.slots)


GATHER_TOKENS = 2 * LANES


def _gather_kernel(cfg, cj_ref, h_ref, pos_ref, aff_ref, x_ref, gate_ref, acc_ref, gacc_ref):
    g, e = pl.program_id(0), pl.program_id(1)
    sb = _slot_block(cfg)
    slots = cfg.slots
    acc_ref[...] = jnp.zeros_like(acc_ref)
    gacc_ref[...] = jnp.zeros_like(gacc_ref)
    base = (g * N_EXPERTS + e) * LANES
    nt = cfg.gs // LANES

    def window(tb):
        return jnp.minimum((cj_ref[base + tb] // SUBLANES) * SUBLANES, slots - sb)

    fits = functools.reduce(jnp.logical_and,
                            [cj_ref[base + tb + 1] - window(tb) <= sb for tb in range(nt)])

    @pl.when(fits)
    def _():
        sub = lax.broadcasted_iota(I32, (sb, LANES), 0)
        tw = GATHER_TOKENS
        for tb2 in range(cfg.gs // tw):
            p2 = pos_ref[pl.ds(e, 1), tb2 * tw:(tb2 + 1) * tw]
            a2 = aff_ref[pl.ds(e, 1), tb2 * tw:(tb2 + 1) * tw]
            for half in range(tw // LANES):
                tb = tb2 * (tw // LANES) + half
                ws = pl.multiple_of(window(tb), SUBLANES)
                p = p2[:, half * LANES:(half + 1) * LANES]
                hit = jnp.broadcast_to(p, (sb, LANES)) == (sub + ws)
                acc_ref[pl.ds(ws, sb), :] += _dot(hit.astype(BF16), h_ref[tb * LANES:(tb + 1) * LANES, :])
                a = a2[:, half * LANES:(half + 1) * LANES]
                gacc_ref[pl.ds(ws, sb), :] += jnp.sum(jnp.where(hit, a, 0.0), axis=1, keepdims=True)

    @pl.when(jnp.logical_not(fits))
    def _():
        tw = GATHER_TOKENS
        sub = lax.broadcasted_iota(I32, (sb, tw), 0)
        for tb in range(cfg.gs // tw):
            s0, s1 = cj_ref[base + tb * (tw // LANES)], cj_ref[base + (tb + 1) * (tw // LANES)]
            for jb in range(slots // sb):
                @pl.when((s1 > s0) & (s0 < (jb + 1) * sb) & (s1 > jb * sb))
                def _():
                    p = pos_ref[pl.ds(e, 1), tb * tw:(tb + 1) * tw]
                    hit = jnp.broadcast_to(p, (sb, tw)) == (sub + jb * sb)
                    acc_ref[jb * sb:(jb + 1) * sb, :] += _dot(hit.astype(BF16), h_ref[tb * tw:(tb + 1) * tw, :])
                    a = aff_ref[pl.ds(e, 1), tb * tw:(tb + 1) * tw]
                    gacc_ref[jb * sb:(jb + 1) * sb, :] += jnp.sum(jnp.where(hit, a, 0.0), axis=1, keepdims=True)

    x_ref[...] = acc_ref[...].astype(BF16)
    gate_ref[...] = gacc_ref[...]


def moe_gather(cfg, cj, h, pos, aff):
    gs, slots = cfg.gs, cfg.slots
    grid_spec = pltpu.PrefetchScalarGridSpec(
        num_scalar_prefetch=1,
        grid=(cfg.groups, N_EXPERTS),
        in_specs=[pl.BlockSpec((gs, D_MODEL), lambda g, e, cj: (g, 0)),
                  pl.BlockSpec((None, N_EXPERTS, gs), lambda g, e, cj: (g, 0, 0)),
                  pl.BlockSpec((None, N_EXPERTS, gs), lambda g, e, cj: (g, 0, 0))],
        out_specs=[pl.BlockSpec((None, slots, D_MODEL), lambda g, e, cj: (e, g, 0)),
                   pl.BlockSpec((None, slots, 1), lambda g, e, cj: (e, g, 0))],
        scratch_shapes=[pltpu.VMEM((slots, D_MODEL), F32), pltpu.VMEM((slots, 1), F32)])
    return pl.pallas_call(
        functools.partial(_gather_kernel, cfg),
        grid_spec=grid_spec,
        out_shape=[jax.ShapeDtypeStruct((N_EXPERTS, cfg.groups * slots, D_MODEL), BF16),
                   jax.ShapeDtypeStruct((N_EXPERTS, cfg.groups * slots, 1), F32)],
        compiler_params=_cp(("parallel", "arbitrary")),
        name="moe_gather",
    )(cj.reshape(-1), h, pos, aff)


def _ffn_kernel(rows, x_ref, gate_ref, w1_ref, w3_ref, w2_ref, y_ref, acc_ref, w1b, w3b, w2b):
    c = pl.program_id(1)
    w1b[...] = w1_ref[...].astype(BF16)
    w3b[...] = w3_ref[...].astype(BF16)
    w2b[...] = w2_ref[...].astype(BF16)
    rt = min(FFN_ROWS, rows)

    def body(i, _):
        r = pl.multiple_of(i * rt, rt)
        x = x_ref[pl.ds(r, rt), :]
        a = _dot(x, w1b[...])
        b = _dot(x, w3b[...])
        hid = (a * jax.nn.sigmoid(a) * b).astype(BF16)
        y = _dot(hid, w2b[...])

        @pl.when(c == 0)
        def _():
            acc_ref[pl.ds(r, rt), :] = y

        @pl.when(c > 0)
        def _():
            acc_ref[pl.ds(r, rt), :] += y

        return 0

    lax.fori_loop(0, rows // rt, body, 0)

    @pl.when(c == pl.num_programs(1) - 1)
    def _():
        y_ref[...] = (acc_ref[...] * gate_ref[...]).astype(BF16)


def moe_ffn(cfg, layer, xg, gate, w1, w3, w2):
    rows = cfg.groups * cfg.slots
    nff = EXPERT_FF // FF_TILE
    return pl.pallas_call(
        functools.partial(_ffn_kernel, rows),
        grid=(N_EXPERTS, nff),
        in_specs=[pl.BlockSpec((None, rows, D_MODEL), lambda e, c: (e, 0, 0)),
                  pl.BlockSpec((None, rows, 1), lambda e, c: (e, 0, 0)),
                  pl.BlockSpec((None, None, D_MODEL, FF_TILE), lambda e, c: (layer, e, 0, c)),
                  pl.BlockSpec((None, None, D_MODEL, FF_TILE), lambda e, c: (layer, e, 0, c)),
                  pl.BlockSpec((None, None, FF_TILE, D_MODEL), lambda e, c: (layer, e, c, 0))],
        out_specs=pl.BlockSpec((None, rows, D_MODEL), lambda e, c: (e, 0, 0)),
        out_shape=jax.ShapeDtypeStruct((N_EXPERTS, rows, D_MODEL), BF16),
        scratch_shapes=[pltpu.VMEM((rows, D_MODEL), F32),
                        pltpu.VMEM((D_MODEL, FF_TILE), BF16), pltpu.VMEM((D_MODEL, FF_TILE), BF16),
                        pltpu.VMEM((FF_TILE, D_MODEL), BF16)],
        compiler_params=_cp(("parallel", "arbitrary")),
        name="moe_ffn",
    )(xg, gate, w1, w3, w2)


def _scatter_kernel(cfg, final, cj_ref, x_ref, pos_ref, y_ref, g_ref, fw_ref, o_ref):
    g, tb = pl.program_id(0), pl.program_id(1)
    sb = _slot_block(cfg)
    win = min(2 * sb, cfg.slots)
    sub = lax.broadcasted_iota(I32, (win, LANES), 0)
    acc = jnp.zeros((LANES, D_MODEL), F32)
    for e in range(N_EXPERTS):
        s0 = cj_ref[(g * N_EXPERTS + e) * LANES + tb]
        ws = pl.multiple_of(jnp.minimum((s0 // sb) * sb, cfg.slots - win), sb)
        hit = jnp.broadcast_to(pos_ref[e:e + 1, :], (win, LANES)) == (sub + ws)
        hit_t = jnp.where(hit, 1.0, 0.0).T.astype(BF16)
        acc = acc + _dot(hit_t, y_ref[e, pl.ds(ws, win), :])
    out = x_ref[...] + g_ref[...] * acc
    if final:
        out = _rms(out, fw_ref[...])
    o_ref[...] = out


def moe_scatter(cfg, layer, cj, x, pos, y, mod5, final_w, final):
    gs, slots = cfg.gs, cfg.slots
    nt = gs // LANES
    grid_spec = pltpu.PrefetchScalarGridSpec(
        num_scalar_prefetch=1,
        grid=(cfg.groups, nt),
        in_specs=[pl.BlockSpec((LANES, D_MODEL), lambda g, tb, cj: (g * nt + tb, 0)),
                  pl.BlockSpec((None, N_EXPERTS, LANES), lambda g, tb, cj: (g, 0, tb)),
                  pl.BlockSpec((N_EXPERTS, slots, D_MODEL), lambda g, tb, cj: (0, g, 0)),
                  pl.BlockSpec((None, None, None, 1, D_MODEL), lambda g, tb, cj: (layer, g, 5, 0, 0)),
                  pl.BlockSpec((1, D_MODEL), lambda g, tb, cj: (0, 0))],
        out_specs=pl.BlockSpec((LANES, D_MODEL), lambda g, tb, cj: (g * nt + tb, 0)))
    return pl.pallas_call(
        functools.partial(_scatter_kernel, cfg, final),
        grid_spec=grid_spec,
        out_shape=jax.ShapeDtypeStruct((cfg.t, D_MODEL), F32),
        compiler_params=_cp(("parallel", "arbitrary")),
        name="moe_scatter",
    )(cj.reshape(-1), x, pos, y, mod5, final_w.reshape(1, D_MODEL))


def _forward(cfg, x_prompt, x_sample, state_s5, cache_k, cache_v, state_hgrn, state_ret, c, c_ctx, p):
    assert cfg.tc == cfg.lat_n and cfg.ctx_n % ROW_TILE == 0 and cfg.lat_n % (GRID_W * 2) == 0
    x = jnp.concatenate([x_prompt.reshape(cfg.tc, D_MODEL), x_sample.reshape(-1, D_MODEL)], axis=0)
    conds = jnp.zeros((8, D_MODEL), F32).at[0].set(c_ctx).at[1:1 + cfg.lat_b].set(c)
    mod = adaln_table(conds, p['mod_w'], p['mod_b'])
    mod5 = mod[:, :cfg.groups].reshape(DEPTH, cfg.groups, 6, 1, D_MODEL)
    rope_att = _rope_tables(cfg.lat_n, ATT_HEAD_DIM, 512)
    rope_ret = _rope_tables(cfg.lat_n, HEAD_DIM, 512)
    amat = jnp.asarray(_rec_tables(), BF16)
    s5_fin, k_out, v_out, hgrn_fin, ret_fin = [], [], [], [], []
    for layer in range(DEPTH):
        if layer % 2 == 0:
            e = layer // 2
            proj = norm_mod_matmul(cfg, x, p['norm1_w'][layer], mod5, layer, p['ab_w_in'][e].astype(BF16))
            s5_tabs = _s5_tables(p['s5_lambda_re'][e], p['s5_lambda_im'][e], p['s5_log_dt'][e],
                                 p['s5_b_re'][e], p['s5_b_im'][e], p['s5_c_re'][e], p['s5_c_im'][e])
            h0 = state_s5[:, e].reshape(cfg.lat_b, 4, S5_PAIRS, 2 * S5_STATE).transpose(2, 0, 1, 3)
            yd, fin = s5_mix(cfg, proj, *s5_tabs, h0)
            s5_fin.append(fin.transpose(2, 1, 0, 3).reshape(cfg.ctx_b, 2, 2, S5_GROUPS, S5_STATE))
            ck = cache_k[:, e].reshape(cfg.lat_b, cfg.past, LANES)
            cv = cache_v[:, e].reshape(cfg.lat_b, cfg.past, LANES)
            yatt = attention(cfg, proj, p['attn_sink'][e], ck, cv, rope_att)
            k_out.append(proj[:cfg.tc, 1024:1152].reshape(cfg.ctx_b, cfg.ctx_n, ATT_KV_HEADS, ATT_HEAD_DIM))
            v_out.append(proj[:cfg.tc, 1152:1280].reshape(cfg.ctx_b, cfg.ctx_n, ATT_KV_HEADS, ATT_HEAD_DIM))
            x = even_out(cfg, x, proj, yd, yatt, p['s5_d'][e], p['s5_glu_w'][e].astype(BF16),
                         p['ab_w_out'][e].astype(BF16), mod5, layer)
        else:
            o = layer // 2
            proj = norm_mod_matmul(cfg, x, p['norm1_w'][layer], mod5, layer, p['cd_w_in'][o].astype(BF16))
            s0 = jnp.concatenate([state_hgrn[:, o], state_ret[:, o]], axis=2).transpose(1, 0, 2, 4, 3)
            rdec = jnp.repeat(p['ret_decay'][o], HEAD_DIM, axis=-1)
            o_f, o_b, st_f, st_b = recurrences(cfg, o, proj, amat, p['hgrn_lb_logits'], rdec[None], rope_ret, s0)
            st = jnp.stack([st_f[:cfg.ctx_b], st_b[:cfg.ctx_b]], axis=1).transpose(0, 1, 2, 4, 3)
            hgrn_fin.append(st[:, :, :REC_HEADS])
            ret_fin.append(st[:, :, REC_HEADS:])
            x = odd_out(cfg, x, o_f, o_b, proj, p['hgrn_norm_w'][o], p['ret_norm_w'][o],
                        p['cd_w_out'][o].astype(BF16), mod5, layer)
        h2, logits = moe_norm_logits(cfg, x, p['norm2_w'][layer], mod5, layer,
                                     p['router_w'][layer].T.astype(BF16))
        pos, aff, cj = moe_route(cfg, logits)
        xg, gate = moe_gather(cfg, cj, h2, pos, aff)
        yg = moe_ffn(cfg, layer, xg, gate, p['moe_w1'], p['moe_w3'], p['moe_w2'])
        x = moe_scatter(cfg, layer, cj, x, pos, yg, mod5, p['final_norm_w'], layer == DEPTH - 1)
    y_prompt = x[:cfg.tc].reshape(cfg.ctx_b, cfg.ctx_n, D_MODEL)
    y_sample = x[cfg.tc:].reshape(cfg.lat_b, cfg.lat_n, D_MODEL)
    return (y_prompt, y_sample, jnp.stack(s5_fin, axis=1), jnp.stack(k_out, axis=1), jnp.stack(v_out, axis=1),
            jnp.stack(hgrn_fin, axis=1), jnp.stack(ret_fin, axis=1))


def kernel(x_prompt, x_sample, state_s5, cache_k, cache_v, state_hgrn, state_ret, c, c_ctx, mod_w, mod_b, norm1_w, norm2_w, final_norm_w, ab_w_in, ab_w_out, s5_lambda_re, s5_lambda_im, s5_log_dt, s5_b_re, s5_b_im, s5_c_re, s5_c_im, s5_d, s5_glu_w, attn_sink, cd_w_in, cd_w_out, hgrn_lb_logits, hgrn_norm_w, ret_decay, ret_norm_w, router_w, moe_w1, moe_w3, moe_w2):
    cfg = Cfg(ctx_b=x_prompt.shape[0], ctx_n=x_prompt.shape[1], lat_b=x_sample.shape[0],
              lat_n=x_sample.shape[1], past=cache_k.shape[2])
    p = dict(mod_w=mod_w, mod_b=mod_b, norm1_w=norm1_w, norm2_w=norm2_w, final_norm_w=final_norm_w,
             ab_w_in=ab_w_in, ab_w_out=ab_w_out, s5_lambda_re=s5_lambda_re, s5_lambda_im=s5_lambda_im,
             s5_log_dt=s5_log_dt, s5_b_re=s5_b_re, s5_b_im=s5_b_im, s5_c_re=s5_c_re, s5_c_im=s5_c_im,
             s5_d=s5_d, s5_glu_w=s5_glu_w, attn_sink=attn_sink, cd_w_in=cd_w_in, cd_w_out=cd_w_out,
             hgrn_lb_logits=hgrn_lb_logits, hgrn_norm_w=hgrn_norm_w, ret_decay=ret_decay,
             ret_norm_w=ret_norm_w, router_w=router_w, moe_w1=moe_w1, moe_w3=moe_w3, moe_w2=moe_w2)
    return _forward(cfg, x_prompt, x_sample, state_s5, cache_k, cache_v, state_hgrn, state_ret, c, c_ctx, p)
```

```python
import functools
import math
from typing import NamedTuple

import numpy as np
import jax
import jax.numpy as jnp
from jax import lax
from jax.experimental import pallas as pl
from jax.experimental.pallas import tpu as pltpu

F32 = jnp.float32
BF16 = jnp.bfloat16
I32 = jnp.int32

D_MODEL = 1024
DEPTH = 4
N_EVEN = 2
N_ODD = 2
EPS = 1e-6
GRID_W = 64
S5_WIDTH = 512
S5_GROUP = 16
S5_GROUPS = 32
S5_STATE = 64
S5_CHUNK = 16
S5_PAIRS = S5_GROUPS // 2
S5_QUAD = 4
ATT_HEAD_DIM = 64
ATT_HEADS = 8
ATT_KV_HEADS = 2
ATT_GROUP = 4
WINDOW = 128
ROPE_BASE = 10000.0
AB_IN = 1280
HEAD_DIM = 128
REC_HEADS = 4
CD_IN = 4608
N_EXPERTS = 16
EXPERT_FF = 1536
CAPACITY_FACTOR = 2

LANES = 128
SUBLANES = 8
ROW_TILE = 256
FFN_ROWS = 512
FF_TILE = 512
F32_INF_BITS = 0x7F800000
F32_MIN_NORMAL_BITS = 0x00800000
VMEM_LIMIT = 56 * 1024 * 1024


class Cfg(NamedTuple):
    ctx_b: int
    ctx_n: int
    lat_b: int
    lat_n: int
    past: int

    @property
    def tc(self):
        return self.ctx_b * self.ctx_n

    @property
    def t(self):
        return self.tc + self.lat_b * self.lat_n

    @property
    def gs(self):
        return self.lat_n

    @property
    def groups(self):
        return 1 + self.lat_b

    @property
    def slots(self):
        return CAPACITY_FACTOR * self.gs // N_EXPERTS


def _cp(sem, vmem=VMEM_LIMIT):
    return pltpu.CompilerParams(dimension_semantics=sem, vmem_limit_bytes=vmem)


def _nt(a, b):
    return lax.dot_general(a, b, (((1,), (1,)), ((), ())), preferred_element_type=F32)


def _dot(a, b, precision=None):
    return jnp.dot(a, b, preferred_element_type=F32, precision=precision)


_HI = lax.Precision.HIGHEST


def _adaln_kernel(c_ref, w_ref, b_ref, o_ref):
    c = c_ref[...]
    s = c * jax.nn.sigmoid(c)
    o_ref[0] = _dot(s.astype(BF16), w_ref[0].astype(BF16)) + b_ref[0]


def adaln_table(conds, mod_w, mod_b):
    n6 = 6 * D_MODEL
    tn = 1536
    return pl.pallas_call(
        _adaln_kernel,
        grid=(DEPTH, n6 // tn),
        in_specs=[pl.BlockSpec((8, D_MODEL), lambda l, j: (0, 0)),
                  pl.BlockSpec((1, D_MODEL, tn), lambda l, j: (l, 0, j)),
                  pl.BlockSpec((1, 1, tn), lambda l, j: (l, 0, j))],
        out_specs=pl.BlockSpec((1, 8, tn), lambda l, j: (l, 0, j)),
        out_shape=jax.ShapeDtypeStruct((DEPTH, 8, n6), F32),
        compiler_params=_cp(("parallel", "parallel")),
        name="adaln_table",
    )(conds, mod_w, mod_b.reshape(DEPTH, 1, n6))


def _mod_spec(cfg, layer, part):
    tiles_per_group = cfg.gs // ROW_TILE
    return pl.BlockSpec((None, None, None, 1, D_MODEL),
                        lambda i: (layer, i // tiles_per_group, part, 0, 0))


def _rms(x, w):
    return x * lax.rsqrt(jnp.mean(x * x, axis=-1, keepdims=True) + EPS) * w


def _nmm_kernel(x_ref, nw_ref, sh_ref, sc_ref, w_ref, o_ref):
    h = _rms(x_ref[...], nw_ref[...]) * (1.0 + sc_ref[...]) + sh_ref[...]
    o_ref[...] = _dot(h.astype(BF16), w_ref[...])


def norm_mod_matmul(cfg, x, nw, mod5, layer, w_bf16):
    n_out = w_bf16.shape[1]
    return pl.pallas_call(
        _nmm_kernel,
        grid=(cfg.t // ROW_TILE,),
        in_specs=[pl.BlockSpec((ROW_TILE, D_MODEL), lambda i: (i, 0)),
                  pl.BlockSpec((1, D_MODEL), lambda i: (0, 0)),
                  _mod_spec(cfg, layer, 0), _mod_spec(cfg, layer, 1),
                  pl.BlockSpec((D_MODEL, n_out), lambda i: (0, 0))],
        out_specs=pl.BlockSpec((ROW_TILE, n_out), lambda i: (i, 0)),
        out_shape=jax.ShapeDtypeStruct((cfg.t, n_out), F32),
        compiler_params=_cp(("parallel",)),
        name="norm_mod_matmul",
    )(x, nw.reshape(1, D_MODEL), mod5, mod5, w_bf16)


def _s5_tables(lam_re, lam_im, log_dt, b_re, b_im, c_re, c_im):
    L = S5_CHUNK
    lr, li = lam_re.astype(F32), lam_im.astype(F32)
    dt = jnp.exp(log_dt.astype(F32))[..., None]
    mag = jnp.exp(lr * dt)
    ar, ai = mag * jnp.cos(li * dt), mag * jnp.sin(li * dt)
    den = lr * lr + li * li
    cr = ((ar - 1.0) * lr + ai * li) / den
    ci = (ai * lr - (ar - 1.0) * li) / den
    br, bi = b_re.astype(F32), b_im.astype(F32)
    bbr = cr[..., None] * br - ci[..., None] * bi
    bbi = cr[..., None] * bi + ci[..., None] * br

    def pw_step(carry, _):
        pr, pi = carry
        return (pr * ar - pi * ai, pr * ai + pi * ar), (pr, pi)

    (_, _), (pr, pi) = lax.scan(pw_step, (jnp.ones_like(ar), jnp.zeros_like(ar)), None, length=L + 1)
    er = pr[:L, ..., None] * bbr[None] - pi[:L, ..., None] * bbi[None]
    ei = pr[:L, ..., None] * bbi[None] + pi[:L, ..., None] * bbr[None]
    ccr, cci = c_re.astype(F32), c_im.astype(F32)
    kt = (jnp.einsum('dgcp,tdgpe->tdgce', ccr, er, precision=_HI)
          - jnp.einsum('dgcp,tdgpe->tdgce', cci, ei, precision=_HI))
    eye2 = jnp.eye(2, dtype=F32)
    wide = 2 * L * S5_GROUP

    def pairs(a, axis):
        return a.reshape(a.shape[:axis] + (S5_PAIRS, 2) + a.shape[axis + 1:])

    def lag_rows(k):
        k = pairs(k, 1).transpose(1, 2, 4, 0, 3)
        k = k[:, :, :, :, None, :] * eye2[None, :, None, None, :, None]
        return k.reshape(S5_PAIRS, 2 * S5_GROUP, wide)

    kc_f, kc_b = lag_rows(kt[:, 0]), lag_rows(kt[::-1, 1])
    wq = jnp.stack([er[::-1, 0], ei[::-1, 0], er[:, 1], ei[:, 1]], axis=0)
    wq = pairs(wq, 2).transpose(2, 0, 1, 3, 5, 4).astype(BF16).reshape(S5_PAIRS, 4, wide, S5_STATE)
    eye2b = eye2.astype(BF16)

    def g_coef(d, p_r, p_i):
        g_r = ccr[d][None] * p_r[:, :, None, :] - cci[d][None] * p_i[:, :, None, :]
        g_i = -ccr[d][None] * p_i[:, :, None, :] - cci[d][None] * p_r[:, :, None, :]
        return g_r, g_i

    gq = jnp.stack(g_coef(0, pr[1:L + 1, 0], pi[1:L + 1, 0]) + g_coef(1, pr[L:0:-1, 1], pi[L:0:-1, 1]), axis=0)
    gq = pairs(gq, 2).transpose(2, 0, 3, 5, 1, 4).astype(BF16)
    gmat = (gq[:, :, :, :, :, None, :] * eye2b[None, None, :, None, None, :, None]).reshape(S5_PAIRS, wide, wide)

    def pair_lanes(a):
        return a.reshape(S5_PAIRS, 2 * S5_STATE)

    a16 = jnp.stack([pair_lanes(pr[L, 0]), pair_lanes(pi[L, 0]),
                     pair_lanes(pr[L, 1]), pair_lanes(pi[L, 1])], axis=1)
    return wq, kc_f, kc_b, gmat, a16


def _s5_pair(cfg, u, w_mat, m_mat, g_mat, a, h0_ref, pp, fin_ref, xs_ref):
    rc = cfg.tc // S5_CHUNK
    lc = cfg.ctx_n // S5_CHUNK
    ll = cfg.lat_n // S5_CHUNK
    r_all = cfg.t // S5_CHUNK
    w = [_dot(u, w_mat[k].astype(BF16)) for k in range(4)]
    row = lax.broadcasted_iota(I32, (r_all, LANES), 0)
    is_ctx = row < rc
    pos = jnp.where(is_ctx, row % lc, (row - rc) % ll)
    seg = jnp.where(is_ctx, lc, ll)
    prev_parts = []
    for d in range(2):
        xr, xi = w[2 * d], w[2 * d + 1]
        ar, ai = a[2 * d:2 * d + 1], a[2 * d + 1:2 * d + 2]
        first = 0 if d == 0 else seg - 1
        h0r = jnp.zeros((r_all, LANES), F32)
        h0i = jnp.zeros((r_all, LANES), F32)
        for b in range(cfg.lat_b):
            sel = (row >= rc + b * ll) & (row < rc + (b + 1) * ll)
            h0r = jnp.where(sel, h0_ref[pp, b, 2 * d:2 * d + 1, :], h0r)
            h0i = jnp.where(sel, h0_ref[pp, b, 2 * d + 1:2 * d + 2, :], h0i)
        at_first = pos == first
        xr = xr + jnp.where(at_first, ar * h0r - ai * h0i, 0.0)
        xi = xi + jnp.where(at_first, ar * h0i + ai * h0r, 0.0)
        pr, pi = ar, ai
        step = 1
        while step < max(lc, ll):
            if d == 0:
                sr, si = pltpu.roll(xr, step, 0), pltpu.roll(xi, step, 0)
                ok = (pos >= step)
            else:
                sr, si = pltpu.roll(xr, r_all - step, 0), pltpu.roll(xi, r_all - step, 0)
                ok = (pos < seg - step)
            xr, xi = (xr + jnp.where(ok, pr * sr - pi * si, 0.0),
                      xi + jnp.where(ok, pr * si + pi * sr, 0.0))
            pr, pi = pr * pr - pi * pi, 2.0 * pr * pi
            step *= 2
        if d == 0:
            nr, ni = pltpu.roll(xr, 1, 0), pltpu.roll(xi, 1, 0)
        else:
            nr, ni = pltpu.roll(xr, r_all - 1, 0), pltpu.roll(xi, r_all - 1, 0)
        prev_parts += [jnp.where(at_first, h0r, nr), jnp.where(at_first, h0i, ni)]
        xs_ref[2 * d] = xr
        xs_ref[2 * d + 1] = xi
    xin = jnp.concatenate(prev_parts, axis=1).astype(BF16)
    for d in range(2):
        start = lc - 1 if d == 0 else 0
        for k in range(2):
            fin_ref[pp, 2 * d + k] = xs_ref[2 * d + k, pl.ds(start, cfg.ctx_b, stride=lc), :]
    return _dot(u, m_mat) + _dot(xin, g_mat)


def _s5_toeplitz(kf, kb):
    sw = 2 * S5_GROUP
    wide = S5_CHUNK * sw
    lane = lax.broadcasted_iota(I32, (sw, wide), 1)
    blocks = []
    for s in range(S5_CHUNK):
        f = kf if s == 0 else pltpu.roll(kf, sw * s, 1)
        back = (wide - sw * (S5_CHUNK - 1 - s)) % wide
        b = kb if back == 0 else pltpu.roll(kb, back, 1)
        blocks.append(jnp.where(lane >= sw * s, f, 0.0) + jnp.where(lane < sw * (s + 1), b, 0.0))
    return jnp.concatenate(blocks, axis=0).astype(BF16)


def _s5_kernel(cfg, u_ref, w_ref, kf_ref, kb_ref, g_ref, a_ref, h0_ref, y_ref, fin_ref,
               uflat_ref, ystage_ref, xs_ref, wexp_ref):
    r_all = cfg.t // S5_CHUNK
    sw = 2 * S5_GROUP
    wide = S5_CHUNK * sw
    for s in range(S5_CHUNK):
        blk = u_ref[pl.ds(s, r_all, stride=S5_CHUNK), :]
        for pp in range(S5_QUAD):
            uflat_ref[pp, :, s * sw:(s + 1) * sw] = blk[:, pp * sw:(pp + 1) * sw]
    row_gi = (lax.broadcasted_iota(I32, (wide, S5_STATE), 0) // S5_GROUP) % 2
    for pp in range(S5_QUAD):
        for k in range(4):
            wk = w_ref[pp, k].astype(F32)
            for gi in range(2):
                wexp_ref[k, :, gi * S5_STATE:(gi + 1) * S5_STATE] = jnp.where(row_gi == gi, wk, 0.0)
        y = _s5_pair(cfg, uflat_ref[pp].astype(BF16), wexp_ref, _s5_toeplitz(kf_ref[pp], kb_ref[pp]),
                     g_ref[pp], a_ref[pp], h0_ref, pp, fin_ref, xs_ref)
        for s in range(S5_CHUNK):
            ystage_ref[s, :, pp * sw:(pp + 1) * sw] = y[:, s * sw:(s + 1) * sw]
    for s in range(S5_CHUNK):
        y_ref[pl.ds(s, r_all, stride=S5_CHUNK), :] = ystage_ref[s]


def s5_mix(cfg, proj, wq, kc_f, kc_b, gmat, a16, h0):
    r_all = cfg.t // S5_CHUNK
    wide = 2 * S5_CHUNK * S5_GROUP
    mat = pl.BlockSpec((S5_QUAD, wide, wide), lambda q: (q, 0, 0))
    lag = pl.BlockSpec((S5_QUAD, 2 * S5_GROUP, wide), lambda q: (q, 0, 0))
    return pl.pallas_call(
        functools.partial(_s5_kernel, cfg),
        grid=(S5_PAIRS // S5_QUAD,),
        in_specs=[pl.BlockSpec((cfg.t, LANES), lambda q: (0, q)),
                  pl.BlockSpec((S5_QUAD, 4, wide, S5_STATE), lambda q: (q, 0, 0, 0)), lag, lag, mat,
                  pl.BlockSpec((S5_QUAD, 4, LANES), lambda q: (q, 0, 0)),
                  pl.BlockSpec((S5_QUAD, cfg.lat_b, 4, LANES), lambda q: (q, 0, 0, 0))],
        out_specs=[pl.BlockSpec((cfg.t, LANES), lambda q: (0, q)),
                   pl.BlockSpec((S5_QUAD, 4, cfg.ctx_b, LANES), lambda q: (q, 0, 0, 0))],
        out_shape=[jax.ShapeDtypeStruct((cfg.t, S5_WIDTH), F32),
                   jax.ShapeDtypeStruct((S5_PAIRS, 4, cfg.ctx_b, LANES), F32)],
        scratch_shapes=[pltpu.VMEM((S5_QUAD, r_all, wide), F32),
                        pltpu.VMEM((S5_CHUNK, r_all, LANES), F32),
                        pltpu.VMEM((4, r_all, LANES), F32),
                        pltpu.VMEM((4, wide, LANES), F32)],
        compiler_params=_cp(("parallel",)),
        name="s5_mix",
    )(proj, wq, kc_f, kc_b, gmat, a16, h0)


def _rope(x, cos, sin_signed, half):
    w = x.shape[-1]
    lane = lax.broadcasted_iota(I32, x.shape, x.ndim - 1)
    swapped = jnp.where((lane % (2 * half)) < half,
                        pltpu.roll(x, w - half, x.ndim - 1), pltpu.roll(x, half, x.ndim - 1))
    return x * cos + swapped * sin_signed


def _attn_kernel(n_parts, latent, n_q_blocks, sink_ref, q_ref, *refs):
    hd = ATT_HEAD_DIM
    scale = hd ** -0.5
    if latent:
        (kp_ref, kc_ref, kn_ref, vp_ref, vc_ref, vn_ref, ck_ref, cv_ref,
         cq_ref, sq_ref, ckp_ref, skp_ref, ckc_ref, skc_ref, ckn_ref, skn_ref, o_ref) = refs
        i = pl.program_id(1)
        q = _rope(q_ref[...], cq_ref[...], sq_ref[...], hd // 2)
        kparts = [_rope(kp_ref[...], ckp_ref[...], skp_ref[...], hd // 2),
                  _rope(kc_ref[...], ckc_ref[...], skc_ref[...], hd // 2),
                  _rope(kn_ref[...], ckn_ref[...], skn_ref[...], hd // 2)]
        kband = jnp.concatenate(kparts, axis=0)
        vband = jnp.concatenate([vp_ref[...], vc_ref[...], vn_ref[...]], axis=0)
        keys = [kband, ck_ref[...]]
        vals = [vband, cv_ref[...]]
        nrow = ATT_GROUP * LANES
        qpos = lax.broadcasted_iota(I32, (nrow, 3 * LANES), 0) % LANES
        kpos = lax.broadcasted_iota(I32, (nrow, 3 * LANES), 1) - LANES
        kabs = kpos + i * LANES
        band_ok = (jnp.abs(kpos - qpos) <= WINDOW) & (kabs >= 0) & (kabs < n_q_blocks * LANES)
        masks = [band_ok, None]
    else:
        k_ref, v_ref, o_ref = refs
        q = q_ref[...]
        keys = [k_ref[...]]
        vals = [v_ref[...]]
        masks = [None]
    for kv in range(ATT_KV_HEADS):
        qs = jnp.concatenate([q[:, (kv * ATT_GROUP + g) * hd:(kv * ATT_GROUP + g + 1) * hd]
                              for g in range(ATT_GROUP)], axis=0).astype(BF16)
        sink_col = jnp.concatenate([jnp.full((LANES, 1), sink_ref[kv * ATT_GROUP + g], F32)
                                    for g in range(ATT_GROUP)], axis=0)
        logits = []
        mx = sink_col
        for kk, msk in zip(keys, masks):
            s = _nt(qs, kk[:, kv * hd:(kv + 1) * hd].astype(BF16)) * scale
            if msk is not None:
                s = jnp.where(msk, s, -1e30)
            logits.append(s)
            mx = jnp.maximum(mx, jnp.max(s, axis=-1, keepdims=True))
        den = jnp.exp(sink_col - mx)
        acc = None
        for s, vv in zip(logits, vals):
            p = jnp.exp(s - mx)
            den = den + jnp.sum(p, axis=-1, keepdims=True)
            o = _dot(p.astype(BF16), vv[:, kv * hd:(kv + 1) * hd].astype(BF16))
            acc = o if acc is None else acc + o
        out = acc / den
        for g in range(ATT_GROUP):
            h = kv * ATT_GROUP + g
            o_ref[:, h * hd:(h + 1) * hd] = out[g * LANES:(g + 1) * LANES, :]


def attention(cfg, proj, sink, cache_k, cache_v, rope_cs):
    qcol = S5_WIDTH // 512
    kcol = (S5_WIDTH + 512) // LANES
    vcol = kcol + 1
    smem = pl.BlockSpec(memory_space=pltpu.SMEM)
    nqc = cfg.ctx_n // LANES
    y_ctx = pl.pallas_call(
        functools.partial(_attn_kernel, 1, False, nqc),
        grid=(cfg.ctx_b, nqc),
        in_specs=[smem,
                  pl.BlockSpec((LANES, 512), lambda b, i: (b * nqc + i, qcol)),
                  pl.BlockSpec((cfg.ctx_n, LANES), lambda b, i: (b, kcol)),
                  pl.BlockSpec((cfg.ctx_n, LANES), lambda b, i: (b, vcol))],
        out_specs=pl.BlockSpec((LANES, 512), lambda b, i: (b * nqc + i, 0)),
        out_shape=jax.ShapeDtypeStruct((cfg.tc, 512), F32),
        compiler_params=_cp(("parallel", "parallel")),
        name="attn_context",
    )(sink, proj, proj, proj)
    nql = cfg.lat_n // LANES
    off = cfg.tc // LANES
    cos_t, sin_t = rope_cs

    def rb(b, i):
        return off + b * nql + i

    def prev(i):
        return jnp.maximum(i - 1, 0)

    def nxt(i):
        return jnp.minimum(i + 1, nql - 1)

    kspec = lambda f, col: pl.BlockSpec((LANES, LANES), lambda b, i: (rb(b, f(i)), col))
    tspec = lambda f: pl.BlockSpec((LANES, LANES), lambda b, i: (f(i), 0))
    same = lambda i: i
    y_lat = pl.pallas_call(
        functools.partial(_attn_kernel, 2, True, nql),
        grid=(cfg.lat_b, nql),
        in_specs=[smem,
                  pl.BlockSpec((LANES, 512), lambda b, i: (rb(b, i), qcol)),
                  kspec(prev, kcol), kspec(same, kcol), kspec(nxt, kcol),
                  kspec(prev, vcol), kspec(same, vcol), kspec(nxt, vcol),
                  pl.BlockSpec((None, cfg.past, LANES), lambda b, i: (b, 0, 0)),
                  pl.BlockSpec((None, cfg.past, LANES), lambda b, i: (b, 0, 0)),
                  pl.BlockSpec((LANES, 512), lambda b, i: (i, 0)),
                  pl.BlockSpec((LANES, 512), lambda b, i: (i, 0)),
                  tspec(prev), tspec(prev), tspec(same), tspec(same), tspec(nxt), tspec(nxt)],
        out_specs=pl.BlockSpec((LANES, 512), lambda b, i: (b * nql + i, 0)),
        out_shape=jax.ShapeDtypeStruct((cfg.lat_b * cfg.lat_n, 512), F32),
        compiler_params=_cp(("parallel", "parallel")),
        name="attn_latent",
    )(sink, proj, proj, proj, proj, proj, proj, proj, cache_k, cache_v,
      cos_t, sin_t, cos_t, sin_t, cos_t, sin_t, cos_t, sin_t)
    return jnp.concatenate([y_ctx, y_lat], axis=0)


def _rope_tables(n, head, width):
    rows = jnp.repeat(jnp.arange(n // GRID_W, dtype=F32), GRID_W)
    cols = (jnp.arange(n) % GRID_W).astype(F32)
    nf = head // 4
    inv = ROPE_BASE ** (-jnp.arange(nf, dtype=F32) / nf)
    ang = jnp.concatenate([rows[:, None] * inv, cols[:, None] * inv], axis=-1)
    c, s = jnp.cos(ang), jnp.sin(ang)
    cos_h = jnp.concatenate([c, c], axis=-1)
    sin_h = jnp.concatenate([-s, s], axis=-1)
    reps = width // head
    return jnp.tile(cos_h, (1, reps)), jnp.tile(sin_h, (1, reps))


def _even_out_kernel(x_ref, u_ref, yd_ref, ya_ref, d_ref, glu_ref, wo_ref, g_ref, o_ref):
    y = jax.nn.gelu(u_ref[...] * d_ref[...] + yd_ref[...])
    z = y * jax.nn.sigmoid(_dot(y.astype(BF16), glu_ref[...]))
    cat = jnp.concatenate([z, ya_ref[...]], axis=-1).astype(BF16)
    o_ref[...] = x_ref[...] + g_ref[...] * _dot(cat, wo_ref[...])


def even_out(cfg, x, proj, yd, yatt, s5_d, glu_bf16, wo_bf16, mod5, layer):
    tile = lambda w: pl.BlockSpec((ROW_TILE, w), lambda i: (i, 0))
    return pl.pallas_call(
        _even_out_kernel,
        grid=(cfg.t // ROW_TILE,),
        in_specs=[tile(D_MODEL), tile(S5_WIDTH), tile(S5_WIDTH), tile(512),
                  pl.BlockSpec((1, S5_WIDTH), lambda i: (0, 0)),
                  pl.BlockSpec((S5_WIDTH, S5_WIDTH), lambda i: (0, 0)),
                  pl.BlockSpec((D_MODEL, D_MODEL), lambda i: (0, 0)),
                  _mod_spec(cfg, layer, 2)],
        out_specs=tile(D_MODEL),
        out_shape=jax.ShapeDtypeStruct((cfg.t, D_MODEL), F32),
        compiler_params=_cp(("parallel",)),
        name="even_out",
    )(x, proj, yd, yatt, s5_d.reshape(1, S5_WIDTH), glu_bf16, wo_bf16, mod5)


REC_CHUNK = 128
REC_LEVELS = 7


def _rec_tables():
    c = REC_CHUNK
    t = np.arange(c)[:, None]
    u = np.arange(c)[None, :]
    out = np.zeros((2, (REC_LEVELS + 2) * c, c), np.float32)
    for d in range(2):
        for lv in range(REC_LEVELS):
            h = 1 << lv
            mid = (t // (2 * h)) * (2 * h) + h
            second = (t % (2 * h)) >= h
            if d == 0:
                a = np.where(second, (u >= mid) & (u <= t), (u > t) & (u < mid))
            else:
                a = np.where(second, (u >= mid) & (u < t), (u >= t) & (u < mid))
            out[d, lv * c:(lv + 1) * c] = a
        if d == 0:
            out[d, REC_LEVELS * c:(REC_LEVELS + 1) * c] = u <= t
            out[d, (REC_LEVELS + 1) * c:] = u > t
        else:
            out[d, REC_LEVELS * c:(REC_LEVELS + 1) * c] = u >= t
            out[d, (REC_LEVELS + 1) * c:] = u < t
    return out


def _rec_kernel(cfg, layer_o, *refs):
    fwd_in, bwd_in = refs[0:6], refs[6:12]
    amat_ref, lbl_ref, rdec_ref = refs[12:15]
    rope_f, rope_b = refs[15:17], refs[17:19]
    s0_f, s0_b, o_f, o_b, st_f, st_b = refs[19:25]
    _rec_direction(cfg, layer_o, 0, *fwd_in, amat_ref, lbl_ref, rdec_ref, *rope_f, s0_f, o_f, st_f)
    _rec_direction(cfg, layer_o, 1, *bwd_in, amat_ref, lbl_ref, rdec_ref, *rope_b, s0_b, o_b, st_b)


def _rec_direction(cfg, layer_o, d, q_ref, z_ref, v_ref, rq_ref, rk_ref, rv_ref,
                   amat_ref, lbl_ref, rdec_ref, cos_ref, sin_ref, s0_ref, o_ref, st_ref):
    c = REC_CHUNK
    step = pl.program_id(0)
    n_chunks = cfg.t // c
    chunk = step if d == 0 else n_chunks - 1 - step
    cpc = cfg.ctx_n // c
    cpl = cfg.lat_n // c
    n_ctx_chunks = cfg.tc // c
    is_lat = chunk >= n_ctx_chunks
    pos = jnp.where(is_lat, (chunk - n_ctx_chunks) % cpl, chunk % cpc)
    last = jnp.where(is_lat, cpl - 1, cpc - 1)
    first_processed = (pos == 0) if d == 0 else (pos == last)

    @pl.when(first_processed & jnp.logical_not(is_lat))
    def _():
        st_ref[...] = jnp.zeros_like(st_ref)

    @pl.when(first_processed & is_lat)
    def _():
        st_ref[...] = s0_ref[...]

    row = lax.broadcasted_iota(I32, (c, c), 0)
    col = lax.broadcasted_iota(I32, (c, c), 1)
    fwd = d == 0
    amat = amat_ref[d]

    def mix_head(q, k, v, scores, e_q, e_k, hidx):
        o = _dot(scores.astype(BF16), v.astype(BF16))
        st = st_ref[hidx]
        o = o + _nt((q * e_q).astype(BF16), st.astype(BF16))
        kd = (k * e_k).astype(BF16)
        total = jnp.where(fwd, e_q[c - 1:c], e_q[0:1])
        st_ref[hidx] = st * total + _dot(v.T.astype(BF16), kd)
        return o

    def hier_scores(q, k, e):
        scores = jnp.where(row == col, _nt(q.astype(BF16), k.astype(BF16)), 0.0)
        for lv in range(REC_LEVELS):
            h = 1 << lv
            el = e[lv * c:(lv + 1) * c]
            second = (row % (2 * h)) >= h
            is_q = second == fwd
            qt = jnp.where(is_q, q * el, 0.0).astype(BF16)
            kt = jnp.where(is_q, 0.0, k * el).astype(BF16)
            same = (row // (2 * h)) == (col // (2 * h))
            scores = scores + jnp.where(same, _nt(qt, kt), 0.0)
        return scores

    lbl = lbl_ref[...]
    mx = jnp.max(lbl, axis=0, keepdims=True)
    ex = jnp.exp(lbl - mx)
    sm = ex / jnp.sum(ex, axis=0, keepdims=True)
    lb_all = jnp.zeros_like(sm[0])
    for i in range(1, layer_o + 1):
        lb_all = lb_all + sm[i]
    lb = jnp.where(fwd, lb_all[0:1], lb_all[1:2])
    z = z_ref[...]
    log_sig = jnp.minimum(z, 0.0) - jnp.log1p(jnp.exp(-jnp.abs(z)))
    a_ = jnp.log1p(-lb) + log_sig
    b_ = jnp.log(lb)
    big = jnp.maximum(a_, b_)
    logf = big + jnp.log1p(jnp.exp(-jnp.abs(a_ - b_)))
    kh = (1.0 - lb) * jax.nn.sigmoid(-z)
    l1 = logf.astype(BF16)
    r1 = logf - l1.astype(F32)
    l2 = r1.astype(BF16)
    l3 = (r1 - l2.astype(F32)).astype(BF16)
    wide = REC_HEADS * HEAD_DIM
    dsum_all = _dot(amat, jnp.concatenate([l1, l2, l3], axis=1))
    dsum_all = dsum_all[:, :wide] + dsum_all[:, wide:2 * wide] + dsum_all[:, 2 * wide:]
    qh, vh = q_ref[...], v_ref[...]
    outs = []
    for h in range(REC_HEADS):
        sl = slice(h * HEAD_DIM, (h + 1) * HEAD_DIM)
        e = jnp.exp(dsum_all[:, sl])
        q, k = qh[:, sl], kh[:, sl]
        outs.append(mix_head(q, k, vh[:, sl], hier_scores(q, k, e),
                             e[REC_LEVELS * c:(REC_LEVELS + 1) * c], e[(REC_LEVELS + 1) * c:], h))
    rq, rk = rq_ref[...], rk_ref[...]
    rq_rot = _rope(rq, cos_ref[...], sin_ref[...], HEAD_DIM // 2)
    rk_rot = _rope(rk, cos_ref[...], sin_ref[...], HEAD_DIM // 2)
    rq = jnp.where(is_lat, rq_rot, rq)
    rk = jnp.where(is_lat, rk_rot, rk) * (HEAD_DIM ** -0.5)
    rv = rv_ref[...]
    gam = jnp.exp(jnp.where(fwd, rdec_ref[0:1], rdec_ref[1:2]))
    lag = jnp.where(fwd, row - col, col - row)
    lag_f = jnp.maximum(lag, 0).astype(F32)
    steps_q = jnp.where(fwd, row + 1, c - row).astype(F32)
    steps_k = jnp.where(fwd, c - 1 - row, row).astype(F32)
    for h in range(REC_HEADS):
        sl = slice(h * HEAD_DIM, (h + 1) * HEAD_DIM)
        g = gam[:, sl]
        q, k = rq[:, sl], rk[:, sl]
        scores = jnp.where(lag >= 0, _nt(q.astype(BF16), k.astype(BF16)) * jnp.exp(-g * lag_f), 0.0)
        outs.append(mix_head(q, k, rv[:, sl], scores, jnp.exp(-g * steps_q), jnp.exp(-g * steps_k),
                             REC_HEADS + h))
    o_ref[...] = jnp.concatenate(outs, axis=1)


def recurrences(cfg, layer_o, proj, amat, lb_logits, ret_decay_lanes, rope_cs, s0):
    c = REC_CHUNK
    n_chunks = cfg.t // c
    n_ctx_chunks = cfg.tc // c
    cpc, cpl = cfg.ctx_n // c, cfg.lat_n // c
    n_seq = cfg.ctx_b + cfg.lat_b

    def chunk_of(d, s):
        return s if d == 0 else n_chunks - 1 - s

    def seq_of(d, s):
        ch = chunk_of(d, s)
        return jnp.where(ch >= n_ctx_chunks, cfg.ctx_b + (ch - n_ctx_chunks) // cpl, ch // cpc)

    def lat_of(d, s):
        return jnp.maximum(seq_of(d, s) - cfg.ctx_b, 0)

    def lat_pos(d, s):
        ch = chunk_of(d, s)
        return jnp.where(ch >= n_ctx_chunks, (ch - n_ctx_chunks) % cpl, 0)

    def col(d, k):
        return pl.BlockSpec((c, 512), lambda s: (chunk_of(d, s), k))

    def cols(d):
        return [col(d, 0), col(d, 1 + d), col(d, 3), col(d, 5), col(d, 6), col(d, 7)]

    def rope(d):
        return [pl.BlockSpec((c, 512), lambda s: (lat_pos(d, s), 0))] * 2

    def state_in(d):
        return pl.BlockSpec((None, None, 2 * REC_HEADS, HEAD_DIM, HEAD_DIM), lambda s: (d, lat_of(d, s), 0, 0, 0))

    def state_out(d):
        return pl.BlockSpec((None, 2 * REC_HEADS, HEAD_DIM, HEAD_DIM), lambda s: (seq_of(d, s), 0, 0, 0))

    nrow = (REC_LEVELS + 2) * c
    cos_t, sin_t = rope_cs
    o_shape = jax.ShapeDtypeStruct((cfg.t, D_MODEL), F32)
    st_shape = jax.ShapeDtypeStruct((n_seq, 2 * REC_HEADS, HEAD_DIM, HEAD_DIM), F32)
    return pl.pallas_call(
        functools.partial(_rec_kernel, cfg, layer_o),
        grid=(n_chunks,),
        in_specs=cols(0) + cols(1) + [
            pl.BlockSpec((2, nrow, c), lambda s: (0, 0, 0)),
            pl.BlockSpec((N_ODD, 2, 512), lambda s: (0, 0, 0)),
            pl.BlockSpec((None, 2, 512), lambda s: (0, 0, 0))] + rope(0) + rope(1) + [state_in(0), state_in(1)],
        out_specs=[pl.BlockSpec((c, D_MODEL), lambda s: (chunk_of(0, s), 0)),
                   pl.BlockSpec((c, D_MODEL), lambda s: (chunk_of(1, s), 0)),
                   state_out(0), state_out(1)],
        out_shape=[o_shape, o_shape, st_shape, st_shape],
        compiler_params=_cp(("arbitrary",)),
        name="recurrences",
    )(*([proj] * 12), amat, lb_logits, ret_decay_lanes, cos_t, sin_t, cos_t, sin_t, s0, s0)


def _odd_out_kernel(x_ref, of_ref, ob_ref, cg_ref, rg_ref, hw_ref, rw_ref, wo_ref, g_ref, o_ref):
    o = of_ref[...] + ob_ref[...]
    hw, rw = hw_ref[...], rw_ref[...]
    parts = []
    for h in range(REC_HEADS):
        sl = slice(h * HEAD_DIM, (h + 1) * HEAD_DIM)
        oh = o[:, sl]
        parts.append(oh * lax.rsqrt(jnp.mean(oh * oh, -1, keepdims=True) + EPS) * hw[:, sl])
    oc = jnp.concatenate(parts, axis=1) * jax.nn.sigmoid(cg_ref[...])
    parts = []
    for h in range(REC_HEADS):
        sl = slice(h * HEAD_DIM, (h + 1) * HEAD_DIM)
        oh = o[:, 512 + h * HEAD_DIM:512 + (h + 1) * HEAD_DIM]
        oh = oh - jnp.mean(oh, -1, keepdims=True)
        parts.append(oh * lax.rsqrt(jnp.mean(oh * oh, -1, keepdims=True) + EPS) * rw[:, sl])
    rg = rg_ref[...]
    orr = jnp.concatenate(parts, axis=1) * (rg * jax.nn.sigmoid(rg))
    cat = jnp.concatenate([oc, orr], axis=1).astype(BF16)
    o_ref[...] = x_ref[...] + g_ref[...] * _dot(cat, wo_ref[...])


def odd_out(cfg, x, o_f, o_b, proj, hgrn_norm_w, ret_norm_w, wo_bf16, mod5, layer):
    tile = lambda w, k=0: pl.BlockSpec((ROW_TILE, w), lambda i: (i, k))
    return pl.pallas_call(
        _odd_out_kernel,
        grid=(cfg.t // ROW_TILE,),
        in_specs=[tile(D_MODEL), tile(D_MODEL), tile(D_MODEL),
                  tile(512, 4), tile(512, 8),
                  pl.BlockSpec((1, 512), lambda i: (0, 0)), pl.BlockSpec((1, 512), lambda i: (0, 0)),
                  pl.BlockSpec((D_MODEL, D_MODEL), lambda i: (0, 0)),
                  _mod_spec(cfg, layer, 2)],
        out_specs=tile(D_MODEL),
        out_shape=jax.ShapeDtypeStruct((cfg.t, D_MODEL), F32),
        compiler_params=_cp(("parallel",)),
        name="odd_out",
    )(x, o_f, o_b, proj, proj, hgrn_norm_w.reshape(1, 512), ret_norm_w.reshape(1, 512), wo_bf16, mod5)


def _norm_logits_kernel(x_ref, nw_ref, sh_ref, sc_ref, wr_ref, h_ref, lg_ref):
    h = (_rms(x_ref[...], nw_ref[...]) * (1.0 + sc_ref[...]) + sh_ref[...]).astype(BF16)
    h_ref[...] = h
    lg_ref[...] = _nt(wr_ref[...], h)


def moe_norm_logits(cfg, x, nw, mod5, layer, wr_t_bf16):
    return pl.pallas_call(
        _norm_logits_kernel,
        grid=(cfg.t // ROW_TILE,),
        in_specs=[pl.BlockSpec((ROW_TILE, D_MODEL), lambda i: (i, 0)),
                  pl.BlockSpec((1, D_MODEL), lambda i: (0, 0)),
                  _mod_spec(cfg, layer, 3), _mod_spec(cfg, layer, 4),
                  pl.BlockSpec((N_EXPERTS, D_MODEL), lambda i: (0, 0))],
        out_specs=[pl.BlockSpec((ROW_TILE, D_MODEL), lambda i: (i, 0)),
                   pl.BlockSpec((N_EXPERTS, ROW_TILE), lambda i: (0, i))],
        out_shape=[jax.ShapeDtypeStruct((cfg.t, D_MODEL), BF16),
                   jax.ShapeDtypeStruct((N_EXPERTS, cfg.t), F32)],
        compiler_params=_cp(("parallel",)),
        name="moe_norm_logits",
    )(x, nw.reshape(1, D_MODEL), mod5, mod5, wr_t_bf16)


def _route_kernel(cfg, lg_ref, pos_ref, aff_ref, cj_ref):
    gs = cfg.gs
    g = pl.program_id(0)
    logits = lg_ref[...]
    mx = jnp.max(logits, axis=0, keepdims=True)
    ex = jnp.exp(logits - mx)
    aff = ex / jnp.sum(ex, axis=0, keepdims=True)
    aff_ref[...] = aff

    def seg_sum(x, seg):
        parts = []
        for r in range(gs // seg):
            s = jnp.sum(x[:, r * seg:(r + 1) * seg], axis=1, keepdims=True)
            parts.append(jnp.broadcast_to(s, (N_EXPERTS, seg)))
        return parts[0] if len(parts) == 1 else jnp.concatenate(parts, axis=1)

    tri = (lax.broadcasted_iota(I32, (LANES, LANES), 0) < lax.broadcasted_iota(I32, (LANES, LANES), 1)).astype(BF16)
    lane = lax.broadcasted_iota(I32, (N_EXPERTS, LANES), 1)

    def prefix(mask, seg):
        parts = []
        carry = jnp.zeros((N_EXPERTS, 1), F32)
        starts = jnp.zeros((N_EXPERTS, LANES), F32)
        for j in range(gs // LANES):
            if (j * LANES) % seg == 0:
                carry = jnp.zeros((N_EXPERTS, 1), F32)
            m = mask[:, j * LANES:(j + 1) * LANES]
            starts = starts + jnp.where(lane == j, carry, 0.0)
            parts.append(_dot(m.astype(BF16), tri) + carry)
            carry = carry + jnp.sum(m, axis=1, keepdims=True)
        starts = starts + jnp.where(lane == gs // LANES, carry, 0.0)
        return jnp.concatenate(parts, axis=1), starts

    def select(seg, cap):
        def at_least_cap(mid):
            return seg_sum((aff >= mid).astype(F32), seg) >= cap

        def halve_bits(_, lohi):
            lo, hi = lohi
            mid = lo + ((hi - lo) >> 1)
            ok = at_least_cap(pltpu.bitcast(mid, F32))
            return jnp.where(ok, mid, lo), jnp.where(ok, hi, mid)

        def halve_value(_, lohi):
            lo, hi = lohi
            mid = 0.5 * (lo + hi)
            ok = at_least_cap(mid)
            return jnp.where(ok, mid, lo), jnp.where(ok, hi, mid)

        lo0 = jnp.zeros((N_EXPERTS, gs), I32)
        hi0 = jnp.full((N_EXPERTS, gs), F32_INF_BITS, I32)
        lo_b, hi_b = lax.fori_loop(0, 31, halve_bits, (lo0, hi0))
        lo_f = jnp.where(lo_b < F32_MIN_NORMAL_BITS, 0.0, pltpu.bitcast(lo_b, F32))
        lo_f, hi_f = lax.fori_loop(0, 30, halve_value, (lo_f, pltpu.bitcast(hi_b, F32)))
        above = aff >= hi_f
        tied = jnp.logical_and(aff >= lo_f, aff < hi_f)
        need = cap - seg_sum(above.astype(F32), seg)
        before, _ = prefix(tied.astype(F32), seg)
        return jnp.logical_or(above, jnp.logical_and(tied, before < need))

    def finish(sel):
        self_f = sel.astype(F32)
        slot, starts = prefix(self_f, gs)
        pos_ref[...] = jnp.where(sel, slot.astype(I32), -1)
        cj_ref[...] = starts.astype(I32)

    @pl.when(g == 0)
    def _():
        finish(select(cfg.ctx_n, CAPACITY_FACTOR * cfg.ctx_n // N_EXPERTS))

    @pl.when(g > 0)
    def _():
        finish(select(cfg.lat_n, CAPACITY_FACTOR * cfg.lat_n // N_EXPERTS))


def moe_route(cfg, logits):
    gs = cfg.gs
    return pl.pallas_call(
        functools.partial(_route_kernel, cfg),
        grid=(cfg.groups,),
        in_specs=[pl.BlockSpec((N_EXPERTS, gs), lambda g: (0, g))],
        out_specs=[pl.BlockSpec((None, N_EXPERTS, gs), lambda g: (g, 0, 0)),
                   pl.BlockSpec((None, N_EXPERTS, gs), lambda g: (g, 0, 0)),
                   pl.BlockSpec((None, N_EXPERTS, LANES), lambda g: (g, 0, 0))],
        out_shape=[jax.ShapeDtypeStruct((cfg.groups, N_EXPERTS, gs), I32),
                   jax.ShapeDtypeStruct((cfg.groups, N_EXPERTS, gs), F32),
                   jax.ShapeDtypeStruct((cfg.groups, N_EXPERTS, LANES), I32)],
        compiler_params=_cp(("parallel",)),
        name="moe_route",
    )(logits)


def _slot_block(cfg):
    return min(LANES, cfg.slots)


GATHER_TOKENS = 2 * LANES


def _gather_kernel(cfg, cj_ref, h_ref, pos_ref, aff_ref, x_ref, gate_ref, acc_ref, gacc_ref):
    g, e = pl.program_id(0), pl.program_id(1)
    sb = _slot_block(cfg)
    slots = cfg.slots
    acc_ref[...] = jnp.zeros_like(acc_ref)
    gacc_ref[...] = jnp.zeros_like(gacc_ref)
    base = (g * N_EXPERTS + e) * LANES
    nt = cfg.gs // LANES

    def window(tb):
        return jnp.minimum((cj_ref[base + tb] // SUBLANES) * SUBLANES, slots - sb)

    fits = functools.reduce(jnp.logical_and,
                            [cj_ref[base + tb + 1] - window(tb) <= sb for tb in range(nt)])

    @pl.when(fits)
    def _():
        sub = lax.broadcasted_iota(I32, (sb, LANES), 0)
        tw = GATHER_TOKENS
        for tb2 in range(cfg.gs // tw):
            p2 = pos_ref[pl.ds(e, 1), tb2 * tw:(tb2 + 1) * tw]
            a2 = aff_ref[pl.ds(e, 1), tb2 * tw:(tb2 + 1) * tw]
            for half in range(tw // LANES):
                tb = tb2 * (tw // LANES) + half
                ws = pl.multiple_of(window(tb), SUBLANES)
                p = p2[:, half * LANES:(half + 1) * LANES]
                hit = jnp.broadcast_to(p, (sb, LANES)) == (sub + ws)
                acc_ref[pl.ds(ws, sb), :] += _dot(hit.astype(BF16), h_ref[tb * LANES:(tb + 1) * LANES, :])
                a = a2[:, half * LANES:(half + 1) * LANES]
                gacc_ref[pl.ds(ws, sb), :] += jnp.sum(jnp.where(hit, a, 0.0), axis=1, keepdims=True)

    @pl.when(jnp.logical_not(fits))
    def _():
        tw = GATHER_TOKENS
        sub = lax.broadcasted_iota(I32, (sb, tw), 0)
        for tb in range(cfg.gs // tw):
            s0, s1 = cj_ref[base + tb * (tw // LANES)], cj_ref[base + (tb + 1) * (tw // LANES)]
            for jb in range(slots // sb):
                @pl.when((s1 > s0) & (s0 < (jb + 1) * sb) & (s1 > jb * sb))
                def _():
                    p = pos_ref[pl.ds(e, 1), tb * tw:(tb + 1) * tw]
                    hit = jnp.broadcast_to(p, (sb, tw)) == (sub + jb * sb)
                    acc_ref[jb * sb:(jb + 1) * sb, :] += _dot(hit.astype(BF16), h_ref[tb * tw:(tb + 1) * tw, :])
                    a = aff_ref[pl.ds(e, 1), tb * tw:(tb + 1) * tw]
                    gacc_ref[jb * sb:(jb + 1) * sb, :] += jnp.sum(jnp.where(hit, a, 0.0), axis=1, keepdims=True)

    x_ref[...] = acc_ref[...].astype(BF16)
    gate_ref[...] = gacc_ref[...]


def moe_gather(cfg, cj, h, pos, aff):
    gs, slots = cfg.gs, cfg.slots
    grid_spec = pltpu.PrefetchScalarGridSpec(
        num_scalar_prefetch=1,
        grid=(cfg.groups, N_EXPERTS),
        in_specs=[pl.BlockSpec((gs, D_MODEL), lambda g, e, cj: (g, 0)),
                  pl.BlockSpec((None, N_EXPERTS, gs), lambda g, e, cj: (g, 0, 0)),
                  pl.BlockSpec((None, N_EXPERTS, gs), lambda g, e, cj: (g, 0, 0))],
        out_specs=[pl.BlockSpec((None, slots, D_MODEL), lambda g, e, cj: (e, g, 0)),
                   pl.BlockSpec((None, slots, 1), lambda g, e, cj: (e, g, 0))],
        scratch_shapes=[pltpu.VMEM((slots, D_MODEL), F32), pltpu.VMEM((slots, 1), F32)])
    return pl.pallas_call(
        functools.partial(_gather_kernel, cfg),
        grid_spec=grid_spec,
        out_shape=[jax.ShapeDtypeStruct((N_EXPERTS, cfg.groups * slots, D_MODEL), BF16),
                   jax.ShapeDtypeStruct((N_EXPERTS, cfg.groups * slots, 1), F32)],
        compiler_params=_cp(("parallel", "arbitrary")),
        name="moe_gather",
    )(cj.reshape(-1), h, pos, aff)


def _ffn_kernel(rows, x_ref, gate_ref, w1_ref, w3_ref, w2_ref, y_ref, acc_ref, w1b, w3b, w2b):
    c = pl.program_id(1)
    w1b[...] = w1_ref[...].astype(BF16)
    w3b[...] = w3_ref[...].astype(BF16)
    w2b[...] = w2_ref[...].astype(BF16)
    rt = min(FFN_ROWS, rows)

    def body(i, _):
        r = pl.multiple_of(i * rt, rt)
        x = x_ref[pl.ds(r, rt), :]
        a = _dot(x, w1b[...])
        b = _dot(x, w3b[...])
        hid = (a * jax.nn.sigmoid(a) * b).astype(BF16)
        y = _dot(hid, w2b[...])

        @pl.when(c == 0)
        def _():
            acc_ref[pl.ds(r, rt), :] = y

        @pl.when(c > 0)
        def _():
            acc_ref[pl.ds(r, rt), :] += y

        return 0

    lax.fori_loop(0, rows // rt, body, 0)

    @pl.when(c == pl.num_programs(1) - 1)
    def _():
        y_ref[...] = (acc_ref[...] * gate_ref[...]).astype(BF16)


def moe_ffn(cfg, layer, xg, gate, w1, w3, w2):
    rows = cfg.groups * cfg.slots
    nff = EXPERT_FF // FF_TILE
    return pl.pallas_call(
        functools.partial(_ffn_kernel, rows),
        grid=(N_EXPERTS, nff),
        in_specs=[pl.BlockSpec((None, rows, D_MODEL), lambda e, c: (e, 0, 0)),
                  pl.BlockSpec((None, rows, 1), lambda e, c: (e, 0, 0)),
                  pl.BlockSpec((None, None, D_MODEL, FF_TILE), lambda e, c: (layer, e, 0, c)),
                  pl.BlockSpec((None, None, D_MODEL, FF_TILE), lambda e, c: (layer, e, 0, c)),
                  pl.BlockSpec((None, None, FF_TILE, D_MODEL), lambda e, c: (layer, e, c, 0))],
        out_specs=pl.BlockSpec((None, rows, D_MODEL), lambda e, c: (e, 0, 0)),
        out_shape=jax.ShapeDtypeStruct((N_EXPERTS, rows, D_MODEL), BF16),
        scratch_shapes=[pltpu.VMEM((rows, D_MODEL), F32),
                        pltpu.VMEM((D_MODEL, FF_TILE), BF16), pltpu.VMEM((D_MODEL, FF_TILE), BF16),
                        pltpu.VMEM((FF_TILE, D_MODEL), BF16)],
        compiler_params=_cp(("parallel", "arbitrary")),
        name="moe_ffn",
    )(xg, gate, w1, w3, w2)


def _scatter_kernel(cfg, final, cj_ref, x_ref, pos_ref, y_ref, g_ref, fw_ref, o_ref):
    g, tb = pl.program_id(0), pl.program_id(1)
    sb = _slot_block(cfg)
    win = min(2 * sb, cfg.slots)
    sub = lax.broadcasted_iota(I32, (win, LANES), 0)
    acc = jnp.zeros((LANES, D_MODEL), F32)
    for e in range(N_EXPERTS):
        s0 = cj_ref[(g * N_EXPERTS + e) * LANES + tb]
        ws = pl.multiple_of(jnp.minimum((s0 // sb) * sb, cfg.slots - win), sb)
        hit = jnp.broadcast_to(pos_ref[e:e + 1, :], (win, LANES)) == (sub + ws)
        hit_t = jnp.where(hit, 1.0, 0.0).T.astype(BF16)
        acc = acc + _dot(hit_t, y_ref[e, pl.ds(ws, win), :])
    out = x_ref[...] + g_ref[...] * acc
    if final:
        out = _rms(out, fw_ref[...])
    o_ref[...] = out


def moe_scatter(cfg, layer, cj, x, pos, y, mod5, final_w, final):
    gs, slots = cfg.gs, cfg.slots
    nt = gs // LANES
    grid_spec = pltpu.PrefetchScalarGridSpec(
        num_scalar_prefetch=1,
        grid=(cfg.groups, nt),
        in_specs=[pl.BlockSpec((LANES, D_MODEL), lambda g, tb, cj: (g * nt + tb, 0)),
                  pl.BlockSpec((None, N_EXPERTS, LANES), lambda g, tb, cj: (g, 0, tb)),
                  pl.BlockSpec((N_EXPERTS, slots, D_MODEL), lambda g, tb, cj: (0, g, 0)),
                  pl.BlockSpec((None, None, None, 1, D_MODEL), lambda g, tb, cj: (layer, g, 5, 0, 0)),
                  pl.BlockSpec((1, D_MODEL), lambda g, tb, cj: (0, 0))],
        out_specs=pl.BlockSpec((LANES, D_MODEL), lambda g, tb, cj: (g * nt + tb, 0)))
    return pl.pallas_call(
        functools.partial(_scatter_kernel, cfg, final),
        grid_spec=grid_spec,
        out_shape=jax.ShapeDtypeStruct((cfg.t, D_MODEL), F32),
        compiler_params=_cp(("parallel", "arbitrary")),
        name="moe_scatter",
    )(cj.reshape(-1), x, pos, y, mod5, final_w.reshape(1, D_MODEL))


def _forward(cfg, x_prompt, x_sample, state_s5, cache_k, cache_v, state_hgrn, state_ret, c, c_ctx, p):
    assert cfg.tc == cfg.lat_n and cfg.ctx_n % ROW_TILE == 0 and cfg.lat_n % (GRID_W * 2) == 0
    x = jnp.concatenate([x_prompt.reshape(cfg.tc, D_MODEL), x_sample.reshape(-1, D_MODEL)], axis=0)
    conds = jnp.zeros((8, D_MODEL), F32).at[0].set(c_ctx).at[1:1 + cfg.lat_b].set(c)
    mod = adaln_table(conds, p['mod_w'], p['mod_b'])
    mod5 = mod[:, :cfg.groups].reshape(DEPTH, cfg.groups, 6, 1, D_MODEL)
    rope_att = _rope_tables(cfg.lat_n, ATT_HEAD_DIM, 512)
    rope_ret = _rope_tables(cfg.lat_n, HEAD_DIM, 512)
    amat = jnp.asarray(_rec_tables(), BF16)
    s5_fin, k_out, v_out, hgrn_fin, ret_fin = [], [], [], [], []
    for layer in range(DEPTH):
        if layer % 2 == 0:
            e = layer // 2
            proj = norm_mod_matmul(cfg, x, p['norm1_w'][layer], mod5, layer, p['ab_w_in'][e].astype(BF16))
            s5_tabs = _s5_tables(p['s5_lambda_re'][e], p['s5_lambda_im'][e], p['s5_log_dt'][e],
                                 p['s5_b_re'][e], p['s5_b_im'][e], p['s5_c_re'][e], p['s5_c_im'][e])
            h0 = state_s5[:, e].reshape(cfg.lat_b, 4, S5_PAIRS, 2 * S5_STATE).transpose(2, 0, 1, 3)
            yd, fin = s5_mix(cfg, proj, *s5_tabs, h0)
            s5_fin.append(fin.transpose(2, 1, 0, 3).reshape(cfg.ctx_b, 2, 2, S5_GROUPS, S5_STATE))
            ck = cache_k[:, e].reshape(cfg.lat_b, cfg.past, LANES)
            cv = cache_v[:, e].reshape(cfg.lat_b, cfg.past, LANES)
            yatt = attention(cfg, proj, p['attn_sink'][e], ck, cv, rope_att)
            k_out.append(proj[:cfg.tc, 1024:1152].reshape(cfg.ctx_b, cfg.ctx_n, ATT_KV_HEADS, ATT_HEAD_DIM))
            v_out.append(proj[:cfg.tc, 1152:1280].reshape(cfg.ctx_b, cfg.ctx_n, ATT_KV_HEADS, ATT_HEAD_DIM))
            x = even_out(cfg, x, proj, yd, yatt, p['s5_d'][e], p['s5_glu_w'][e].astype(BF16),
                         p['ab_w_out'][e].astype(BF16), mod5, layer)
        else:
            o = layer // 2
            proj = norm_mod_matmul(cfg, x, p['norm1_w'][layer], mod5, layer, p['cd_w_in'][o].astype(BF16))
            s0 = jnp.concatenate([state_hgrn[:, o], state_ret[:, o]], axis=2).transpose(1, 0, 2, 4, 3)
            rdec = jnp.repeat(p['ret_decay'][o], HEAD_DIM, axis=-1)
            o_f, o_b, st_f, st_b = recurrences(cfg, o, proj, amat, p['hgrn_lb_logits'], rdec[None], rope_ret, s0)
            st = jnp.stack([st_f[:cfg.ctx_b], st_b[:cfg.ctx_b]], axis=1).transpose(0, 1, 2, 4, 3)
            hgrn_fin.append(st[:, :, :REC_HEADS])
            ret_fin.append(st[:, :, REC_HEADS:])
            x = odd_out(cfg, x, o_f, o_b, proj, p['hgrn_norm_w'][o], p['ret_norm_w'][o],
                        p['cd_w_out'][o].astype(BF16), mod5, layer)
        h2, logits = moe_norm_logits(cfg, x, p['norm2_w'][layer], mod5, layer,
                                     p['router_w'][layer].T.astype(BF16))
        pos, aff, cj = moe_route(cfg, logits)
        xg, gate = moe_gather(cfg, cj, h2, pos, aff)
        yg = moe_ffn(cfg, layer, xg, gate, p['moe_w1'], p['moe_w3'], p['moe_w2'])
        x = moe_scatter(cfg, layer, cj, x, pos, yg, mod5, p['final_norm_w'], layer == DEPTH - 1)
    y_prompt = x[:cfg.tc].reshape(cfg.ctx_b, cfg.ctx_n, D_MODEL)
    y_sample = x[cfg.tc:].reshape(cfg.lat_b, cfg.lat_n, D_MODEL)
    return (y_prompt, y_sample, jnp.stack(s5_fin, axis=1), jnp.stack(k_out, axis=1), jnp.stack(v_out, axis=1),
            jnp.stack(hgrn_fin, axis=1), jnp.stack(ret_fin, axis=1))


def kernel(x_prompt, x_sample, state_s5, cache_k, cache_v, state_hgrn, state_ret, c, c_ctx, mod_w, mod_b, norm1_w, norm2_w, final_norm_w, ab_w_in, ab_w_out, s5_lambda_re, s5_lambda_im, s5_log_dt, s5_b_re, s5_b_im, s5_c_re, s5_c_im, s5_d, s5_glu_w, attn_sink, cd_w_in, cd_w_out, hgrn_lb_logits, hgrn_norm_w, ret_decay, ret_norm_w, router_w, moe_w1, moe_w3, moe_w2):
    cfg = Cfg(ctx_b=x_prompt.shape[0], ctx_n=x_prompt.shape[1], lat_b=x_sample.shape[0],
              lat_n=x_sample.shape[1], past=cache_k.shape[2])
    p = dict(mod_w=mod_w, mod_b=mod_b, norm1_w=norm1_w, norm2_w=norm2_w, final_norm_w=final_norm_w,
             ab_w_in=ab_w_in, ab_w_out=ab_w_out, s5_lambda_re=s5_lambda_re, s5_lambda_im=s5_lambda_im,
             s5_log_dt=s5_log_dt, s5_b_re=s5_b_re, s5_b_im=s5_b_im, s5_c_re=s5_c_re, s5_c_im=s5_c_im,
             s5_d=s5_d, s5_glu_w=s5_glu_w, attn_sink=attn_sink, cd_w_in=cd_w_in, cd_w_out=cd_w_out,
             hgrn_lb_logits=hgrn_lb_logits, hgrn_norm_w=hgrn_norm_w, ret_decay=ret_decay,
             ret_norm_w=ret_norm_w, router_w=router_w, moe_w1=moe_w1, moe_w3=moe_w3, moe_w2=moe_w2)
    return _forward(cfg, x_prompt, x_sample, state_s5, cache_k, cache_v, state_hgrn, state_ret, c, c_ctx, p)
```
